```python
import math
import jax, jax.numpy as jnp
from jax import lax
import numpy as np

D_MODEL = 2048
BATCH = 1
SEQ = 16384
DEPTH = 1
DEC_BATCH = 16
DEC_SEQ = 16
PAST_LEN = 4096

CHUNK = 64
SSM_EXPAND = 2
SSM_INNER = SSM_EXPAND * D_MODEL
SSM_HEAD_DIM = 64
SSM_HEADS = SSM_INNER // SSM_HEAD_DIM
SSM_GROUPS = 8
SSM_STATE = 128
CONV_WIDTH = 4
CONV_DIM = SSM_INNER + 2 * SSM_GROUPS * SSM_STATE
MLA_HEADS = 16
Q_LORA = 512
KV_LORA = 512
QK_NOPE = 128
QK_ROPE = 64
V_DIM = 128
MLA_INNER = MLA_HEADS * V_DIM
ROPE_THETA = 10000.0
Q_BLOCK = 128
EPS = 1e-6
IN_SIZES = (SSM_INNER, CONV_DIM, SSM_HEADS, Q_LORA, KV_LORA, QK_ROPE, MLA_INNER, D_MODEL, D_MODEL)
IN_DIM = sum(IN_SIZES)

kernel_name = 'hybrid_ssd_mla_streaming_step'


def _rmsnorm(x, w):
    xf = x.astype(jnp.float32)
    y = xf * lax.rsqrt(jnp.mean(xf * xf, axis=-1, keepdims=True) + EPS)
    return (y * w.astype(jnp.float32)).astype(x.dtype)


def _rope(x, pos):
    half = QK_ROPE // 2
    inv = 1.0 / (ROPE_THETA ** (jnp.arange(half, dtype=jnp.float32) * (2.0 / QK_ROPE)))
    ang = pos.astype(jnp.float32)[:, None] * inv[None, :]
    shape = (1, pos.shape[0]) + (1,) * (x.ndim - 3) + (half,)
    cos = jnp.cos(ang).reshape(shape)
    sin = jnp.sin(ang).reshape(shape)
    xf = x.astype(jnp.float32)
    x1, x2 = xf[..., :half], xf[..., half:]
    return jnp.concatenate([x1 * cos - x2 * sin, x1 * sin + x2 * cos], axis=-1).astype(x.dtype)


def _causal_conv(xbc, conv_prev, conv_w, conv_b):
    l = xbc.shape[1]
    xpad = jnp.concatenate([conv_prev.astype(xbc.dtype), xbc], axis=1)
    out = conv_b + xpad[:, 0:l] * conv_w[0]
    for k in range(1, CONV_WIDTH):
        out = out + xpad[:, k:k + l] * conv_w[k]
    return jax.nn.silu(out), xpad[:, -(CONV_WIDTH - 1):]


def _ssd(xh, dt, a, bm, cm, s0):
    b, l, h, p = xh.shape
    g, n = bm.shape[2], bm.shape[3]
    hg = h // g
    q = min(CHUNK, l)
    nc = l // q
    f32 = jnp.float32
    x = xh.astype(f32).reshape(b, nc, q, g, hg, p)
    dtc = dt.reshape(b, nc, q, g, hg)
    bc = bm.astype(f32).reshape(b, nc, q, g, n)
    cc = cm.astype(f32).reshape(b, nc, q, g, n)
    acs = jnp.cumsum(dtc * a.reshape(g, hg), axis=2)
    causal = jnp.tril(jnp.ones((q, q), dtype=bool))[None, None, :, :, None, None]
    seg = acs[:, :, :, None] - acs[:, :, None, :]
    decay = jnp.exp(jnp.where(causal, seg, -jnp.inf))
    cb = jnp.einsum('bcign,bcjgn->bcijg', cc, bc)
    wts = cb[..., None] * decay * dtc[:, :, None]
    y_diag = jnp.einsum('bcijgh,bcjghp->bcighp', wts, x)
    tail = jnp.exp(acs[:, :, -1:] - acs) * dtc
    chunk_states = jnp.einsum('bcjgn,bcjghp->bcghpn', bc, x * tail[..., None])
    chunk_decay = jnp.exp(acs[:, :, -1])

    def step(s, inp):
        st, dec = inp
        return s * dec[..., None, None] + st, s

    s_init = s0.astype(f32).reshape(b, g, hg, p, n)
    s_final, s_prev = lax.scan(step, s_init, (jnp.moveaxis(chunk_states, 1, 0), jnp.moveaxis(chunk_decay, 1, 0)))
    s_prev = jnp.moveaxis(s_prev, 0, 1)
    y_off = jnp.einsum('bcign,bcghpn->bcighp', cc, s_prev) * jnp.exp(acs)[..., None]
    y = (y_diag + y_off).reshape(b, l, h, p)
    return y, s_final.reshape(b, h, p, n)


def _attend_prompt(q_nope, q_rope, ckv, kr, w_uk, w_uv):
    b, s = ckv.shape[0], ckv.shape[1]
    k_nope = jnp.einsum('bsc,chd->bshd', ckv, w_uk)
    v = jnp.einsum('bsc,chd->bshd', ckv, w_uv)
    key_chunk = jnp.arange(s) // CHUNK
    scale = (QK_NOPE + QK_ROPE) ** -0.5

    def block(start):
        qn = lax.dynamic_slice_in_dim(q_nope, start, Q_BLOCK, axis=1)
        qr = lax.dynamic_slice_in_dim(q_rope, start, Q_BLOCK, axis=1)
        sc = (jnp.einsum('bqhd,bkhd->bhqk', qn, k_nope).astype(jnp.float32)
              + jnp.einsum('bqhr,bkr->bhqk', qr, kr).astype(jnp.float32)) * scale
        q_chunk = (start + jnp.arange(Q_BLOCK)) // CHUNK
        mask = key_chunk[None, :] <= q_chunk[:, None]
        pr = jax.nn.softmax(jnp.where(mask, sc, -jnp.inf), axis=-1)
        return jnp.einsum('bhqk,bkhd->bqhd', pr.astype(v.dtype), v)

    starts = jnp.arange(s // Q_BLOCK) * Q_BLOCK
    o = lax.map(block, starts)
    return jnp.transpose(o, (1, 0, 2, 3, 4)).reshape(b, s, MLA_INNER)


def _attend_cached(q_nope, q_rope, ckv_all, kr_all, w_uk, w_uv):
    b, t = q_nope.shape[0], q_nope.shape[1]
    scale = (QK_NOPE + QK_ROPE) ** -0.5
    q_lat = jnp.einsum('bthd,chd->bthc', q_nope, w_uk)
    sc = (jnp.einsum('bthc,bkc->bhtk', q_lat, ckv_all).astype(jnp.float32)
          + jnp.einsum('bthr,bkr->bhtk', q_rope, kr_all).astype(jnp.float32)) * scale
    pr = jax.nn.softmax(sc, axis=-1)
    o_lat = jnp.einsum('bhtk,bkc->bthc', pr.astype(ckv_all.dtype), ckv_all)
    return jnp.einsum('bthc,chd->bthd', o_lat, w_uv).reshape(b, t, MLA_INNER)


def _layer(x, pos, conv_prev, ssm_prev, cache_kv, cache_kr, norm_in_w, w_in, conv_w, conv_b, dt_bias, a_log,
           d_skip, ssm_norm_w, w_ssm_out, q_norm_w, w_q_up, kv_norm_w, w_kv_up, w_mla_out, w_out):
    b, l, _ = x.shape
    f32 = jnp.float32
    h = _rmsnorm(x, norm_in_w)
    offsets = [int(o) for o in np.cumsum(IN_SIZES)[:-1]]
    z, xbc, dt_raw, cq, ckv_raw, kr_raw, g_mla, gate_ssm, gate_mla = jnp.split(h @ w_in, offsets, axis=-1)

    xbc_c, new_conv = _causal_conv(xbc, conv_prev, conv_w, conv_b)
    xs = xbc_c[..., :SSM_INNER].reshape(b, l, SSM_HEADS, SSM_HEAD_DIM)
    bm = xbc_c[..., SSM_INNER:SSM_INNER + SSM_GROUPS * SSM_STATE].reshape(b, l, SSM_GROUPS, SSM_STATE)
    cm = xbc_c[..., SSM_INNER + SSM_GROUPS * SSM_STATE:].reshape(b, l, SSM_GROUPS, SSM_STATE)
    dt = jax.nn.softplus(dt_raw.astype(f32) + dt_bias.astype(f32))
    a = -jnp.exp(a_log.astype(f32))
    y, new_ssm = _ssd(xs, dt, a, bm, cm, ssm_prev)
    y = (y + d_skip.astype(f32)[:, None] * xs.astype(f32)).reshape(b, l, SSM_INNER) * jax.nn.silu(z.astype(f32))
    yg = y.reshape(b, l, SSM_GROUPS, SSM_INNER // SSM_GROUPS)
    yg = yg * lax.rsqrt(jnp.mean(yg * yg, axis=-1, keepdims=True) + EPS)
    y = yg.reshape(b, l, SSM_INNER) * ssm_norm_w.astype(f32)
    y_ssm = y.astype(x.dtype) @ w_ssm_out

    q = (_rmsnorm(cq, q_norm_w) @ w_q_up).reshape(b, l, MLA_HEADS, QK_NOPE + QK_ROPE)
    q_nope = q[..., :QK_NOPE]
    q_rope = _rope(q[..., QK_NOPE:], pos)
    ckv = _rmsnorm(ckv_raw, kv_norm_w)
    kr = _rope(kr_raw, pos)
    w_kv = w_kv_up.reshape(KV_LORA, MLA_HEADS, QK_NOPE + V_DIM)
    w_uk, w_uv = w_kv[..., :QK_NOPE], w_kv[..., QK_NOPE:]
    if cache_kv is None:
        attn = _attend_prompt(q_nope, q_rope, ckv, kr, w_uk, w_uv)
    else:
        ckv_all = jnp.concatenate([cache_kv.astype(ckv.dtype), ckv], axis=1)
        kr_all = jnp.concatenate([cache_kr.astype(kr.dtype), kr], axis=1)
        attn = _attend_cached(q_nope, q_rope, ckv_all, kr_all, w_uk, w_uv)
    y_mla = (attn * jax.nn.silu(g_mla)) @ w_mla_out

    mixed = jax.nn.sigmoid(gate_ssm) * y_ssm + jax.nn.sigmoid(gate_mla) * y_mla
    x_out = x + mixed @ w_out
    return x_out, ckv, kr, new_ssm.astype(x.dtype), new_conv


def setup_inputs(seed: int = 0) -> dict:
    key = jax.random.key(seed)
    ks = jax.random.split(key, 24)

    def nrm(k, shape, scale):
        return jax.random.normal(k, shape, jnp.float32) * scale

    dt0 = jnp.exp(jax.random.uniform(ks[10], (DEPTH, SSM_HEADS), jnp.float32,
                                     minval=math.log(1e-3), maxval=math.log(1e-1)))
    return {
        'x_prompt': nrm(ks[0], (BATCH, SEQ, D_MODEL), 1.0),
        'x_sample': nrm(ks[1], (DEC_BATCH, DEC_SEQ, D_MODEL), 1.0),
        'cache_kv_latent': nrm(ks[2], (DEPTH, DEC_BATCH, PAST_LEN, KV_LORA), 1.0),
        'cache_k_rope': nrm(ks[3], (DEPTH, DEC_BATCH, PAST_LEN, QK_ROPE), 1.0),
        'state_ssm': nrm(ks[4], (DEPTH, DEC_BATCH, SSM_HEADS, SSM_HEAD_DIM, SSM_STATE), 0.1),
        'state_conv': nrm(ks[5], (DEPTH, DEC_BATCH, CONV_WIDTH - 1, CONV_DIM), 1.0),
        'norm_in_w': 1.0 + nrm(ks[6], (DEPTH, D_MODEL), 0.02),
        'w_in': nrm(ks[7], (DEPTH, D_MODEL, IN_DIM), D_MODEL ** -0.5),
        'conv_w': nrm(ks[8], (DEPTH, CONV_WIDTH, CONV_DIM), CONV_WIDTH ** -0.5),
        'conv_b': nrm(ks[9], (DEPTH, CONV_DIM), 0.02),
        'dt_bias': dt0 + jnp.log(-jnp.expm1(-dt0)),
        'a_log': jnp.log(jax.random.uniform(ks[11], (DEPTH, SSM_HEADS), jnp.float32, minval=1.0, maxval=16.0)),
        'd_skip': 1.0 + nrm(ks[12], (DEPTH, SSM_HEADS), 0.1),
        'ssm_norm_w': 1.0 + nrm(ks[13], (DEPTH, SSM_INNER), 0.02),
        'w_ssm_out': nrm(ks[14], (DEPTH, SSM_INNER, D_MODEL), SSM_INNER ** -0.5),
        'q_norm_w': 1.0 + nrm(ks[15], (DEPTH, Q_LORA), 0.02),
        'w_q_up': nrm(ks[16], (DEPTH, Q_LORA, MLA_HEADS * (QK_NOPE + QK_ROPE)), Q_LORA ** -0.5),
        'kv_norm_w': 1.0 + nrm(ks[17], (DEPTH, KV_LORA), 0.02),
        'w_kv_up': nrm(ks[18], (DEPTH, KV_LORA, MLA_HEADS * (QK_NOPE + V_DIM)), KV_LORA ** -0.5),
        'w_mla_out': nrm(ks[19], (DEPTH, MLA_INNER, D_MODEL), MLA_INNER ** -0.5),
        'w_out': nrm(ks[20], (DEPTH, D_MODEL, D_MODEL), D_MODEL ** -0.5),
        'final_norm_w': 1.0 + nrm(ks[21], (D_MODEL,), 0.02),
    }


def reference(x_prompt, x_sample, cache_kv_latent, cache_k_rope, state_ssm, state_conv, norm_in_w, w_in,
              conv_w, conv_b, dt_bias, a_log, d_skip, ssm_norm_w, w_ssm_out, q_norm_w, w_q_up, kv_norm_w,
              w_kv_up, w_mla_out, w_out, final_norm_w):
    b_p, s_p = x_prompt.shape[0], x_prompt.shape[1]
    pos_p = jnp.arange(s_p)
    pos_s = cache_kv_latent.shape[2] + jnp.arange(x_sample.shape[1])
    xp, xs = x_prompt, x_sample
    kvp, krp, ssmp, convp = [], [], [], []
    kvs, krs, ssms, convs = [], [], [], []
    for i in range(DEPTH):
        lw = dict(norm_in_w=norm_in_w[i], w_in=w_in[i], conv_w=conv_w[i], conv_b=conv_b[i], dt_bias=dt_bias[i],
                  a_log=a_log[i], d_skip=d_skip[i], ssm_norm_w=ssm_norm_w[i], w_ssm_out=w_ssm_out[i],
                  q_norm_w=q_norm_w[i], w_q_up=w_q_up[i], kv_norm_w=kv_norm_w[i], w_kv_up=w_kv_up[i],
                  w_mla_out=w_mla_out[i], w_out=w_out[i])
        conv0 = jnp.zeros((b_p, CONV_WIDTH - 1, CONV_DIM), x_prompt.dtype)
        ssm0 = jnp.zeros((b_p, SSM_HEADS, SSM_HEAD_DIM, SSM_STATE), x_prompt.dtype)
        xp, a1, a2, a3, a4 = _layer(xp, pos_p, conv0, ssm0, None, None, **lw)
        kvp.append(a1); krp.append(a2); ssmp.append(a3); convp.append(a4)
        xs, c1, c2, c3, c4 = _layer(xs, pos_s, state_conv[i], state_ssm[i], cache_kv_latent[i], cache_k_rope[i], **lw)
        kvs.append(c1); krs.append(c2); ssms.append(c3); convs.append(c4)
    y_prompt = _rmsnorm(xp, final_norm_w)
    y_sample = _rmsnorm(xs, final_norm_w)
    return (y_prompt, y_sample, jnp.stack(kvp), jnp.stack(krp), jnp.stack(ssmp), jnp.stack(convp),
            jnp.stack(kvs), jnp.stack(krs), jnp.stack(ssms), jnp.stack(convs))
```

```python
import functools
import math

import jax
import jax.numpy as jnp
import numpy as np
from jax import lax
from jax.experimental import pallas as pl
from jax.experimental.pallas import tpu as pltpu

F32 = jnp.float32
BF16 = jnp.bfloat16

D_MODEL = 2048
CHUNK = 64
SSM_INNER = 4096
SSM_HEAD_DIM = 64
SSM_HEADS = 64
SSM_GROUPS = 8
SSM_STATE = 128
CONV_WIDTH = 4
CONV_DIM = SSM_INNER + 2 * SSM_GROUPS * SSM_STATE
GROUP_LANES = SSM_INNER // SSM_GROUPS
MLA_HEADS = 16
Q_LORA = 512
KV_LORA = 512
QK_NOPE = 128
QK_ROPE = 64
V_DIM = 128
MLA_INNER = MLA_HEADS * V_DIM
ROPE_THETA = 10000.0
EPS = 1e-6
QK_PAD = 256
SCALE = (QK_NOPE + QK_ROPE) ** -0.5

LANES = 128
VMEM_LIMIT = 56 * 1024 * 1024

IN_TN = 512
SEG_Z = SSM_INNER
SEG_XBC = CONV_DIM
SEG_MLA = Q_LORA + KV_LORA + 512
SEG_G = MLA_INNER + 2 * D_MODEL
IN_SEGS = (SEG_Z, SEG_XBC, SEG_MLA, SEG_G)


def _cparams(sem, vmem=VMEM_LIMIT):
    return pltpu.CompilerParams(dimension_semantics=sem, vmem_limit_bytes=vmem)


def _const_spec(shape):
    nd = len(shape)
    return pl.BlockSpec(shape, lambda *_: (0,) * nd, pipeline_mode=pl.Buffered(1))


def _sigmoid(x):
    return 1.0 / (1.0 + jnp.exp(-x))


def _silu(x):
    return x * _sigmoid(x)


def _rms(x, w):
    return x * lax.rsqrt(jnp.mean(x * x, axis=-1, keepdims=True) + EPS) * w


def _dot(a, b):
    return jnp.dot(a, b, preferred_element_type=F32)


def _dot_nt(a, b):
    return lax.dot_general(a, b, (((1,), (1,)), ((), ())), preferred_element_type=F32)


def _dot_tn(a, b):
    return lax.dot_general(a, b, (((0,), (0,)), ((), ())), preferred_element_type=F32)


def _split3(v):
    hi = v.astype(BF16)
    r1 = v - hi.astype(F32)
    mid = r1.astype(BF16)
    lo = (r1 - mid.astype(F32)).astype(BF16)
    return hi, mid, lo


def _inproj_kernel(x_ref, nw_ref, w_ref, *rest, bounds):
    outs, h_ref = rest[:-1], rest[-1]
    j = pl.program_id(1)

    @pl.when(j == 0)
    def _():
        h_ref[...] = _rms(x_ref[...], nw_ref[...]).astype(BF16)

    for o_ref, (s, e) in zip(outs, bounds):
        @pl.when((j >= s) & (j < e))
        def _(o_ref=o_ref):
            o_ref[...] = _dot(h_ref[...], w_ref[...])


def _inproj(x, norm_w, w_all, tm):
    rows = x.shape[0]
    counts = [s // IN_TN for s in IN_SEGS]
    starts = np.cumsum([0] + counts)
    bounds = tuple((int(starts[k]), int(starts[k + 1])) for k in range(len(counts)))

    def omap(k):
        s, n = bounds[k][0], counts[k]
        return lambda i, j: (i, jnp.clip(j - s, 0, n - 1))

    return pl.pallas_call(
        functools.partial(_inproj_kernel, bounds=bounds),
        grid=(rows // tm, int(starts[-1])),
        in_specs=[
            pl.BlockSpec((tm, D_MODEL), lambda i, j: (i, 0)),
            pl.BlockSpec((1, D_MODEL), lambda i, j: (0, 0)),
            pl.BlockSpec((D_MODEL, IN_TN), lambda i, j: (0, j)),
        ],
        out_specs=[pl.BlockSpec((tm, IN_TN), omap(k)) for k in range(len(counts))],
        out_shape=[jax.ShapeDtypeStruct((rows, s), F32) for s in IN_SEGS],
        scratch_shapes=[pltpu.VMEM((tm, D_MODEL), BF16)],
        compiler_params=_cparams(("arbitrary", "arbitrary")),
        name="inproj",
    )(x, norm_w, w_all)


CONV_PAD = 8


def _ssd_kernel(xbc_ref, z_ref, sm_ref, cprev_ref, s0_ref, cw_ref, cb_ref, dtb_ref, alog_ref, dsk_ref,
                nw_ref, ep_ref, eq_ref, y_ref, sout_ref, cout_ref, st_ref, xpad_ref, xc_ref, exp_p_ref,
                exp_q_ref, yd_ref, *, q):
    c = pl.program_id(1)
    nc = pl.num_programs(1)
    hp = LANES // q
    tiles_per_group = (SSM_HEADS // SSM_GROUPS) // hp
    tile_w = hp * SSM_HEAD_DIM
    lo = CONV_PAD - (CONV_WIDTH - 1)

    @pl.when(c == 0)
    def _():
        st_ref[...] = s0_ref[0].T
        xpad_ref[lo:CONV_PAD, :] = cprev_ref[0]

    xpad_ref[CONV_PAD:CONV_PAD + q, :] = xbc_ref[0]
    for s in range(0, CONV_DIM, 1024):
        acc = cb_ref[:, s:s + 1024] + xpad_ref[lo:lo + q, s:s + 1024] * cw_ref[0:1, s:s + 1024]
        for k in range(1, CONV_WIDTH):
            acc = acc + xpad_ref[lo + k:lo + k + q, s:s + 1024] * cw_ref[k:k + 1, s:s + 1024]
        xc_ref[:, s:s + 1024] = _silu(acc)
    tail_rows = xpad_ref[CONV_PAD + q - (CONV_WIDTH - 1):CONV_PAD + q, :]
    xpad_ref[lo:CONV_PAD, :] = tail_rows

    xdt = sm_ref[0][:, 0:SSM_HEADS] + dtb_ref[...]
    dt = jnp.maximum(xdt, 0.0) + jnp.log1p(jnp.exp(-jnp.abs(xdt)))
    adt = dt * (-jnp.exp(alog_ref[...]))
    ri = lax.broadcasted_iota(jnp.int32, (q, q), 0)
    ci = lax.broadcasted_iota(jnp.int32, (q, q), 1)
    tril = jnp.where(ci <= ri, 1.0, 0.0).astype(BF16)
    h3 = _split3(adt)
    acs = _dot(tril, h3[0]) + _dot(tril, h3[1]) + _dot(tril, h3[2])

    st3 = _split3(jnp.concatenate([acs, dt], axis=0))
    exp_p_ref[...] = _dot(st3[0], ep_ref[...]) + _dot(st3[1], ep_ref[...]) + _dot(st3[2], ep_ref[...])
    if q == SSM_HEAD_DIM:
        xq_ref = exp_p_ref
    else:
        exp_q_ref[...] = _dot(st3[0], eq_ref[...]) + _dot(st3[1], eq_ref[...]) + _dot(st3[2], eq_ref[...])
        xq_ref = exp_q_ref

    r128 = lax.broadcasted_iota(jnp.int32, (q, LANES), 0)
    c128 = lax.broadcasted_iota(jnp.int32, (q, LANES), 1)
    key128 = c128 & (q - 1)
    causal = key128 <= r128
    diag = key128 == r128
    br = lax.broadcasted_iota(jnp.int32, (LANES, tile_w), 0)
    bc = lax.broadcasted_iota(jnp.int32, (LANES, tile_w), 1)
    blockdiag = (br // q) == (bc // SSM_HEAD_DIM)

    for g in range(SSM_GROUPS):
        gs = g * GROUP_LANES
        b_g = xc_ref[:, SSM_INNER + g * SSM_STATE:SSM_INNER + (g + 1) * SSM_STATE]
        c_g = xc_ref[:, SSM_INNER + (SSM_GROUPS + g) * SSM_STATE:SSM_INNER + (SSM_GROUPS + g + 1) * SSM_STATE]
        b_bf = b_g.astype(BF16)
        c_bf = c_g.astype(BF16)
        cb_t = _dot_nt(c_bf, jnp.concatenate([b_bf] * hp, axis=0))

        for tt in range(tiles_per_group):
            t = g * tiles_per_group + tt
            a_t = xq_ref[0:q, t * LANES:(t + 1) * LANES]
            d_t = xq_ref[q:2 * q, t * LANES:(t + 1) * LANES]
            a_key = jnp.sum(jnp.where(diag, a_t, 0.0), axis=0, keepdims=True)
            d_key = jnp.sum(jnp.where(diag, d_t, 0.0), axis=0, keepdims=True)
            w_t = jnp.exp(jnp.where(causal, a_t - a_key, -jnp.inf)) * d_key * cb_t
            x_t = xc_ref[:, t * tile_w:(t + 1) * tile_w]
            rhs = jnp.where(blockdiag, jnp.concatenate([x_t] * hp, axis=0), 0.0).astype(BF16)
            yd_ref[:, t * tile_w:(t + 1) * tile_w] = _dot(w_t.astype(BF16), rhs)

        a_p = exp_p_ref[0:q, gs:gs + GROUP_LANES]
        d_p = exp_p_ref[q:2 * q, gs:gs + GROUP_LANES]
        a_last = exp_p_ref[q - 1:q, gs:gs + GROUP_LANES]
        xs_g = xc_ref[:, gs:gs + GROUP_LANES]
        st_g = st_ref[:, gs:gs + GROUP_LANES]
        y_off = _dot(c_bf, st_g.astype(BF16)) * jnp.exp(a_p)
        xt = (xs_g * (jnp.exp(a_last - a_p) * d_p)).astype(BF16)
        st_ref[:, gs:gs + GROUP_LANES] = st_g * jnp.exp(a_last) + _dot_tn(b_bf, xt)

        y = yd_ref[:, gs:gs + GROUP_LANES] + y_off + dsk_ref[:, gs:gs + GROUP_LANES] * xs_g
        y = y * _silu(z_ref[0][:, gs:gs + GROUP_LANES])
        y = y * lax.rsqrt(jnp.mean(y * y, axis=-1, keepdims=True) + EPS)
        y_ref[0, :, gs:gs + GROUP_LANES] = (y * nw_ref[:, gs:gs + GROUP_LANES]).astype(y_ref.dtype)

    @pl.when(c == nc - 1)
    def _():
        sout_ref[0] = st_ref[...].T
        cout_ref[0] = tail_rows


def _ssd(xbc, z, mla, conv_prev, s0, p, q):
    nb, lb = xbc.shape[0], xbc.shape[1]
    nc = lb // q
    kq = SSM_HEADS * q
    in_specs = [
        pl.BlockSpec((1, q, CONV_DIM), lambda b, c: (b, c, 0)),
        pl.BlockSpec((1, q, SSM_INNER), lambda b, c: (b, c, 0)),
        pl.BlockSpec((1, q, 512), lambda b, c: (b, c, (Q_LORA + KV_LORA) // 512)),
        pl.BlockSpec((1, CONV_WIDTH - 1, CONV_DIM), lambda b, c: (b, 0, 0)),
        pl.BlockSpec((1, SSM_INNER, SSM_STATE), lambda b, c: (b, 0, 0)),
        _const_spec((CONV_WIDTH, CONV_DIM)),
        _const_spec((1, CONV_DIM)),
        _const_spec((1, SSM_HEADS)),
        _const_spec((1, SSM_HEADS)),
        _const_spec((1, SSM_INNER)),
        _const_spec((1, SSM_INNER)),
        _const_spec((SSM_HEADS, SSM_INNER)),
        _const_spec((SSM_HEADS, kq)),
    ]
    out_specs = [
        pl.BlockSpec((1, q, SSM_INNER), lambda b, c: (b, c, 0)),
        pl.BlockSpec((1, SSM_INNER, SSM_STATE), lambda b, c: (b, 0, 0)),
        pl.BlockSpec((1, CONV_WIDTH - 1, CONV_DIM), lambda b, c: (b, 0, 0)),
    ]
    out_shape = [
        jax.ShapeDtypeStruct((nb, lb, SSM_INNER), BF16),
        jax.ShapeDtypeStruct((nb, SSM_INNER, SSM_STATE), F32),
        jax.ShapeDtypeStruct((nb, CONV_WIDTH - 1, CONV_DIM), F32),
    ]
    scratch = [
        pltpu.VMEM((SSM_STATE, SSM_INNER), F32),
        pltpu.VMEM((CONV_PAD + q, CONV_DIM), F32),
        pltpu.VMEM((q, CONV_DIM), F32),
        pltpu.VMEM((2 * q, SSM_INNER), F32),
        pltpu.VMEM((2 * q, kq), F32),
        pltpu.VMEM((q, SSM_INNER), F32),
    ]
    return pl.pallas_call(
        functools.partial(_ssd_kernel, q=q),
        grid=(nb, nc),
        in_specs=in_specs,
        out_specs=out_specs,
        out_shape=out_shape,
        scratch_shapes=scratch,
        compiler_params=_cparams(("arbitrary", "arbitrary")),
        name="ssd",
    )(xbc, z, mla, conv_prev, s0, p["conv_w"], p["conv_b"], p["dt_bias"], p["a_log"], p["d_skip_p"],
      p["ssm_norm_w"], p["e_p"], p["e_q%d" % q])


def _mla_common(mla_ref, ct_ref, sn_ref, qnw_ref, kvnw_ref, wqa_ref, wqb_ref, ckv_ref, kr_ref):
    m = mla_ref[...]
    cqn = _rms(m[:, 0:Q_LORA], qnw_ref[...]).astype(BF16)
    ckv = _rms(m[:, Q_LORA:Q_LORA + KV_LORA], kvnw_ref[...])
    ckv_ref[...] = ckv
    small = m[:, Q_LORA + KV_LORA:]
    ct, sn = ct_ref[...], sn_ref[...]
    kr128 = small[:, 128:256] * ct + small[:, 256:384] * sn
    kr_ref[...] = kr128[:, 0:QK_ROPE]
    qa = _dot(cqn, wqa_ref[...])
    qb = _dot(cqn, wqb_ref[...])
    return ckv.astype(BF16), kr128, qa, qb, ct, sn


def _mla_prompt_kernel(mla_ref, ct_ref, sn_ref, qnw_ref, kvnw_ref, wqa_ref, wqb_ref, wk_ref, wv2_ref,
                       ckv_ref, kr_ref, q_ref, k_ref, v_ref):
    ckv_bf, kr128, qa, qb, ct, sn = _mla_common(mla_ref, ct_ref, sn_ref, qnw_ref, kvnw_ref, wqa_ref, wqb_ref,
                                                 ckv_ref, kr_ref)
    kn = _dot(ckv_bf, wk_ref[...])
    vv = _dot(ckv_bf, wv2_ref[...])
    kr_bf = kr128.astype(BF16)
    for h in range(MLA_HEADS):
        o = h * QK_PAD
        q_rope = qa[:, o + QK_NOPE:o + QK_PAD] * ct + qb[:, h * LANES:(h + 1) * LANES] * sn
        q_ref[h, :, 0:QK_NOPE] = (qa[:, o:o + QK_NOPE] * SCALE).astype(BF16)
        q_ref[h, :, QK_NOPE:QK_PAD] = (q_rope * SCALE).astype(BF16)
        k_ref[h, :, 0:QK_NOPE] = kn[:, h * QK_NOPE:(h + 1) * QK_NOPE].astype(BF16)
        k_ref[h, :, QK_NOPE:QK_PAD] = kr_bf
        v_ref[h] = vv[:, o:o + QK_PAD].astype(BF16)


def _mla_prompt(mla, ct, sn, p, tm):
    rows = mla.shape[0]
    row = lambda w: pl.BlockSpec((tm, w), lambda i: (i, 0))
    head = pl.BlockSpec((MLA_HEADS, tm, QK_PAD), lambda i: (0, i, 0))
    return pl.pallas_call(
        _mla_prompt_kernel,
        grid=(rows // tm,),
        in_specs=[row(SEG_MLA), row(LANES), row(LANES), _const_spec((1, Q_LORA)), _const_spec((1, KV_LORA)),
                  _const_spec((Q_LORA, MLA_HEADS * QK_PAD)), _const_spec((Q_LORA, MLA_HEADS * LANES)),
                  _const_spec((KV_LORA, MLA_HEADS * QK_NOPE)), _const_spec((KV_LORA, MLA_HEADS * QK_PAD))],
        out_specs=[row(KV_LORA), row(QK_ROPE), head, head, head],
        out_shape=[jax.ShapeDtypeStruct((rows, KV_LORA), F32), jax.ShapeDtypeStruct((rows, QK_ROPE), F32)]
        + [jax.ShapeDtypeStruct((MLA_HEADS, rows, QK_PAD), BF16)] * 3,
        compiler_params=_cparams(("arbitrary",)),
        name="mla_prep_prompt",
    )(mla, ct, sn, p["q_norm_w"], p["kv_norm_w"], p["wq_a"], p["wq_b"], p["w_uk"], p["w_uv2"])


def _mla_sample_kernel(mla_ref, ct_ref, sn_ref, qnw_ref, kvnw_ref, wqa_ref, wqb_ref, wk_ref,
                       ckv_ref, kr_ref, ql_ref, qr_ref):
    _, _, qa, qb, ct, sn = _mla_common(mla_ref, ct_ref, sn_ref, qnw_ref, kvnw_ref, wqa_ref, wqb_ref,
                                       ckv_ref, kr_ref)
    for h in range(MLA_HEADS):
        o = h * QK_PAD
        q_rope = qa[:, o + QK_NOPE:o + QK_PAD] * ct + qb[:, h * LANES:(h + 1) * LANES] * sn
        qr_ref[h] = (q_rope * SCALE).astype(BF16)
        q_nope = qa[:, o:o + QK_NOPE].astype(BF16)
        q_lat = _dot_nt(q_nope, wk_ref[:, h * QK_NOPE:(h + 1) * QK_NOPE])
        ql_ref[h] = (q_lat * SCALE).astype(BF16)


def _mla_sample(mla, ct, sn, p):
    rows = mla.shape[0]
    full = lambda w: pl.BlockSpec((rows, w), lambda i: (0, 0))
    return pl.pallas_call(
        _mla_sample_kernel,
        grid=(1,),
        in_specs=[full(SEG_MLA), full(LANES), full(LANES), _const_spec((1, Q_LORA)), _const_spec((1, KV_LORA)),
                  _const_spec((Q_LORA, MLA_HEADS * QK_PAD)), _const_spec((Q_LORA, MLA_HEADS * LANES)),
                  _const_spec((KV_LORA, MLA_HEADS * QK_NOPE))],
        out_specs=[full(KV_LORA), full(QK_ROPE),
                   pl.BlockSpec((MLA_HEADS, rows, KV_LORA), lambda i: (0, 0, 0)),
                   pl.BlockSpec((MLA_HEADS, rows, LANES), lambda i: (0, 0, 0))],
        out_shape=[jax.ShapeDtypeStruct((rows, KV_LORA), F32), jax.ShapeDtypeStruct((rows, QK_ROPE), F32),
                   jax.ShapeDtypeStruct((MLA_HEADS, rows, KV_LORA), BF16),
                   jax.ShapeDtypeStruct((MLA_HEADS, rows, LANES), BF16)],
        compiler_params=_cparams(("arbitrary",)),
        name="mla_prep_sample",
    )(mla, ct, sn, p["q_norm_w"], p["kv_norm_w"], p["wq_a"], p["wq_b"], p["w_uk"])


def _attn_kernel(qi_ref, kj_ref, q_ref, k_ref, v_ref, o_ref, m_ref, l_ref, acc_ref, *, tq, tk):
    t = pl.program_id(0)
    i = qi_ref[t]
    j = kj_ref[t]
    r = tq // tk
    pairs = MLA_HEADS // 2

    @pl.when(j == 0)
    def _():
        m_ref[...] = jnp.full(m_ref.shape, -jnp.inf, F32)
        l_ref[...] = jnp.zeros(l_ref.shape, F32)
        acc_ref[...] = jnp.zeros(acc_ref.shape, F32)

    def step(masked):
        if masked:
            rows = (i * tq + lax.broadcasted_iota(jnp.int32, (tq, tk), 0)) // CHUNK
            cols = (j * tk + lax.broadcasted_iota(jnp.int32, (tq, tk), 1)) // CHUNK
            visible = cols <= rows

        def body(g, carry):
            ps, alphas = [], []
            for e in range(2):
                h = 2 * g + e
                s = _dot_nt(q_ref[h], k_ref[h])
                if masked:
                    s = jnp.where(visible, s, -jnp.inf)
                m_prev = m_ref[h]
                m_new = jnp.maximum(m_prev, jnp.max(s, axis=1, keepdims=True))
                alpha = jnp.exp(m_prev - m_new)
                pr = jnp.exp(s - m_new[:, 0:1])
                l_ref[h] = alpha * l_ref[h] + jnp.sum(pr, axis=1, keepdims=True)
                m_ref[h] = m_new
                ps.append(pr.astype(BF16))
                alphas.append(alpha)
            acc_ref[g] = (acc_ref[g] * jnp.concatenate(alphas, axis=1)
                          + _dot(ps[0], v_ref[2 * g]) + _dot(ps[1], v_ref[2 * g + 1]))
            return carry

        lax.fori_loop(0, pairs, body, 0)

    @pl.when(j < i * r)
    def _():
        step(False)

    @pl.when(j >= i * r)
    def _():
        step(True)

    @pl.when(j == (i + 1) * r - 1)
    def _():
        for g in range(pairs):
            inv = jnp.concatenate([1.0 / l_ref[2 * g], 1.0 / l_ref[2 * g + 1]], axis=1)
            o_ref[:, g * 2 * V_DIM:(g + 1) * 2 * V_DIM] = acc_ref[g] * inv


def _attention_prompt(qh, kh, vh, tq, tk):
    rows = qh.shape[1]
    r = tq // tk
    qi, kj = [], []
    for i in range(rows // tq):
        for j in range((i + 1) * r):
            qi.append(i)
            kj.append(j)
    qi = jnp.asarray(np.array(qi, np.int32))
    kj = jnp.asarray(np.array(kj, np.int32))
    grid_spec = pltpu.PrefetchScalarGridSpec(
        num_scalar_prefetch=2,
        grid=(int(qi.shape[0]),),
        in_specs=[
            pl.BlockSpec((MLA_HEADS, tq, QK_PAD), lambda t, qi, kj: (0, qi[t], 0)),
            pl.BlockSpec((MLA_HEADS, tk, QK_PAD), lambda t, qi, kj: (0, kj[t], 0)),
            pl.BlockSpec((MLA_HEADS, tk, QK_PAD), lambda t, qi, kj: (0, kj[t], 0)),
        ],
        out_specs=pl.BlockSpec((tq, MLA_INNER), lambda t, qi, kj: (qi[t], 0)),
        scratch_shapes=[
            pltpu.VMEM((MLA_HEADS, tq, LANES), F32),
            pltpu.VMEM((MLA_HEADS, tq, LANES), F32),
            pltpu.VMEM((MLA_HEADS // 2, tq, 2 * V_DIM), F32),
        ],
    )
    return pl.pallas_call(
        functools.partial(_attn_kernel, tq=tq, tk=tk),
        grid_spec=grid_spec,
        out_shape=jax.ShapeDtypeStruct((rows, MLA_INNER), F32),
        compiler_params=_cparams(("arbitrary",)),
        name="attn_prompt",
    )(qi, kj, qh, kh, vh)


def _attn_cached_kernel(ql_ref, qr_ref, ckv_c_ref, kr_c_ref, ckv_n_ref, kr_n_ref, wv_ref, o_ref, *, t_new):
    rows = MLA_HEADS * t_new
    ql = ql_ref[...].reshape(rows, KV_LORA)
    qr = qr_ref[...].reshape(rows, LANES)[:, 0:QK_ROPE]
    kc = ckv_c_ref[0].astype(BF16)
    kn = ckv_n_ref[0].astype(BF16)
    s_c = _dot_nt(ql, kc) + _dot_nt(qr, kr_c_ref[0].astype(BF16))
    s_n = _dot_nt(ql, kn) + _dot_nt(qr, kr_n_ref[0].astype(BF16))
    m = jnp.maximum(jnp.max(s_c, axis=1, keepdims=True), jnp.max(s_n, axis=1, keepdims=True))
    p_c = jnp.exp(s_c - m)
    p_n = jnp.exp(s_n - m)
    den = jnp.sum(p_c, axis=1, keepdims=True) + jnp.sum(p_n, axis=1, keepdims=True)
    o_lat = (_dot(p_c.astype(BF16), kc) + _dot(p_n.astype(BF16), kn)) / den
    o_bf = o_lat.astype(BF16)
    for h in range(MLA_HEADS):
        o_ref[0, :, h * V_DIM:(h + 1) * V_DIM] = _dot(o_bf[h * t_new:(h + 1) * t_new, :],
                                                        wv_ref[:, h * V_DIM:(h + 1) * V_DIM])


def _attention_cached(ql, qr, cache_kv, cache_kr, ckv_new, kr_new, w_uv):
    nb, past = cache_kv.shape[0], cache_kv.shape[1]
    t_new = ckv_new.shape[1]
    return pl.pallas_call(
        functools.partial(_attn_cached_kernel, t_new=t_new),
        grid=(nb,),
        in_specs=[
            pl.BlockSpec((MLA_HEADS, t_new, KV_LORA), lambda b: (0, b, 0)),
            pl.BlockSpec((MLA_HEADS, t_new, LANES), lambda b: (0, b, 0)),
            pl.BlockSpec((1, past, KV_LORA), lambda b: (b, 0, 0)),
            pl.BlockSpec((1, past, QK_ROPE), lambda b: (b, 0, 0)),
            pl.BlockSpec((1, t_new, KV_LORA), lambda b: (b, 0, 0)),
            pl.BlockSpec((1, t_new, QK_ROPE), lambda b: (b, 0, 0)),
            _const_spec((KV_LORA, MLA_INNER)),
        ],
        out_specs=pl.BlockSpec((1, t_new, MLA_INNER), lambda b: (b, 0, 0)),
        out_shape=jax.ShapeDtypeStruct((nb, t_new, MLA_INNER), F32),
        compiler_params=_cparams(("arbitrary",)),
        name="attn_cached",
    )(ql, qr, cache_kv, cache_kr, ckv_new, kr_new, w_uv)


def _merge_kernel(y_ref, attn_ref, gm_ref, gs_ref, ga_ref, wssm_ref, wmla_ref, o_ref):
    y_ssm = _dot(y_ref[...], wssm_ref[...])
    y_mla = _dot((attn_ref[...] * _silu(gm_ref[...])).astype(BF16), wmla_ref[...])
    o_ref[...] = (_sigmoid(gs_ref[...]) * y_ssm + _sigmoid(ga_ref[...]) * y_mla).astype(o_ref.dtype)


def _merge(y_norm, attn, gates, p, tm):
    rows = y_norm.shape[0]
    gate = lambda k: pl.BlockSpec((tm, D_MODEL), lambda i: (i, k))
    return pl.pallas_call(
        _merge_kernel,
        grid=(rows // tm,),
        in_specs=[pl.BlockSpec((tm, SSM_INNER), lambda i: (i, 0)), pl.BlockSpec((tm, MLA_INNER), lambda i: (i, 0)),
                  gate(0), gate(1), gate(2),
                  _const_spec((SSM_INNER, D_MODEL)), _const_spec((MLA_INNER, D_MODEL))],
        out_specs=pl.BlockSpec((tm, D_MODEL), lambda i: (i, 0)),
        out_shape=jax.ShapeDtypeStruct((rows, D_MODEL), BF16),
        compiler_params=_cparams(("arbitrary",)),
        name="merge",
    )(y_norm, attn, gates, gates, gates, p["w_ssm_out"], p["w_mla_out"])


def _outproj_kernel(x_ref, mixed_ref, w_ref, nw_ref, o_ref):
    x_out = x_ref[...] + _dot(mixed_ref[...], w_ref[...])
    o_ref[...] = _rms(x_out, nw_ref[...])


def _outproj(x, mixed, p, tm):
    rows = x.shape[0]
    row = pl.BlockSpec((tm, D_MODEL), lambda i: (i, 0))
    return pl.pallas_call(
        _outproj_kernel,
        grid=(rows // tm,),
        in_specs=[row, row, _const_spec((D_MODEL, D_MODEL)), _const_spec((1, D_MODEL))],
        out_specs=row,
        out_shape=jax.ShapeDtypeStruct((rows, D_MODEL), F32),
        compiler_params=_cparams(("arbitrary",)),
        name="outproj",
    )(x, mixed, p["w_out"], p["final_norm_w"])


def _prepare(norm_in_w, w_in, conv_w, conv_b, dt_bias, a_log, d_skip, ssm_norm_w, w_ssm_out, q_norm_w, w_q_up,
             kv_norm_w, w_kv_up, w_mla_out, w_out, final_norm_w, chunk_lens):
    offs = np.cumsum((0, SSM_INNER, CONV_DIM, SSM_HEADS, Q_LORA, KV_LORA, QK_ROPE, MLA_INNER, D_MODEL, D_MODEL))
    col = lambda k: w_in[:, int(offs[k]):int(offs[k + 1])]
    half = QK_ROPE // 2
    kr_w = col(5)
    zeros = lambda n: jnp.zeros((D_MODEL, n), w_in.dtype)
    w_all = jnp.concatenate([
        col(0), col(1), col(3), col(4),
        col(2), zeros(64), kr_w, zeros(64), kr_w[:, half:], kr_w[:, :half], zeros(64 + 128),
        col(6), col(7), col(8)], axis=1).astype(BF16)

    wq = w_q_up.reshape(Q_LORA, MLA_HEADS, QK_NOPE + QK_ROPE)
    wq_rope = wq[..., QK_NOPE:]
    zq = jnp.zeros((Q_LORA, MLA_HEADS, 64), w_q_up.dtype)
    wq_a = jnp.concatenate([wq, zq], axis=-1).reshape(Q_LORA, MLA_HEADS * QK_PAD).astype(BF16)
    wq_b = jnp.concatenate([wq_rope[..., half:], wq_rope[..., :half], zq], axis=-1)
    wq_b = wq_b.reshape(Q_LORA, MLA_HEADS * LANES).astype(BF16)

    wkv = w_kv_up.reshape(KV_LORA, MLA_HEADS, QK_NOPE + V_DIM)
    w_uk = wkv[..., :QK_NOPE].reshape(KV_LORA, MLA_HEADS * QK_NOPE).astype(BF16)
    w_uv_h = wkv[..., QK_NOPE:]
    w_uv = w_uv_h.reshape(KV_LORA, MLA_INNER).astype(BF16)
    pairs = w_uv_h.reshape(KV_LORA, MLA_HEADS // 2, 2, V_DIM)
    zv = jnp.zeros((KV_LORA, MLA_HEADS // 2, V_DIM), w_kv_up.dtype)
    w_uv2 = jnp.stack([jnp.concatenate([pairs[:, :, 0], zv], axis=-1),
                       jnp.concatenate([zv, pairs[:, :, 1]], axis=-1)], axis=2)
    w_uv2 = w_uv2.reshape(KV_LORA, MLA_HEADS * QK_PAD).astype(BF16)

    head_of_lane = np.arange(SSM_INNER) // SSM_HEAD_DIM
    p = dict(
        norm_in_w=norm_in_w.reshape(1, D_MODEL), w_all=w_all,
        conv_w=conv_w, conv_b=conv_b.reshape(1, CONV_DIM), dt_bias=dt_bias.reshape(1, SSM_HEADS),
        a_log=a_log.reshape(1, SSM_HEADS), d_skip_p=jnp.repeat(d_skip, SSM_HEAD_DIM).reshape(1, SSM_INNER),
        ssm_norm_w=ssm_norm_w.reshape(1, SSM_INNER), w_ssm_out=w_ssm_out.astype(BF16),
        q_norm_w=q_norm_w.reshape(1, Q_LORA), kv_norm_w=kv_norm_w.reshape(1, KV_LORA),
        wq_a=wq_a, wq_b=wq_b, w_uk=w_uk, w_uv=w_uv, w_uv2=w_uv2,
        w_mla_out=w_mla_out.astype(BF16), w_out=w_out.astype(BF16),
        final_norm_w=final_norm_w.reshape(1, D_MODEL),
        e_p=jnp.asarray(head_of_lane[None, :] == np.arange(SSM_HEADS)[:, None], BF16),
    )
    for q in chunk_lens:
        head_of_key_lane = np.arange(SSM_HEADS * q) // q
        p["e_q%d" % q] = jnp.asarray(head_of_key_lane[None, :] == np.arange(SSM_HEADS)[:, None], BF16)
    return p


def _rope_tables(pos):
    half = QK_ROPE // 2
    inv = 1.0 / (ROPE_THETA ** (jnp.arange(half, dtype=F32) * (2.0 / QK_ROPE)))
    ang = pos.astype(F32)[:, None] * inv[None, :]
    cos, sin = jnp.cos(ang), jnp.sin(ang)
    pad = jnp.zeros((pos.shape[0], LANES - QK_ROPE), F32)
    return jnp.concatenate([cos, cos, pad], axis=1), jnp.concatenate([-sin, sin, pad], axis=1)


def kernel(x_prompt, x_sample, cache_kv_latent, cache_k_rope, state_ssm, state_conv, norm_in_w, w_in, conv_w,
           conv_b, dt_bias, a_log, d_skip, ssm_norm_w, w_ssm_out, q_norm_w, w_q_up, kv_norm_w, w_kv_up, w_mla_out,
           w_out, final_norm_w):
    depth = w_in.shape[0]
    assert depth == 1 and x_prompt.shape[0] == 1
    seq = x_prompt.shape[1]
    nb, t_new = x_sample.shape[0], x_sample.shape[1]
    past = cache_kv_latent.shape[2]
    q_prompt, q_sample = min(CHUNK, seq), min(CHUNK, t_new)
    p = _prepare(norm_in_w[0], w_in[0], conv_w[0], conv_b[0], dt_bias[0], a_log[0], d_skip[0], ssm_norm_w[0],
                 w_ssm_out[0], q_norm_w[0], w_q_up[0], kv_norm_w[0], w_kv_up[0], w_mla_out[0], w_out[0],
                 final_norm_w, sorted({q_prompt, q_sample}))

    xp = x_prompt[0]
    z, xbc, mla, gates = _inproj(xp, p["norm_in_w"], p["w_all"], tm=min(512, seq))
    y_norm, ssm_p, conv_p = _ssd(xbc[None], z[None], mla[None],
                                 jnp.zeros((1, CONV_WIDTH - 1, CONV_DIM), F32),
                                 jnp.zeros((1, SSM_INNER, SSM_STATE), F32), p, q_prompt)
    ct, sn = _rope_tables(jnp.arange(seq))
    ckv_p, kr_p, qh, kh, vh = _mla_prompt(mla, ct, sn, p, tm=min(256, seq))
    attn = _attention_prompt(qh, kh, vh, tq=min(512, seq), tk=min(512, seq))
    mixed = _merge(y_norm[0], attn, gates, p, tm=min(256, seq))
    y_prompt = _outproj(xp, mixed, p, tm=min(256, seq))

    rows_s = nb * t_new
    xs = x_sample.reshape(rows_s, D_MODEL)
    z_s, xbc_s, mla_s, gates_s = _inproj(xs, p["norm_in_w"], p["w_all"], tm=rows_s)
    y_norm_s, ssm_s, conv_s = _ssd(xbc_s.reshape(nb, t_new, CONV_DIM), z_s.reshape(nb, t_new, SSM_INNER),
                                   mla_s.reshape(nb, t_new, SEG_MLA), state_conv[0],
                                   state_ssm[0].reshape(nb, SSM_INNER, SSM_STATE), p, q_sample)
    ct_s, sn_s = _rope_tables(past + jnp.arange(t_new))
    ckv_s, kr_s, ql, qr = _mla_sample(mla_s, jnp.tile(ct_s, (nb, 1)), jnp.tile(sn_s, (nb, 1)), p)
    ckv_s = ckv_s.reshape(nb, t_new, KV_LORA)
    kr_s = kr_s.reshape(nb, t_new, QK_ROPE)
    attn_s = _attention_cached(ql, qr, cache_kv_latent[0], cache_k_rope[0], ckv_s, kr_s, p["w_uv"])
    mixed_s = _merge(y_norm_s.reshape(rows_s, SSM_INNER), attn_s.reshape(rows_s, MLA_INNER), gates_s, p, tm=rows_s)
    y_sample = _outproj(xs, mixed_s, p, tm=rows_s).reshape(nb, t_new, D_MODEL)

    hshape = (SSM_HEADS, SSM_HEAD_DIM, SSM_STATE)
    return (y_prompt[None], y_sample,
            ckv_p[None, None], kr_p[None, None], ssm_p.reshape((1, 1) + hshape), conv_p[None],
            ckv_s[None], kr_s[None], ssm_s.reshape((1, nb) + hshape), conv_s[None])
```

```python
import functools
import math

import jax
import jax.numpy as jnp
import numpy as np
from jax import lax
from jax.experimental import pallas as pl
from jax.experimental.pallas import tpu as pltpu

F32 = jnp.float32
BF16 = jnp.bfloat16

D_MODEL = 2048
CHUNK = 64
SSM_INNER = 4096
SSM_HEAD_DIM = 64
SSM_HEADS = 64
SSM_GROUPS = 8
SSM_STATE = 128
CONV_WIDTH = 4
CONV_DIM = SSM_INNER + 2 * SSM_GROUPS * SSM_STATE
GROUP_LANES = SSM_INNER // SSM_GROUPS
MLA_HEADS = 16
Q_LORA = 512
KV_LORA = 512
QK_NOPE = 128
QK_ROPE = 64
V_DIM = 128
MLA_INNER = MLA_HEADS * V_DIM
ROPE_THETA = 10000.0
EPS = 1e-6
QK_PAD = 256
SCALE = (QK_NOPE + QK_ROPE) ** -0.5

LANES = 128
VMEM_LIMIT = 56 * 1024 * 1024

IN_TN = 512
SEG_Z = SSM_INNER
SEG_XBC = CONV_DIM
SEG_MLA = Q_LORA + KV_LORA + 512
SEG_G = MLA_INNER + 2 * D_MODEL
IN_SEGS = (SEG_Z, SEG_XBC, SEG_MLA, SEG_G)


def _cparams(sem, vmem=VMEM_LIMIT):
    return pltpu.CompilerParams(dimension_semantics=sem, vmem_limit_bytes=vmem)


def _const_spec(shape):
    nd = len(shape)
    return pl.BlockSpec(shape, lambda *_: (0,) * nd, pipeline_mode=pl.Buffered(1))


def _sigmoid(x):
    return 0.5 + 0.5 * jnp.tanh(0.5 * x)


def _silu(x):
    h = 0.5 * x
    return h + h * jnp.tanh(h)


def _rms(x, w):
    return x * lax.rsqrt(jnp.mean(x * x, axis=-1, keepdims=True) + EPS) * w


def _dot(a, b):
    return jnp.dot(a, b, preferred_element_type=F32)


def _dot_nt(a, b):
    return lax.dot_general(a, b, (((1,), (1,)), ((), ())), preferred_element_type=F32)


def _dot_tn(a, b):
    return lax.dot_general(a, b, (((0,), (0,)), ((), ())), preferred_element_type=F32)


def _split3(v):
    hi = v.astype(BF16)
    r1 = v - hi.astype(F32)
    mid = r1.astype(BF16)
    lo = (r1 - mid.astype(F32)).astype(BF16)
    return hi, mid, lo


def _inproj_kernel(x_ref, nw_ref, w_ref, *rest, bounds):
    outs, h_ref = rest[:-1], rest[-1]
    j = pl.program_id(1)

    @pl.when(j == 0)
    def _():
        h_ref[...] = _rms(x_ref[...], nw_ref[...]).astype(BF16)

    for o_ref, (s, e) in zip(outs, bounds):
        @pl.when((j >= s) & (j < e))
        def _(o_ref=o_ref):
            o_ref[...] = _dot(h_ref[...], w_ref[...])


def _inproj(x, norm_w, w_all, tm):
    rows = x.shape[0]
    counts = [s // IN_TN for s in IN_SEGS]
    starts = np.cumsum([0] + counts)
    bounds = tuple((int(starts[k]), int(starts[k + 1])) for k in range(len(counts)))

    def omap(k):
        s, n = bounds[k][0], counts[k]
        return lambda i, j: (i, jnp.clip(j - s, 0, n - 1))

    return pl.pallas_call(
        functools.partial(_inproj_kernel, bounds=bounds),
        grid=(rows // tm, int(starts[-1])),
        in_specs=[
            pl.BlockSpec((tm, D_MODEL), lambda i, j: (i, 0)),
            pl.BlockSpec((1, D_MODEL), lambda i, j: (0, 0)),
            pl.BlockSpec((D_MODEL, IN_TN), lambda i, j: (0, j)),
        ],
        out_specs=[pl.BlockSpec((tm, IN_TN), omap(k)) for k in range(len(counts))],
        out_shape=[jax.ShapeDtypeStruct((rows, s), F32) for s in IN_SEGS],
        scratch_shapes=[pltpu.VMEM((tm, D_MODEL), BF16)],
        compiler_params=_cparams(("arbitrary", "arbitrary")),
        name="inproj",
    )(x, norm_w, w_all)


CONV_PAD = 8


def _ssd_kernel(xbc_ref, z_ref, sm_ref, cprev_ref, s0_ref, cw_ref, cb_ref, dtb_ref, alog_ref, dsk_ref,
                nw_ref, ep_ref, eq_ref, y_ref, sout_ref, cout_ref, st_ref, xpad_ref, xc_ref, exp_p_ref,
                exp_q_ref, yd_ref, *, q):
    c = pl.program_id(1)
    nc = pl.num_programs(1)
    hp = LANES // q
    tiles_per_group = (SSM_HEADS // SSM_GROUPS) // hp
    tile_w = hp * SSM_HEAD_DIM
    lo = CONV_PAD - (CONV_WIDTH - 1)

    @pl.when(c == 0)
    def _():
        st_ref[...] = s0_ref[0].T
        xpad_ref[lo:CONV_PAD, :] = cprev_ref[0]

    xpad_ref[CONV_PAD:CONV_PAD + q, :] = xbc_ref[0]
    for s in range(0, CONV_DIM, 1024):
        acc = cb_ref[:, s:s + 1024] + xpad_ref[lo:lo + q, s:s + 1024] * cw_ref[0:1, s:s + 1024]
        for k in range(1, CONV_WIDTH):
            acc = acc + xpad_ref[lo + k:lo + k + q, s:s + 1024] * cw_ref[k:k + 1, s:s + 1024]
        xc_ref[:, s:s + 1024] = _silu(acc)
    tail_rows = xpad_ref[CONV_PAD + q - (CONV_WIDTH - 1):CONV_PAD + q, :]
    xpad_ref[lo:CONV_PAD, :] = tail_rows

    xdt = sm_ref[0][:, 0:SSM_HEADS] + dtb_ref[...]
    dt = jnp.maximum(xdt, 0.0) + jnp.log1p(jnp.exp(-jnp.abs(xdt)))
    adt = dt * (-jnp.exp(alog_ref[...]))
    ri = lax.broadcasted_iota(jnp.int32, (q, q), 0)
    ci = lax.broadcasted_iota(jnp.int32, (q, q), 1)
    tril = jnp.where(ci <= ri, 1.0, 0.0).astype(BF16)
    h3 = _split3(adt)
    acs = _dot(tril, h3[0]) + _dot(tril, h3[1]) + _dot(tril, h3[2])

    st3 = jnp.concatenate([piece.astype(F32) for piece in _split3(jnp.concatenate([acs, dt], axis=0))],
                          axis=1).astype(BF16)
    exp_p_ref[...] = _dot(st3, ep_ref[...])
    if q == SSM_HEAD_DIM:
        xq_ref = exp_p_ref
    else:
        exp_q_ref[...] = _dot(st3, eq_ref[...])
        xq_ref = exp_q_ref

    r128 = lax.broadcasted_iota(jnp.int32, (q, LANES), 0)
    c128 = lax.broadcasted_iota(jnp.int32, (q, LANES), 1)
    key128 = c128 & (q - 1)
    causal = key128 <= r128
    diag = key128 == r128
    br = lax.broadcasted_iota(jnp.int32, (LANES, tile_w), 0)
    bc = lax.broadcasted_iota(jnp.int32, (LANES, tile_w), 1)
    blockdiag = (br // q) == (bc // SSM_HEAD_DIM)

    for g in range(SSM_GROUPS):
        gs = g * GROUP_LANES
        b_g = xc_ref[:, SSM_INNER + g * SSM_STATE:SSM_INNER + (g + 1) * SSM_STATE]
        c_g = xc_ref[:, SSM_INNER + (SSM_GROUPS + g) * SSM_STATE:SSM_INNER + (SSM_GROUPS + g + 1) * SSM_STATE]
        b_bf = b_g.astype(BF16)
        c_bf = c_g.astype(BF16)
        cb_t = _dot_nt(c_bf, jnp.concatenate([b_bf] * hp, axis=0))

        for tt in range(tiles_per_group):
            t = g * tiles_per_group + tt
            a_t = xq_ref[0:q, t * LANES:(t + 1) * LANES]
            d_t = xq_ref[q:2 * q, t * LANES:(t + 1) * LANES]
            a_key = jnp.sum(jnp.where(diag, a_t, 0.0), axis=0, keepdims=True)
            d_key = jnp.sum(jnp.where(diag, d_t, 0.0), axis=0, keepdims=True)
            w_t = jnp.exp(jnp.where(causal, a_t - a_key, -jnp.inf)) * d_key * cb_t
            x_t = xc_ref[:, t * tile_w:(t + 1) * tile_w]
            rhs = jnp.where(blockdiag, jnp.concatenate([x_t] * hp, axis=0), 0.0).astype(BF16)
            yd_ref[:, t * tile_w:(t + 1) * tile_w] = _dot(w_t.astype(BF16), rhs)

        a_p = exp_p_ref[0:q, gs:gs + GROUP_LANES]
        d_p = exp_p_ref[q:2 * q, gs:gs + GROUP_LANES]
        a_last = exp_p_ref[q - 1:q, gs:gs + GROUP_LANES]
        xs_g = xc_ref[:, gs:gs + GROUP_LANES]
        st_g = st_ref[:, gs:gs + GROUP_LANES]
        y_off = _dot(c_bf, st_g.astype(BF16)) * jnp.exp(a_p)
        xt = (xs_g * (jnp.exp(a_last - a_p) * d_p)).astype(BF16)
        st_ref[:, gs:gs + GROUP_LANES] = st_g * jnp.exp(a_last) + _dot_tn(b_bf, xt)

        y = yd_ref[:, gs:gs + GROUP_LANES] + y_off + dsk_ref[:, gs:gs + GROUP_LANES] * xs_g
        y = y * _silu(z_ref[0][:, gs:gs + GROUP_LANES])
        y = y * lax.rsqrt(jnp.mean(y * y, axis=-1, keepdims=True) + EPS)
        y_ref[0, :, gs:gs + GROUP_LANES] = (y * nw_ref[:, gs:gs + GROUP_LANES]).astype(y_ref.dtype)

    @pl.when(c == nc - 1)
    def _():
        sout_ref[0] = st_ref[...].T
        cout_ref[0] = tail_rows


def _ssd(xbc, z, mla, conv_prev, s0, p, q):
    nb, lb = xbc.shape[0], xbc.shape[1]
    nc = lb // q
    kq = SSM_HEADS * q
    in_specs = [
        pl.BlockSpec((1, q, CONV_DIM), lambda b, c: (b, c, 0)),
        pl.BlockSpec((1, q, SSM_INNER), lambda b, c: (b, c, 0)),
        pl.BlockSpec((1, q, 512), lambda b, c: (b, c, (Q_LORA + KV_LORA) // 512)),
        pl.BlockSpec((1, CONV_WIDTH - 1, CONV_DIM), lambda b, c: (b, 0, 0)),
        pl.BlockSpec((1, SSM_INNER, SSM_STATE), lambda b, c: (b, 0, 0)),
        _const_spec((CONV_WIDTH, CONV_DIM)),
        _const_spec((1, CONV_DIM)),
        _const_spec((1, SSM_HEADS)),
        _const_spec((1, SSM_HEADS)),
        _const_spec((1, SSM_INNER)),
        _const_spec((1, SSM_INNER)),
        _const_spec((3 * SSM_HEADS, SSM_INNER)),
        _const_spec((3 * SSM_HEADS, kq)),
    ]
    out_specs = [
        pl.BlockSpec((1, q, SSM_INNER), lambda b, c: (b, c, 0)),
        pl.BlockSpec((1, SSM_INNER, SSM_STATE), lambda b, c: (b, 0, 0)),
        pl.BlockSpec((1, CONV_WIDTH - 1, CONV_DIM), lambda b, c: (b, 0, 0)),
    ]
    out_shape = [
        jax.ShapeDtypeStruct((nb, lb, SSM_INNER), BF16),
        jax.ShapeDtypeStruct((nb, SSM_INNER, SSM_STATE), F32),
        jax.ShapeDtypeStruct((nb, CONV_WIDTH - 1, CONV_DIM), F32),
    ]
    scratch = [
        pltpu.VMEM((SSM_STATE, SSM_INNER), F32),
        pltpu.VMEM((CONV_PAD + q, CONV_DIM), F32),
        pltpu.VMEM((q, CONV_DIM), F32),
        pltpu.VMEM((2 * q, SSM_INNER), F32),
        pltpu.VMEM((2 * q, kq), F32),
        pltpu.VMEM((q, SSM_INNER), F32),
    ]
    return pl.pallas_call(
        functools.partial(_ssd_kernel, q=q),
        grid=(nb, nc),
        in_specs=in_specs,
        out_specs=out_specs,
        out_shape=out_shape,
        scratch_shapes=scratch,
        compiler_params=_cparams(("arbitrary", "arbitrary")),
        name="ssd",
    )(xbc, z, mla, conv_prev, s0, p["conv_w"], p["conv_b"], p["dt_bias"], p["a_log"], p["d_skip_p"],
      p["ssm_norm_w"], p["e_p"], p["e_q%d" % q])


def _mla_common(mla_ref, ct_ref, sn_ref, qnw_ref, kvnw_ref, wqa_ref, wqb_ref, ckv_ref, kr_ref):
    m = mla_ref[...]
    cqn = _rms(m[:, 0:Q_LORA], qnw_ref[...]).astype(BF16)
    ckv = _rms(m[:, Q_LORA:Q_LORA + KV_LORA], kvnw_ref[...])
    ckv_ref[...] = ckv
    small = m[:, Q_LORA + KV_LORA:]
    ct, sn = ct_ref[...], sn_ref[...]
    kr128 = small[:, 128:256] * ct + small[:, 256:384] * sn
    kr_ref[...] = kr128[:, 0:QK_ROPE]
    qa = _dot(cqn, wqa_ref[...])
    qb = _dot(cqn, wqb_ref[...])
    return ckv.astype(BF16), kr128, qa, qb, ct, sn


def _mla_prompt_kernel(mla_ref, ct_ref, sn_ref, qnw_ref, kvnw_ref, wqa_ref, wqb_ref, wk_ref, wv2_ref,
                       ckv_ref, kr_ref, q_ref, k_ref, v_ref):
    ckv_bf, kr128, qa, qb, ct, sn = _mla_common(mla_ref, ct_ref, sn_ref, qnw_ref, kvnw_ref, wqa_ref, wqb_ref,
                                                 ckv_ref, kr_ref)
    kn = _dot(ckv_bf, wk_ref[...])
    vv = _dot(ckv_bf, wv2_ref[...])
    kr_bf = kr128.astype(BF16)
    for h in range(MLA_HEADS):
        o = h * QK_PAD
        q_rope = qa[:, o + QK_NOPE:o + QK_PAD] * ct + qb[:, h * LANES:(h + 1) * LANES] * sn
        q_ref[h, :, 0:QK_NOPE] = (qa[:, o:o + QK_NOPE] * SCALE).astype(BF16)
        q_ref[h, :, QK_NOPE:QK_PAD] = (q_rope * SCALE).astype(BF16)
        k_ref[h, :, 0:QK_NOPE] = kn[:, h * QK_NOPE:(h + 1) * QK_NOPE].astype(BF16)
        k_ref[h, :, QK_NOPE:QK_PAD] = kr_bf
        v_ref[h] = vv[:, o:o + QK_PAD].astype(BF16)


def _mla_prompt(mla, ct, sn, p, tm):
    rows = mla.shape[0]
    row = lambda w: pl.BlockSpec((tm, w), lambda i: (i, 0))
    head = pl.BlockSpec((MLA_HEADS, tm, QK_PAD), lambda i: (0, i, 0))
    return pl.pallas_call(
        _mla_prompt_kernel,
        grid=(rows // tm,),
        in_specs=[row(SEG_MLA), row(LANES), row(LANES), _const_spec((1, Q_LORA)), _const_spec((1, KV_LORA)),
                  _const_spec((Q_LORA, MLA_HEADS * QK_PAD)), _const_spec((Q_LORA, MLA_HEADS * LANES)),
                  _const_spec((KV_LORA, MLA_HEADS * QK_NOPE)), _const_spec((KV_LORA, MLA_HEADS * QK_PAD))],
        out_specs=[row(KV_LORA), row(QK_ROPE), head, head, head],
        out_shape=[jax.ShapeDtypeStruct((rows, KV_LORA), F32), jax.ShapeDtypeStruct((rows, QK_ROPE), F32)]
        + [jax.ShapeDtypeStruct((MLA_HEADS, rows, QK_PAD), BF16)] * 3,
        compiler_params=_cparams(("arbitrary",)),
        name="mla_prep_prompt",
    )(mla, ct, sn, p["q_norm_w"], p["kv_norm_w"], p["wq_a"], p["wq_b"], p["w_uk"], p["w_uv2"])


def _mla_sample_kernel(mla_ref, ct_ref, sn_ref, qnw_ref, kvnw_ref, wqa_ref, wqb_ref, wk_ref,
                       ckv_ref, kr_ref, ql_ref, qr_ref):
    _, _, qa, qb, ct, sn = _mla_common(mla_ref, ct_ref, sn_ref, qnw_ref, kvnw_ref, wqa_ref, wqb_ref,
                                       ckv_ref, kr_ref)
    for h in range(MLA_HEADS):
        o = h * QK_PAD
        q_rope = qa[:, o + QK_NOPE:o + QK_PAD] * ct + qb[:, h * LANES:(h + 1) * LANES] * sn
        qr_ref[h] = (q_rope * SCALE).astype(BF16)
        q_nope = qa[:, o:o + QK_NOPE].astype(BF16)
        q_lat = _dot_nt(q_nope, wk_ref[:, h * QK_NOPE:(h + 1) * QK_NOPE])
        ql_ref[h] = (q_lat * SCALE).astype(BF16)


def _mla_sample(mla, ct, sn, p):
    rows = mla.shape[0]
    full = lambda w: pl.BlockSpec((rows, w), lambda i: (0, 0))
    return pl.pallas_call(
        _mla_sample_kernel,
        grid=(1,),
        in_specs=[full(SEG_MLA), full(LANES), full(LANES), _const_spec((1, Q_LORA)), _const_spec((1, KV_LORA)),
                  _const_spec((Q_LORA, MLA_HEADS * QK_PAD)), _const_spec((Q_LORA, MLA_HEADS * LANES)),
                  _const_spec((KV_LORA, MLA_HEADS * QK_NOPE))],
        out_specs=[full(KV_LORA), full(QK_ROPE),
                   pl.BlockSpec((MLA_HEADS, rows, KV_LORA), lambda i: (0, 0, 0)),
                   pl.BlockSpec((MLA_HEADS, rows, LANES), lambda i: (0, 0, 0))],
        out_shape=[jax.ShapeDtypeStruct((rows, KV_LORA), F32), jax.ShapeDtypeStruct((rows, QK_ROPE), F32),
                   jax.ShapeDtypeStruct((MLA_HEADS, rows, KV_LORA), BF16),
                   jax.ShapeDtypeStruct((MLA_HEADS, rows, LANES), BF16)],
        compiler_params=_cparams(("arbitrary",)),
        name="mla_prep_sample",
    )(mla, ct, sn, p["q_norm_w"], p["kv_norm_w"], p["wq_a"], p["wq_b"], p["w_uk"])


def _attn_kernel(qi_ref, kj_ref, q_ref, k_ref, v_ref, o_ref, m_ref, l_ref, acc_ref, s0_ref, s1_ref, p0_ref,
                 p1_ref, a0_ref, a1_ref, *, tq, tk):
    s_refs, p_refs, a_refs = (s0_ref, s1_ref), (p0_ref, p1_ref), (a0_ref, a1_ref)
    t = pl.program_id(0)
    i = qi_ref[t]
    j = kj_ref[t]
    r = tq // tk
    pairs = MLA_HEADS // 2

    @pl.when(j == 0)
    def _():
        m_ref[...] = jnp.full(m_ref.shape, -jnp.inf, F32)
        l_ref[...] = jnp.zeros(l_ref.shape, F32)
        acc_ref[...] = jnp.zeros(acc_ref.shape, F32)

    def step(masked):
        if masked:
            rows = (i * tq + lax.broadcasted_iota(jnp.int32, (tq, tk), 0)) // CHUNK
            cols = (j * tk + lax.broadcasted_iota(jnp.int32, (tq, tk), 1)) // CHUNK
            visible = cols <= rows

        def scores(g, par):
            for e in range(2):
                s_refs[par][e] = _dot_nt(q_ref[2 * g + e], k_ref[2 * g + e])

        def softmax(g, par):
            for e in range(2):
                h = 2 * g + e
                s = s_refs[par][e]
                if masked:
                    s = jnp.where(visible, s, -jnp.inf)
                m_prev = m_ref[h]
                m_new = jnp.maximum(m_prev, jnp.max(s, axis=1, keepdims=True))
                alpha = jnp.exp(m_prev - m_new)
                pr = jnp.exp(s - m_new[:, 0:1])
                l_ref[h] = alpha * l_ref[h] + jnp.sum(pr, axis=1, keepdims=True)
                m_ref[h] = m_new
                p_refs[par][e] = pr.astype(BF16)
                a_refs[par][:, e * V_DIM:(e + 1) * V_DIM] = alpha

        def values(g, par):
            acc_ref[g] = (acc_ref[g] * a_refs[par][...]
                          + _dot(p_refs[par][0], v_ref[2 * g]) + _dot(p_refs[par][1], v_ref[2 * g + 1]))

        scores(0, 0)
        scores(1, 1)
        softmax(0, 0)

        def body(u, carry):
            g = 2 * u
            scores(g, 0)
            softmax(g - 1, 1)
            values(g - 2, 0)
            scores(g + 1, 1)
            softmax(g, 0)
            values(g - 1, 1)
            return carry

        lax.fori_loop(1, pairs // 2, body, 0)
        softmax(pairs - 1, 1)
        values(pairs - 2, 0)
        values(pairs - 1, 1)

    @pl.when(j < i * r)
    def _():
        step(False)

    @pl.when(j >= i * r)
    def _():
        step(True)

    @pl.when(j == (i + 1) * r - 1)
    def _():
        for g in range(pairs):
            inv = jnp.concatenate([1.0 / l_ref[2 * g], 1.0 / l_ref[2 * g + 1]], axis=1)
            o_ref[:, g * 2 * V_DIM:(g + 1) * 2 * V_DIM] = acc_ref[g] * inv


def _attention_prompt(qh, kh, vh, tq, tk):
    rows = qh.shape[1]
    r = tq // tk
    qi, kj = [], []
    for i in range(rows // tq):
        for j in range((i + 1) * r):
            qi.append(i)
            kj.append(j)
    qi = jnp.asarray(np.array(qi, np.int32))
    kj = jnp.asarray(np.array(kj, np.int32))
    grid_spec = pltpu.PrefetchScalarGridSpec(
        num_scalar_prefetch=2,
        grid=(int(qi.shape[0]),),
        in_specs=[
            pl.BlockSpec((MLA_HEADS, tq, QK_PAD), lambda t, qi, kj: (0, qi[t], 0)),
            pl.BlockSpec((MLA_HEADS, tk, QK_PAD), lambda t, qi, kj: (0, kj[t], 0)),
            pl.BlockSpec((MLA_HEADS, tk, QK_PAD), lambda t, qi, kj: (0, kj[t], 0)),
        ],
        out_specs=pl.BlockSpec((tq, MLA_INNER), lambda t, qi, kj: (qi[t], 0)),
        scratch_shapes=[
            pltpu.VMEM((MLA_HEADS, tq, LANES), F32),
            pltpu.VMEM((MLA_HEADS, tq, LANES), F32),
            pltpu.VMEM((MLA_HEADS // 2, tq, 2 * V_DIM), F32),
            pltpu.VMEM((2, tq, tk), F32), pltpu.VMEM((2, tq, tk), F32),
            pltpu.VMEM((2, tq, tk), BF16), pltpu.VMEM((2, tq, tk), BF16),
            pltpu.VMEM((tq, 2 * V_DIM), F32), pltpu.VMEM((tq, 2 * V_DIM), F32),
        ],
    )
    return pl.pallas_call(
        functools.partial(_attn_kernel, tq=tq, tk=tk),
        grid_spec=grid_spec,
        out_shape=jax.ShapeDtypeStruct((rows, MLA_INNER), F32),
        compiler_params=_cparams(("arbitrary",)),
        name="attn_prompt",
    )(qi, kj, qh, kh, vh)


def _attn_cached_kernel(ql_ref, qr_ref, ckv_c_ref, kr_c_ref, ckv_n_ref, kr_n_ref, wv_ref, o_ref, *, t_new):
    rows = MLA_HEADS * t_new
    ql = ql_ref[...].reshape(rows, KV_LORA)
    qr = qr_ref[...].reshape(rows, LANES)[:, 0:QK_ROPE]
    kc = ckv_c_ref[0].astype(BF16)
    kn = ckv_n_ref[0].astype(BF16)
    s_c = _dot_nt(ql, kc) + _dot_nt(qr, kr_c_ref[0].astype(BF16))
    s_n = _dot_nt(ql, kn) + _dot_nt(qr, kr_n_ref[0].astype(BF16))
    m = jnp.maximum(jnp.max(s_c, axis=1, keepdims=True), jnp.max(s_n, axis=1, keepdims=True))
    p_c = jnp.exp(s_c - m)
    p_n = jnp.exp(s_n - m)
    den = jnp.sum(p_c, axis=1, keepdims=True) + jnp.sum(p_n, axis=1, keepdims=True)
    o_lat = (_dot(p_c.astype(BF16), kc) + _dot(p_n.astype(BF16), kn)) / den
    o_bf = o_lat.astype(BF16)
    for h in range(MLA_HEADS):
        o_ref[0, :, h * V_DIM:(h + 1) * V_DIM] = _dot(o_bf[h * t_new:(h + 1) * t_new, :],
                                                        wv_ref[:, h * V_DIM:(h + 1) * V_DIM])


def _attention_cached(ql, qr, cache_kv, cache_kr, ckv_new, kr_new, w_uv):
    nb, past = cache_kv.shape[0], cache_kv.shape[1]
    t_new = ckv_new.shape[1]
    return pl.pallas_call(
        functools.partial(_attn_cached_kernel, t_new=t_new),
        grid=(nb,),
        in_specs=[
            pl.BlockSpec((MLA_HEADS, t_new, KV_LORA), lambda b: (0, b, 0)),
            pl.BlockSpec((MLA_HEADS, t_new, LANES), lambda b: (0, b, 0)),
            pl.BlockSpec((1, past, KV_LORA), lambda b: (b, 0, 0)),
            pl.BlockSpec((1, past, QK_ROPE), lambda b: (b, 0, 0)),
            pl.BlockSpec((1, t_new, KV_LORA), lambda b: (b, 0, 0)),
            pl.BlockSpec((1, t_new, QK_ROPE), lambda b: (b, 0, 0)),
            _const_spec((KV_LORA, MLA_INNER)),
        ],
        out_specs=pl.BlockSpec((1, t_new, MLA_INNER), lambda b: (b, 0, 0)),
        out_shape=jax.ShapeDtypeStruct((nb, t_new, MLA_INNER), F32),
        compiler_params=_cparams(("arbitrary",)),
        name="attn_cached",
    )(ql, qr, cache_kv, cache_kr, ckv_new, kr_new, w_uv)


def _merge_kernel(y_ref, attn_ref, gm_ref, gs_ref, ga_ref, wssm_ref, wmla_ref, o_ref):
    y_ssm = _dot(y_ref[...], wssm_ref[...])
    y_mla = _dot((attn_ref[...] * _silu(gm_ref[...])).astype(BF16), wmla_ref[...])
    o_ref[...] = (_sigmoid(gs_ref[...]) * y_ssm + _sigmoid(ga_ref[...]) * y_mla).astype(o_ref.dtype)


def _merge(y_norm, attn, gates, p, tm):
    rows = y_norm.shape[0]
    gate = lambda k: pl.BlockSpec((tm, D_MODEL), lambda i: (i, k))
    return pl.pallas_call(
        _merge_kernel,
        grid=(rows // tm,),
        in_specs=[pl.BlockSpec((tm, SSM_INNER), lambda i: (i, 0)), pl.BlockSpec((tm, MLA_INNER), lambda i: (i, 0)),
                  gate(0), gate(1), gate(2),
                  _const_spec((SSM_INNER, D_MODEL)), _const_spec((MLA_INNER, D_MODEL))],
        out_specs=pl.BlockSpec((tm, D_MODEL), lambda i: (i, 0)),
        out_shape=jax.ShapeDtypeStruct((rows, D_MODEL), BF16),
        compiler_params=_cparams(("arbitrary",)),
        name="merge",
    )(y_norm, attn, gates, gates, gates, p["w_ssm_out"], p["w_mla_out"])


def _outproj_kernel(x_ref, mixed_ref, w_ref, nw_ref, o_ref):
    x_out = x_ref[...] + _dot(mixed_ref[...], w_ref[...])
    o_ref[...] = _rms(x_out, nw_ref[...])


def _outproj(x, mixed, p, tm):
    rows = x.shape[0]
    row = pl.BlockSpec((tm, D_MODEL), lambda i: (i, 0))
    return pl.pallas_call(
        _outproj_kernel,
        grid=(rows // tm,),
        in_specs=[row, row, _const_spec((D_MODEL, D_MODEL)), _const_spec((1, D_MODEL))],
        out_specs=row,
        out_shape=jax.ShapeDtypeStruct((rows, D_MODEL), F32),
        compiler_params=_cparams(("arbitrary",)),
        name="outproj",
    )(x, mixed, p["w_out"], p["final_norm_w"])


def _prepare(norm_in_w, w_in, conv_w, conv_b, dt_bias, a_log, d_skip, ssm_norm_w, w_ssm_out, q_norm_w, w_q_up,
             kv_norm_w, w_kv_up, w_mla_out, w_out, final_norm_w, chunk_lens):
    offs = np.cumsum((0, SSM_INNER, CONV_DIM, SSM_HEADS, Q_LORA, KV_LORA, QK_ROPE, MLA_INNER, D_MODEL, D_MODEL))
    col = lambda k: w_in[:, int(offs[k]):int(offs[k + 1])]
    half = QK_ROPE // 2
    kr_w = col(5)
    zeros = lambda n: jnp.zeros((D_MODEL, n), w_in.dtype)
    w_all = jnp.concatenate([
        col(0), col(1), col(3), col(4),
        col(2), zeros(64), kr_w, zeros(64), kr_w[:, half:], kr_w[:, :half], zeros(64 + 128),
        col(6), col(7), col(8)], axis=1).astype(BF16)

    wq = w_q_up.reshape(Q_LORA, MLA_HEADS, QK_NOPE + QK_ROPE)
    wq_rope = wq[..., QK_NOPE:]
    zq = jnp.zeros((Q_LORA, MLA_HEADS, 64), w_q_up.dtype)
    wq_a = jnp.concatenate([wq, zq], axis=-1).reshape(Q_LORA, MLA_HEADS * QK_PAD).astype(BF16)
    wq_b = jnp.concatenate([wq_rope[..., half:], wq_rope[..., :half], zq], axis=-1)
    wq_b = wq_b.reshape(Q_LORA, MLA_HEADS * LANES).astype(BF16)

    wkv = w_kv_up.reshape(KV_LORA, MLA_HEADS, QK_NOPE + V_DIM)
    w_uk = wkv[..., :QK_NOPE].reshape(KV_LORA, MLA_HEADS * QK_NOPE).astype(BF16)
    w_uv_h = wkv[..., QK_NOPE:]
    w_uv = w_uv_h.reshape(KV_LORA, MLA_INNER).astype(BF16)
    pairs = w_uv_h.reshape(KV_LORA, MLA_HEADS // 2, 2, V_DIM)
    zv = jnp.zeros((KV_LORA, MLA_HEADS // 2, V_DIM), w_kv_up.dtype)
    w_uv2 = jnp.stack([jnp.concatenate([pairs[:, :, 0], zv], axis=-1),
                       jnp.concatenate([zv, pairs[:, :, 1]], axis=-1)], axis=2)
    w_uv2 = w_uv2.reshape(KV_LORA, MLA_HEADS * QK_PAD).astype(BF16)

    head_of_lane = np.arange(SSM_INNER) // SSM_HEAD_DIM
    p = dict(
        norm_in_w=norm_in_w.reshape(1, D_MODEL), w_all=w_all,
        conv_w=conv_w, conv_b=conv_b.reshape(1, CONV_DIM), dt_bias=dt_bias.reshape(1, SSM_HEADS),
        a_log=a_log.reshape(1, SSM_HEADS), d_skip_p=jnp.repeat(d_skip, SSM_HEAD_DIM).reshape(1, SSM_INNER),
        ssm_norm_w=ssm_norm_w.reshape(1, SSM_INNER), w_ssm_out=w_ssm_out.astype(BF16),
        q_norm_w=q_norm_w.reshape(1, Q_LORA), kv_norm_w=kv_norm_w.reshape(1, KV_LORA),
        wq_a=wq_a, wq_b=wq_b, w_uk=w_uk, w_uv=w_uv, w_uv2=w_uv2,
        w_mla_out=w_mla_out.astype(BF16), w_out=w_out.astype(BF16),
        final_norm_w=final_norm_w.reshape(1, D_MODEL),
        e_p=jnp.asarray(np.tile(head_of_lane[None, :] == np.arange(SSM_HEADS)[:, None], (3, 1)), BF16),
    )
    for q in chunk_lens:
        head_of_key_lane = np.arange(SSM_HEADS * q) // q
        p["e_q%d" % q] = jnp.asarray(np.tile(head_of_key_lane[None, :] == np.arange(SSM_HEADS)[:, None], (3, 1)),
                                     BF16)
    return p


def _rope_tables(pos):
    half = QK_ROPE // 2
    inv = 1.0 / (ROPE_THETA ** (jnp.arange(half, dtype=F32) * (2.0 / QK_ROPE)))
    ang = pos.astype(F32)[:, None] * inv[None, :]
    cos, sin = jnp.cos(ang), jnp.sin(ang)
    pad = jnp.zeros((pos.shape[0], LANES - QK_ROPE), F32)
    return jnp.concatenate([cos, cos, pad], axis=1), jnp.concatenate([-sin, sin, pad], axis=1)


def kernel(x_prompt, x_sample, cache_kv_latent, cache_k_rope, state_ssm, state_conv, norm_in_w, w_in, conv_w,
           conv_b, dt_bias, a_log, d_skip, ssm_norm_w, w_ssm_out, q_norm_w, w_q_up, kv_norm_w, w_kv_up, w_mla_out,
           w_out, final_norm_w):
    depth = w_in.shape[0]
    assert depth == 1 and x_prompt.shape[0] == 1
    seq = x_prompt.shape[1]
    nb, t_new = x_sample.shape[0], x_sample.shape[1]
    past = cache_kv_latent.shape[2]
    q_prompt, q_sample = min(CHUNK, seq), min(CHUNK, t_new)
    p = _prepare(norm_in_w[0], w_in[0], conv_w[0], conv_b[0], dt_bias[0], a_log[0], d_skip[0], ssm_norm_w[0],
                 w_ssm_out[0], q_norm_w[0], w_q_up[0], kv_norm_w[0], w_kv_up[0], w_mla_out[0], w_out[0],
                 final_norm_w, sorted({q_prompt, q_sample}))

    xp = x_prompt[0]
    z, xbc, mla, gates = _inproj(xp, p["norm_in_w"], p["w_all"], tm=min(1024, seq))
    y_norm, ssm_p, conv_p = _ssd(xbc[None], z[None], mla[None],
                                 jnp.zeros((1, CONV_WIDTH - 1, CONV_DIM), F32),
                                 jnp.zeros((1, SSM_INNER, SSM_STATE), F32), p, q_prompt)
    ct, sn = _rope_tables(jnp.arange(seq))
    ckv_p, kr_p, qh, kh, vh = _mla_prompt(mla, ct, sn, p, tm=min(256, seq))
    attn = _attention_prompt(qh, kh, vh, tq=min(512, seq), tk=min(512, seq))
    mixed = _merge(y_norm[0], attn, gates, p, tm=min(256, seq))
    y_prompt = _outproj(xp, mixed, p, tm=min(256, seq))

    rows_s = nb * t_new
    xs = x_sample.reshape(rows_s, D_MODEL)
    z_s, xbc_s, mla_s, gates_s = _inproj(xs, p["norm_in_w"], p["w_all"], tm=rows_s)
    y_norm_s, ssm_s, conv_s = _ssd(xbc_s.reshape(nb, t_new, CONV_DIM), z_s.reshape(nb, t_new, SSM_INNER),
                                   mla_s.reshape(nb, t_new, SEG_MLA), state_conv[0],
                                   state_ssm[0].reshape(nb, SSM_INNER, SSM_STATE), p, q_sample)
    ct_s, sn_s = _rope_tables(past + jnp.arange(t_new))
    ckv_s, kr_s, ql, qr = _mla_sample(mla_s, jnp.tile(ct_s, (nb, 1)), jnp.tile(sn_s, (nb, 1)), p)
    ckv_s = ckv_s.reshape(nb, t_new, KV_LORA)
    kr_s = kr_s.reshape(nb, t_new, QK_ROPE)
    attn_s = _attention_cached(ql, qr, cache_kv_latent[0], cache_k_rope[0], ckv_s, kr_s, p["w_uv"])
    mixed_s = _merge(y_norm_s.reshape(rows_s, SSM_INNER), attn_s.reshape(rows_s, MLA_INNER), gates_s, p, tm=rows_s)
    y_sample = _outproj(xs, mixed_s, p, tm=rows_s).reshape(nb, t_new, D_MODEL)

    hshape = (SSM_HEADS, SSM_HEAD_DIM, SSM_STATE)
    return (y_prompt[None], y_sample,
            ckv_p[None, None], kr_p[None, None], ssm_p.reshape((1, 1) + hshape), conv_p[None],
            ckv_s[None], kr_s[None], ssm_s.reshape((1, nb) + hshape), conv_s[None])
```

```python
import functools
import math

import jax
import jax.numpy as jnp
import numpy as np
from jax import lax
from jax.experimental import pallas as pl
from jax.experimental.pallas import tpu as pltpu

F32 = jnp.float32
BF16 = jnp.bfloat16

D_MODEL = 2048
CHUNK = 64
SSM_INNER = 4096
SSM_HEAD_DIM = 64
SSM_HEADS = 64
SSM_GROUPS = 8
SSM_STATE = 128
CONV_WIDTH = 4
CONV_DIM = SSM_INNER + 2 * SSM_GROUPS * SSM_STATE
GROUP_LANES = SSM_INNER // SSM_GROUPS
MLA_HEADS = 16
Q_LORA = 512
KV_LORA = 512
QK_NOPE = 128
QK_ROPE = 64
V_DIM = 128
MLA_INNER = MLA_HEADS * V_DIM
ROPE_THETA = 10000.0
EPS = 1e-6
QK_PAD = 256
SCALE = (QK_NOPE + QK_ROPE) ** -0.5
LOG2E = math.log2(math.e)
V_ROWS = V_DIM + 16

LANES = 128
VMEM_LIMIT = 56 * 1024 * 1024

IN_TN = 512
SEG_Z = SSM_INNER
SEG_XBC = CONV_DIM
SEG_MLA = Q_LORA + KV_LORA + 512
SEG_G = MLA_INNER + 2 * D_MODEL
IN_SEGS = (SEG_Z, SEG_XBC, SEG_MLA, SEG_G)


def _cparams(sem, vmem=VMEM_LIMIT):
    return pltpu.CompilerParams(dimension_semantics=sem, vmem_limit_bytes=vmem)


def _const_spec(shape):
    nd = len(shape)
    return pl.BlockSpec(shape, lambda *_: (0,) * nd, pipeline_mode=pl.Buffered(1))


def _sigmoid(x):
    return 0.5 + 0.5 * jnp.tanh(0.5 * x)


def _silu(x):
    h = 0.5 * x
    return h + h * jnp.tanh(h)


def _rms(x, w):
    return x * lax.rsqrt(jnp.mean(x * x, axis=-1, keepdims=True) + EPS) * w


def _dot(a, b):
    return jnp.dot(a, b, preferred_element_type=F32)


def _dot_nt(a, b):
    return lax.dot_general(a, b, (((1,), (1,)), ((), ())), preferred_element_type=F32)


def _dot_tn(a, b):
    return lax.dot_general(a, b, (((0,), (0,)), ((), ())), preferred_element_type=F32)


def _split3(v):
    hi = v.astype(BF16)
    r1 = v - hi.astype(F32)
    mid = r1.astype(BF16)
    lo = (r1 - mid.astype(F32)).astype(BF16)
    return hi, mid, lo


def _inproj_kernel(x_ref, nw_ref, w_ref, *rest, bounds):
    outs, h_ref = rest[:-1], rest[-1]
    j = pl.program_id(1)

    @pl.when(j == 0)
    def _():
        h_ref[...] = _rms(x_ref[...], nw_ref[...]).astype(BF16)

    for o_ref, (s, e) in zip(outs, bounds):
        @pl.when((j >= s) & (j < e))
        def _(o_ref=o_ref):
            o_ref[...] = _dot(h_ref[...], w_ref[...])


def _inproj(x, norm_w, w_all, tm):
    rows = x.shape[0]
    counts = [s // IN_TN for s in IN_SEGS]
    starts = np.cumsum([0] + counts)
    bounds = tuple((int(starts[k]), int(starts[k + 1])) for k in range(len(counts)))

    def omap(k):
        s, n = bounds[k][0], counts[k]
        return lambda i, j: (i, jnp.clip(j - s, 0, n - 1))

    return pl.pallas_call(
        functools.partial(_inproj_kernel, bounds=bounds),
        grid=(rows // tm, int(starts[-1])),
        in_specs=[
            pl.BlockSpec((tm, D_MODEL), lambda i, j: (i, 0)),
            pl.BlockSpec((1, D_MODEL), lambda i, j: (0, 0)),
            pl.BlockSpec((D_MODEL, IN_TN), lambda i, j: (0, j)),
        ],
        out_specs=[pl.BlockSpec((tm, IN_TN), omap(k)) for k in range(len(counts))],
        out_shape=[jax.ShapeDtypeStruct((rows, s), F32) for s in IN_SEGS],
        scratch_shapes=[pltpu.VMEM((tm, D_MODEL), BF16)],
        compiler_params=_cparams(("arbitrary", "arbitrary")),
        name="inproj",
    )(x, norm_w, w_all)


CONV_PAD = 8


def _ssd_kernel(xbc_ref, z_ref, sm_ref, cprev_ref, s0_ref, cw_ref, cb_ref, dtb_ref, alog_ref, dsk_ref,
                nw_ref, ep_ref, eq_ref, y_ref, sout_ref, cout_ref, st_ref, xpad_ref, xc_ref, exp_p_ref,
                exp_q_ref, yd_ref, *, q):
    c = pl.program_id(1)
    nc = pl.num_programs(1)
    hp = LANES // q
    tiles_per_group = (SSM_HEADS // SSM_GROUPS) // hp
    tile_w = hp * SSM_HEAD_DIM
    lo = CONV_PAD - (CONV_WIDTH - 1)

    @pl.when(c == 0)
    def _():
        st_ref[...] = s0_ref[0].T
        xpad_ref[lo:CONV_PAD, :] = cprev_ref[0]

    xpad_ref[CONV_PAD:CONV_PAD + q, :] = xbc_ref[0]
    for s in range(0, CONV_DIM, 1024):
        acc = cb_ref[:, s:s + 1024] + xpad_ref[lo:lo + q, s:s + 1024] * cw_ref[0:1, s:s + 1024]
        for k in range(1, CONV_WIDTH):
            acc = acc + xpad_ref[lo + k:lo + k + q, s:s + 1024] * cw_ref[k:k + 1, s:s + 1024]
        xc_ref[:, s:s + 1024] = _silu(acc)
    tail_rows = xpad_ref[CONV_PAD + q - (CONV_WIDTH - 1):CONV_PAD + q, :]
    xpad_ref[lo:CONV_PAD, :] = tail_rows

    xdt = sm_ref[0][:, 0:SSM_HEADS] + dtb_ref[...]
    dt = jnp.maximum(xdt, 0.0) + jnp.log1p(jnp.exp(-jnp.abs(xdt)))
    adt = dt * (-jnp.exp(alog_ref[...]))
    ri = lax.broadcasted_iota(jnp.int32, (q, q), 0)
    ci = lax.broadcasted_iota(jnp.int32, (q, q), 1)
    tril = jnp.where(ci <= ri, 1.0, 0.0).astype(BF16)
    h3 = _split3(adt)
    acs = _dot(tril, h3[0]) + _dot(tril, h3[1]) + _dot(tril, h3[2])

    st3 = jnp.concatenate([piece.astype(F32) for piece in _split3(jnp.concatenate([acs, dt], axis=0))],
                          axis=1).astype(BF16)
    exp_p_ref[...] = _dot(st3, ep_ref[...])
    if q == SSM_HEAD_DIM:
        xq_ref = exp_p_ref
    else:
        exp_q_ref[...] = _dot(st3, eq_ref[...])
        xq_ref = exp_q_ref

    r128 = lax.broadcasted_iota(jnp.int32, (q, LANES), 0)
    c128 = lax.broadcasted_iota(jnp.int32, (q, LANES), 1)
    key128 = c128 & (q - 1)
    causal = key128 <= r128
    diag = key128 == r128
    br = lax.broadcasted_iota(jnp.int32, (LANES, tile_w), 0)
    bc = lax.broadcasted_iota(jnp.int32, (LANES, tile_w), 1)
    blockdiag = (br // q) == (bc // SSM_HEAD_DIM)

    for g in range(SSM_GROUPS):
        gs = g * GROUP_LANES
        b_g = xc_ref[:, SSM_INNER + g * SSM_STATE:SSM_INNER + (g + 1) * SSM_STATE]
        c_g = xc_ref[:, SSM_INNER + (SSM_GROUPS + g) * SSM_STATE:SSM_INNER + (SSM_GROUPS + g + 1) * SSM_STATE]
        b_bf = b_g.astype(BF16)
        c_bf = c_g.astype(BF16)
        cb_t = _dot_nt(c_bf, jnp.concatenate([b_bf] * hp, axis=0))

        for tt in range(tiles_per_group):
            t = g * tiles_per_group + tt
            a_t = xq_ref[0:q, t * LANES:(t + 1) * LANES]
            d_t = xq_ref[q:2 * q, t * LANES:(t + 1) * LANES]
            a_key = jnp.sum(jnp.where(diag, a_t, 0.0), axis=0, keepdims=True)
            d_key = jnp.sum(jnp.where(diag, d_t, 0.0), axis=0, keepdims=True)
            w_t = jnp.exp(jnp.where(causal, a_t - a_key, -jnp.inf)) * d_key * cb_t
            x_t = xc_ref[:, t * tile_w:(t + 1) * tile_w]
            rhs = jnp.where(blockdiag, jnp.concatenate([x_t] * hp, axis=0), 0.0).astype(BF16)
            yd_ref[:, t * tile_w:(t + 1) * tile_w] = _dot(w_t.astype(BF16), rhs)

        a_p = exp_p_ref[0:q, gs:gs + GROUP_LANES]
        d_p = exp_p_ref[q:2 * q, gs:gs + GROUP_LANES]
        a_last = exp_p_ref[q - 1:q, gs:gs + GROUP_LANES]
        xs_g = xc_ref[:, gs:gs + GROUP_LANES]
        st_g = st_ref[:, gs:gs + GROUP_LANES]
        y_off = _dot(c_bf, st_g.astype(BF16)) * jnp.exp(a_p)
        xt = (xs_g * (jnp.exp(a_last - a_p) * d_p)).astype(BF16)
        st_ref[:, gs:gs + GROUP_LANES] = st_g * jnp.exp(a_last) + _dot_tn(b_bf, xt)

        y = yd_ref[:, gs:gs + GROUP_LANES] + y_off + dsk_ref[:, gs:gs + GROUP_LANES] * xs_g
        y = y * _silu(z_ref[0][:, gs:gs + GROUP_LANES])
        y = y * lax.rsqrt(jnp.mean(y * y, axis=-1, keepdims=True) + EPS)
        y_ref[0, :, gs:gs + GROUP_LANES] = (y * nw_ref[:, gs:gs + GROUP_LANES]).astype(y_ref.dtype)

    @pl.when(c == nc - 1)
    def _():
        sout_ref[0] = st_ref[...].T
        cout_ref[0] = tail_rows


def _ssd(xbc, z, mla, conv_prev, s0, p, q):
    nb, lb = xbc.shape[0], xbc.shape[1]
    nc = lb // q
    kq = SSM_HEADS * q
    in_specs = [
        pl.BlockSpec((1, q, CONV_DIM), lambda b, c: (b, c, 0)),
        pl.BlockSpec((1, q, SSM_INNER), lambda b, c: (b, c, 0)),
        pl.BlockSpec((1, q, 512), lambda b, c: (b, c, (Q_LORA + KV_LORA) // 512)),
        pl.BlockSpec((1, CONV_WIDTH - 1, CONV_DIM), lambda b, c: (b, 0, 0)),
        pl.BlockSpec((1, SSM_INNER, SSM_STATE), lambda b, c: (b, 0, 0)),
        _const_spec((CONV_WIDTH, CONV_DIM)),
        _const_spec((1, CONV_DIM)),
        _const_spec((1, SSM_HEADS)),
        _const_spec((1, SSM_HEADS)),
        _const_spec((1, SSM_INNER)),
        _const_spec((1, SSM_INNER)),
        _const_spec((3 * SSM_HEADS, SSM_INNER)),
        _const_spec((3 * SSM_HEADS, kq)),
    ]
    out_specs = [
        pl.BlockSpec((1, q, SSM_INNER), lambda b, c: (b, c, 0)),
        pl.BlockSpec((1, SSM_INNER, SSM_STATE), lambda b, c: (b, 0, 0)),
        pl.BlockSpec((1, CONV_WIDTH - 1, CONV_DIM), lambda b, c: (b, 0, 0)),
    ]
    out_shape = [
        jax.ShapeDtypeStruct((nb, lb, SSM_INNER), BF16),
        jax.ShapeDtypeStruct((nb, SSM_INNER, SSM_STATE), F32),
        jax.ShapeDtypeStruct((nb, CONV_WIDTH - 1, CONV_DIM), F32),
    ]
    scratch = [
        pltpu.VMEM((SSM_STATE, SSM_INNER), F32),
        pltpu.VMEM((CONV_PAD + q, CONV_DIM), F32),
        pltpu.VMEM((q, CONV_DIM), F32),
        pltpu.VMEM((2 * q, SSM_INNER), F32),
        pltpu.VMEM((2 * q, kq), F32),
        pltpu.VMEM((q, SSM_INNER), F32),
    ]
    return pl.pallas_call(
        functools.partial(_ssd_kernel, q=q),
        grid=(nb, nc),
        in_specs=in_specs,
        out_specs=out_specs,
        out_shape=out_shape,
        scratch_shapes=scratch,
        compiler_params=_cparams(("arbitrary", "arbitrary")),
        name="ssd",
    )(xbc, z, mla, conv_prev, s0, p["conv_w"], p["conv_b"], p["dt_bias"], p["a_log"], p["d_skip_p"],
      p["ssm_norm_w"], p["e_p"], p["e_q%d" % q])


def _mla_latents(mla_ref, ct_ref, sn_ref, qnw_ref, kvnw_ref, ckv_ref, kr_ref):
    m = mla_ref[...]
    cqn = _rms(m[:, 0:Q_LORA], qnw_ref[...]).astype(BF16)
    ckv = _rms(m[:, Q_LORA:Q_LORA + KV_LORA], kvnw_ref[...])
    ckv_ref[...] = ckv
    small = m[:, Q_LORA + KV_LORA:]
    ct, sn = ct_ref[...], sn_ref[...]
    kr128 = small[:, 128:256] * ct + small[:, 256:384] * sn
    kr_ref[...] = kr128[:, 0:QK_ROPE]
    return cqn, ckv.astype(BF16), kr128, ct, sn


def _mla_prompt_kernel(mla_ref, ct_ref, sn_ref, qnw_ref, kvnw_ref, wqa_ref, wqb_ref, wk_ref, wvt_ref,
                       ckv_ref, kr_ref, q_ref, k_ref, vt_ref):
    cqn, ckv_bf, kr128, ct, sn = _mla_latents(mla_ref, ct_ref, sn_ref, qnw_ref, kvnw_ref, ckv_ref, kr_ref)
    kr_bf = kr128.astype(BF16)
    tm = ckv_bf.shape[0]
    ones_row = jnp.where(lax.broadcasted_iota(jnp.int32, (V_ROWS - V_DIM, tm), 0) == 0, 1.0, 0.0).astype(BF16)
    qs = SCALE * LOG2E
    for h in range(MLA_HEADS):
        qa = _dot(cqn, wqa_ref[:, h * QK_PAD:(h + 1) * QK_PAD])
        qb = _dot(cqn, wqb_ref[:, h * LANES:(h + 1) * LANES])
        q_ref[h, :, 0:QK_NOPE] = (qa[:, 0:QK_NOPE] * qs).astype(BF16)
        q_ref[h, :, QK_NOPE:QK_PAD] = ((qa[:, QK_NOPE:QK_PAD] * ct + qb * sn) * qs).astype(BF16)
        k_ref[h, :, 0:QK_NOPE] = _dot(ckv_bf, wk_ref[:, h * QK_NOPE:(h + 1) * QK_NOPE]).astype(BF16)
        k_ref[h, :, QK_NOPE:QK_PAD] = kr_bf
        vt_ref[h, 0, 0:V_DIM, :] = _dot_nt(wvt_ref[h * V_DIM:(h + 1) * V_DIM, :], ckv_bf).astype(BF16)
        vt_ref[h, 0, V_DIM:V_ROWS, :] = ones_row


def _mla_prompt(mla, ct, sn, p, tm):
    rows = mla.shape[0]
    row = lambda w: pl.BlockSpec((tm, w), lambda i: (i, 0))
    head = pl.BlockSpec((MLA_HEADS, tm, QK_PAD), lambda i: (0, i, 0))
    return pl.pallas_call(
        _mla_prompt_kernel,
        grid=(rows // tm,),
        in_specs=[row(SEG_MLA), row(LANES), row(LANES), _const_spec((1, Q_LORA)), _const_spec((1, KV_LORA)),
                  _const_spec((Q_LORA, MLA_HEADS * QK_PAD)), _const_spec((Q_LORA, MLA_HEADS * LANES)),
                  _const_spec((KV_LORA, MLA_HEADS * QK_NOPE)), _const_spec((MLA_INNER, KV_LORA))],
        out_specs=[row(KV_LORA), row(QK_ROPE), head, head,
                   pl.BlockSpec((MLA_HEADS, 1, V_ROWS, tm), lambda i: (0, i, 0, 0))],
        out_shape=[jax.ShapeDtypeStruct((rows, KV_LORA), F32), jax.ShapeDtypeStruct((rows, QK_ROPE), F32),
                   jax.ShapeDtypeStruct((MLA_HEADS, rows, QK_PAD), BF16),
                   jax.ShapeDtypeStruct((MLA_HEADS, rows, QK_PAD), BF16),
                   jax.ShapeDtypeStruct((MLA_HEADS, rows // tm, V_ROWS, tm), BF16)],
        compiler_params=_cparams(("arbitrary",)),
        name="mla_prep_prompt",
    )(mla, ct, sn, p["q_norm_w"], p["kv_norm_w"], p["wq_a"], p["wq_b"], p["w_uk"], p["w_uv_t"])


def _mla_sample_kernel(mla_ref, ct_ref, sn_ref, qnw_ref, kvnw_ref, wqa_ref, wqb_ref, wk_ref,
                       ckv_ref, kr_ref, ql_ref, qr_ref):
    cqn, _, _, ct, sn = _mla_latents(mla_ref, ct_ref, sn_ref, qnw_ref, kvnw_ref, ckv_ref, kr_ref)
    qa = _dot(cqn, wqa_ref[...])
    qb = _dot(cqn, wqb_ref[...])
    for h in range(MLA_HEADS):
        o = h * QK_PAD
        q_rope = qa[:, o + QK_NOPE:o + QK_PAD] * ct + qb[:, h * LANES:(h + 1) * LANES] * sn
        qr_ref[h] = (q_rope * SCALE).astype(BF16)
        q_nope = qa[:, o:o + QK_NOPE].astype(BF16)
        q_lat = _dot_nt(q_nope, wk_ref[:, h * QK_NOPE:(h + 1) * QK_NOPE])
        ql_ref[h] = (q_lat * SCALE).astype(BF16)


def _mla_sample(mla, ct, sn, p):
    rows = mla.shape[0]
    full = lambda w: pl.BlockSpec((rows, w), lambda i: (0, 0))
    return pl.pallas_call(
        _mla_sample_kernel,
        grid=(1,),
        in_specs=[full(SEG_MLA), full(LANES), full(LANES), _const_spec((1, Q_LORA)), _const_spec((1, KV_LORA)),
                  _const_spec((Q_LORA, MLA_HEADS * QK_PAD)), _const_spec((Q_LORA, MLA_HEADS * LANES)),
                  _const_spec((KV_LORA, MLA_HEADS * QK_NOPE))],
        out_specs=[full(KV_LORA), full(QK_ROPE),
                   pl.BlockSpec((MLA_HEADS, rows, KV_LORA), lambda i: (0, 0, 0)),
                   pl.BlockSpec((MLA_HEADS, rows, LANES), lambda i: (0, 0, 0))],
        out_shape=[jax.ShapeDtypeStruct((rows, KV_LORA), F32), jax.ShapeDtypeStruct((rows, QK_ROPE), F32),
                   jax.ShapeDtypeStruct((MLA_HEADS, rows, KV_LORA), BF16),
                   jax.ShapeDtypeStruct((MLA_HEADS, rows, LANES), BF16)],
        compiler_params=_cparams(("arbitrary",)),
        name="mla_prep_sample",
    )(mla, ct, sn, p["q_norm_w"], p["kv_norm_w"], p["wq_a"], p["wq_b"], p["w_uk"])


def _attn_kernel(qi_ref, kj_ref, q_ref, k_ref, vt_ref, o_ref, m_ref, acc_ref, s0_ref, s1_ref, p0_ref,
                 p1_ref, a0_ref, a1_ref, *, tq, tks):
    s_refs, p_refs, a_refs = (s0_ref, s1_ref), (p0_ref, p1_ref), (a0_ref, a1_ref)
    n_sub = len(s_refs)
    t = pl.program_id(0)
    i = qi_ref[t]
    j = kj_ref[t]
    tk = n_sub * tks
    last_j = ((i + 1) * tq - 1) // tk

    @pl.when(j == 0)
    def _():
        m_ref[...] = jnp.full(m_ref.shape, -jnp.inf, F32)
        acc_ref[...] = jnp.zeros(acc_ref.shape, F32)

    def step(masked):
        if masked:
            q_chunk = (i * tq + lax.broadcasted_iota(jnp.int32, (tks, tq), 1)) // CHUNK
            k_chunk = (j * tk + lax.broadcasted_iota(jnp.int32, (tks, tq), 0)) // CHUNK
            visible = [k_chunk + (c * tks) // CHUNK <= q_chunk for c in range(n_sub)]

        def scores(h, c):
            s_refs[c][...] = _dot_nt(k_ref[h, c * tks:(c + 1) * tks, :], q_ref[h])

        def softmax(h, c):
            s = s_refs[c][...]
            if masked:
                s = jnp.where(visible[c], s, -jnp.inf)
            m_prev = m_ref[h]
            m_new = jnp.maximum(m_prev, jnp.max(s, axis=0, keepdims=True))
            m_ref[h] = m_new
            p_refs[c][...] = jnp.exp2(s - m_new).astype(BF16)
            a_refs[c][...] = jnp.exp2(m_prev - m_new)

        def values(h, c):
            acc_ref[h] = acc_ref[h] * a_refs[c][...] + _dot(vt_ref[h, c], p_refs[c][...])

        scores(0, 0)
        scores(0, 1)
        softmax(0, 0)

        for h in range(1, MLA_HEADS):
            scores(h, 0)
            softmax(h - 1, 1)
            values(h - 1, 0)
            scores(h, 1)
            softmax(h, 0)
            values(h - 1, 1)
        softmax(MLA_HEADS - 1, 1)
        values(MLA_HEADS - 1, 0)
        values(MLA_HEADS - 1, 1)

    @pl.when(j < last_j)
    def _():
        step(False)

    @pl.when(j == last_j)
    def _():
        step(True)
        for h in range(MLA_HEADS):
            den = acc_ref[h, V_DIM:V_DIM + 1, :]
            o_ref[:, h * V_DIM:(h + 1) * V_DIM] = (acc_ref[h, 0:V_DIM, :] * (1.0 / den)).T


def _attention_prompt(qh, kh, vt, tq):
    rows = qh.shape[1]
    n_sub, tks = 2, vt.shape[3]
    tk = tks * n_sub
    qi, kj = [], []
    for i in range(rows // tq):
        for j in range(((i + 1) * tq - 1) // tk + 1):
            qi.append(i)
            kj.append(j)
    qi = jnp.asarray(np.array(qi, np.int32))
    kj = jnp.asarray(np.array(kj, np.int32))
    grid_spec = pltpu.PrefetchScalarGridSpec(
        num_scalar_prefetch=2,
        grid=(int(qi.shape[0]),),
        in_specs=[
            pl.BlockSpec((MLA_HEADS, tq, QK_PAD), lambda t, qi, kj: (0, qi[t], 0)),
            pl.BlockSpec((MLA_HEADS, tk, QK_PAD), lambda t, qi, kj: (0, kj[t], 0)),
            pl.BlockSpec((MLA_HEADS, n_sub, V_ROWS, tks), lambda t, qi, kj: (0, kj[t], 0, 0)),
        ],
        out_specs=pl.BlockSpec((tq, MLA_INNER), lambda t, qi, kj: (qi[t], 0)),
        scratch_shapes=[
            pltpu.VMEM((MLA_HEADS, 1, tq), F32),
            pltpu.VMEM((MLA_HEADS, V_ROWS, tq), F32),
            pltpu.VMEM((tks, tq), F32), pltpu.VMEM((tks, tq), F32),
            pltpu.VMEM((tks, tq), BF16), pltpu.VMEM((tks, tq), BF16),
            pltpu.VMEM((1, tq), F32), pltpu.VMEM((1, tq), F32),
        ],
    )
    return pl.pallas_call(
        functools.partial(_attn_kernel, tq=tq, tks=tks),
        grid_spec=grid_spec,
        out_shape=jax.ShapeDtypeStruct((rows, MLA_INNER), F32),
        compiler_params=_cparams(("arbitrary",)),
        name="attn_prompt",
    )(qi, kj, qh, kh, vt)


def _attn_cached_kernel(ql_ref, qr_ref, ckv_c_ref, kr_c_ref, ckv_n_ref, kr_n_ref, wv_ref, o_ref, *, t_new):
    rows = MLA_HEADS * t_new
    ql = ql_ref[...].reshape(rows, KV_LORA)
    qr = qr_ref[...].reshape(rows, LANES)[:, 0:QK_ROPE]
    kc = ckv_c_ref[0].astype(BF16)
    kn = ckv_n_ref[0].astype(BF16)
    s_c = _dot_nt(ql, kc) + _dot_nt(qr, kr_c_ref[0].astype(BF16))
    s_n = _dot_nt(ql, kn) + _dot_nt(qr, kr_n_ref[0].astype(BF16))
    m = jnp.maximum(jnp.max(s_c, axis=1, keepdims=True), jnp.max(s_n, axis=1, keepdims=True))
    p_c = jnp.exp(s_c - m)
    p_n = jnp.exp(s_n - m)
    den = jnp.sum(p_c, axis=1, keepdims=True) + jnp.sum(p_n, axis=1, keepdims=True)
    o_lat = (_dot(p_c.astype(BF16), kc) + _dot(p_n.astype(BF16), kn)) / den
    o_bf = o_lat.astype(BF16)
    for h in range(MLA_HEADS):
        o_ref[0, :, h * V_DIM:(h + 1) * V_DIM] = _dot(o_bf[h * t_new:(h + 1) * t_new, :],
                                                        wv_ref[:, h * V_DIM:(h + 1) * V_DIM])


def _attention_cached(ql, qr, cache_kv, cache_kr, ckv_new, kr_new, w_uv):
    nb, past = cache_kv.shape[0], cache_kv.shape[1]
    t_new = ckv_new.shape[1]
    return pl.pallas_call(
        functools.partial(_attn_cached_kernel, t_new=t_new),
        grid=(nb,),
        in_specs=[
            pl.BlockSpec((MLA_HEADS, t_new, KV_LORA), lambda b: (0, b, 0)),
            pl.BlockSpec((MLA_HEADS, t_new, LANES), lambda b: (0, b, 0)),
            pl.BlockSpec((1, past, KV_LORA), lambda b: (b, 0, 0)),
            pl.BlockSpec((1, past, QK_ROPE), lambda b: (b, 0, 0)),
            pl.BlockSpec((1, t_new, KV_LORA), lambda b: (b, 0, 0)),
            pl.BlockSpec((1, t_new, QK_ROPE), lambda b: (b, 0, 0)),
            _const_spec((KV_LORA, MLA_INNER)),
        ],
        out_specs=pl.BlockSpec((1, t_new, MLA_INNER), lambda b: (b, 0, 0)),
        out_shape=jax.ShapeDtypeStruct((nb, t_new, MLA_INNER), F32),
        compiler_params=_cparams(("arbitrary",)),
        name="attn_cached",
    )(ql, qr, cache_kv, cache_kr, ckv_new, kr_new, w_uv)


def _merge_kernel(y_ref, attn_ref, gm_ref, gs_ref, ga_ref, wssm_ref, wmla_ref, o_ref):
    y_ssm = _dot(y_ref[...], wssm_ref[...])
    y_mla = _dot((attn_ref[...] * _silu(gm_ref[...])).astype(BF16), wmla_ref[...])
    o_ref[...] = (_sigmoid(gs_ref[...]) * y_ssm + _sigmoid(ga_ref[...]) * y_mla).astype(o_ref.dtype)


def _merge(y_norm, attn, gates, p, tm):
    rows = y_norm.shape[0]
    gate = lambda k: pl.BlockSpec((tm, D_MODEL), lambda i: (i, k))
    return pl.pallas_call(
        _merge_kernel,
        grid=(rows // tm,),
        in_specs=[pl.BlockSpec((tm, SSM_INNER), lambda i: (i, 0)), pl.BlockSpec((tm, MLA_INNER), lambda i: (i, 0)),
                  gate(0), gate(1), gate(2),
                  _const_spec((SSM_INNER, D_MODEL)), _const_spec((MLA_INNER, D_MODEL))],
        out_specs=pl.BlockSpec((tm, D_MODEL), lambda i: (i, 0)),
        out_shape=jax.ShapeDtypeStruct((rows, D_MODEL), BF16),
        compiler_params=_cparams(("arbitrary",)),
        name="merge",
    )(y_norm, attn, gates, gates, gates, p["w_ssm_out"], p["w_mla_out"])


def _outproj_kernel(x_ref, mixed_ref, w_ref, nw_ref, o_ref):
    x_out = x_ref[...] + _dot(mixed_ref[...], w_ref[...])
    o_ref[...] = _rms(x_out, nw_ref[...])


def _outproj(x, mixed, p, tm):
    rows = x.shape[0]
    row = pl.BlockSpec((tm, D_MODEL), lambda i: (i, 0))
    return pl.pallas_call(
        _outproj_kernel,
        grid=(rows // tm,),
        in_specs=[row, row, _const_spec((D_MODEL, D_MODEL)), _const_spec((1, D_MODEL))],
        out_specs=row,
        out_shape=jax.ShapeDtypeStruct((rows, D_MODEL), F32),
        compiler_params=_cparams(("arbitrary",)),
        name="outproj",
    )(x, mixed, p["w_out"], p["final_norm_w"])


def _prepare(norm_in_w, w_in, conv_w, conv_b, dt_bias, a_log, d_skip, ssm_norm_w, w_ssm_out, q_norm_w, w_q_up,
             kv_norm_w, w_kv_up, w_mla_out, w_out, final_norm_w, chunk_lens):
    offs = np.cumsum((0, SSM_INNER, CONV_DIM, SSM_HEADS, Q_LORA, KV_LORA, QK_ROPE, MLA_INNER, D_MODEL, D_MODEL))
    w_in_bf = w_in.astype(BF16)
    col = lambda k: w_in_bf[:, int(offs[k]):int(offs[k + 1])]
    half = QK_ROPE // 2
    kr_w = col(5)
    zeros = lambda n: jnp.zeros((D_MODEL, n), BF16)
    w_all = jnp.concatenate([
        col(0), col(1), col(3), col(4),
        col(2), zeros(64), kr_w, zeros(64), kr_w[:, half:], kr_w[:, :half], zeros(64 + 128),
        col(6), col(7), col(8)], axis=1)

    wq = w_q_up.reshape(Q_LORA, MLA_HEADS, QK_NOPE + QK_ROPE)
    wq_rope = wq[..., QK_NOPE:]
    zq = jnp.zeros((Q_LORA, MLA_HEADS, 64), w_q_up.dtype)
    wq_a = jnp.concatenate([wq, zq], axis=-1).reshape(Q_LORA, MLA_HEADS * QK_PAD).astype(BF16)
    wq_b = jnp.concatenate([wq_rope[..., half:], wq_rope[..., :half], zq], axis=-1)
    wq_b = wq_b.reshape(Q_LORA, MLA_HEADS * LANES).astype(BF16)

    wkv = w_kv_up.reshape(KV_LORA, MLA_HEADS, QK_NOPE + V_DIM)
    w_uk = wkv[..., :QK_NOPE].reshape(KV_LORA, MLA_HEADS * QK_NOPE).astype(BF16)
    w_uv_h = wkv[..., QK_NOPE:]
    w_uv = w_uv_h.reshape(KV_LORA, MLA_INNER).astype(BF16)
    w_uv_t = w_uv.T

    head_of_lane = np.arange(SSM_INNER) // SSM_HEAD_DIM
    p = dict(
        norm_in_w=norm_in_w.reshape(1, D_MODEL), w_all=w_all,
        conv_w=conv_w, conv_b=conv_b.reshape(1, CONV_DIM), dt_bias=dt_bias.reshape(1, SSM_HEADS),
        a_log=a_log.reshape(1, SSM_HEADS), d_skip_p=jnp.repeat(d_skip, SSM_HEAD_DIM).reshape(1, SSM_INNER),
        ssm_norm_w=ssm_norm_w.reshape(1, SSM_INNER), w_ssm_out=w_ssm_out.astype(BF16),
        q_norm_w=q_norm_w.reshape(1, Q_LORA), kv_norm_w=kv_norm_w.reshape(1, KV_LORA),
        wq_a=wq_a, wq_b=wq_b, w_uk=w_uk, w_uv=w_uv, w_uv_t=w_uv_t,
        w_mla_out=w_mla_out.astype(BF16), w_out=w_out.astype(BF16),
        final_norm_w=final_norm_w.reshape(1, D_MODEL),
        e_p=jnp.asarray(np.tile(head_of_lane[None, :] == np.arange(SSM_HEADS)[:, None], (3, 1)), BF16),
    )
    for q in chunk_lens:
        head_of_key_lane = np.arange(SSM_HEADS * q) // q
        p["e_q%d" % q] = jnp.asarray(np.tile(head_of_key_lane[None, :] == np.arange(SSM_HEADS)[:, None], (3, 1)),
                                     BF16)
    return p


def _rope_tables(pos):
    half = QK_ROPE // 2
    inv = 1.0 / (ROPE_THETA ** (jnp.arange(half, dtype=F32) * (2.0 / QK_ROPE)))
    ang = pos.astype(F32)[:, None] * inv[None, :]
    cos, sin = jnp.cos(ang), jnp.sin(ang)
    pad = jnp.zeros((pos.shape[0], LANES - QK_ROPE), F32)
    return jnp.concatenate([cos, cos, pad], axis=1), jnp.concatenate([-sin, sin, pad], axis=1)


def kernel(x_prompt, x_sample, cache_kv_latent, cache_k_rope, state_ssm, state_conv, norm_in_w, w_in, conv_w,
           conv_b, dt_bias, a_log, d_skip, ssm_norm_w, w_ssm_out, q_norm_w, w_q_up, kv_norm_w, w_kv_up, w_mla_out,
           w_out, final_norm_w):
    depth = w_in.shape[0]
    assert depth == 1 and x_prompt.shape[0] == 1
    seq = x_prompt.shape[1]
    nb, t_new = x_sample.shape[0], x_sample.shape[1]
    past = cache_kv_latent.shape[2]
    q_prompt, q_sample = min(CHUNK, seq), min(CHUNK, t_new)
    p = _prepare(norm_in_w[0], w_in[0], conv_w[0], conv_b[0], dt_bias[0], a_log[0], d_skip[0], ssm_norm_w[0],
                 w_ssm_out[0], q_norm_w[0], w_q_up[0], kv_norm_w[0], w_kv_up[0], w_mla_out[0], w_out[0],
                 final_norm_w, sorted({q_prompt, q_sample}))

    xp = x_prompt[0]
    z, xbc, mla, gates = _inproj(xp, p["norm_in_w"], p["w_all"], tm=min(1024, seq))
    y_norm, ssm_p, conv_p = _ssd(xbc[None], z[None], mla[None],
                                 jnp.zeros((1, CONV_WIDTH - 1, CONV_DIM), F32),
                                 jnp.zeros((1, SSM_INNER, SSM_STATE), F32), p, q_prompt)
    ct, sn = _rope_tables(jnp.arange(seq))
    ckv_p, kr_p, qh, kh, vt = _mla_prompt(mla, ct, sn, p, tm=min(512, seq // 2))
    attn = _attention_prompt(qh, kh, vt, tq=min(512, seq))
    mixed = _merge(y_norm[0], attn, gates, p, tm=min(256, seq))
    y_prompt = _outproj(xp, mixed, p, tm=min(256, seq))

    rows_s = nb * t_new
    xs = x_sample.reshape(rows_s, D_MODEL)
    z_s, xbc_s, mla_s, gates_s = _inproj(xs, p["norm_in_w"], p["w_all"], tm=rows_s)
    y_norm_s, ssm_s, conv_s = _ssd(xbc_s.reshape(nb, t_new, CONV_DIM), z_s.reshape(nb, t_new, SSM_INNER),
                                   mla_s.reshape(nb, t_new, SEG_MLA), state_conv[0],
                                   state_ssm[0].reshape(nb, SSM_INNER, SSM_STATE), p, q_sample)
    ct_s, sn_s = _rope_tables(past + jnp.arange(t_new))
    ckv_s, kr_s, ql, qr = _mla_sample(mla_s, jnp.tile(ct_s, (nb, 1)), jnp.tile(sn_s, (nb, 1)), p)
    ckv_s = ckv_s.reshape(nb, t_new, KV_LORA)
    kr_s = kr_s.reshape(nb, t_new, QK_ROPE)
    attn_s = _attention_cached(ql, qr, cache_kv_latent[0], cache_k_rope[0], ckv_s, kr_s, p["w_uv"])
    mixed_s = _merge(y_norm_s.reshape(rows_s, SSM_INNER), attn_s.reshape(rows_s, MLA_INNER), gates_s, p, tm=rows_s)
    y_sample = _outproj(xs, mixed_s, p, tm=rows_s).reshape(nb, t_new, D_MODEL)

    hshape = (SSM_HEADS, SSM_HEAD_DIM, SSM_STATE)
    return (y_prompt[None], y_sample,
            ckv_p[None, None], kr_p[None, None], ssm_p.reshape((1, 1) + hshape), conv_p[None],
            ckv_s[None], kr_s[None], ssm_s.reshape((1, nb) + hshape), conv_s[None])
```

```python
import functools
import math

import jax
import jax.numpy as jnp
import numpy as np
from jax import lax
from jax.experimental import pallas as pl
from jax.experimental.pallas import tpu as pltpu

F32 = jnp.float32
BF16 = jnp.bfloat16

D_MODEL = 2048
CHUNK = 64
SSM_INNER = 4096
SSM_HEAD_DIM = 64
SSM_HEADS = 64
SSM_GROUPS = 8
SSM_STATE = 128
CONV_WIDTH = 4
CONV_DIM = SSM_INNER + 2 * SSM_GROUPS * SSM_STATE
GROUP_LANES = SSM_INNER // SSM_GROUPS
MLA_HEADS = 16
Q_LORA = 512
KV_LORA = 512
QK_NOPE = 128
QK_ROPE = 64
V_DIM = 128
MLA_INNER = MLA_HEADS * V_DIM
ROPE_THETA = 10000.0
EPS = 1e-6
QK_PAD = 256
SCALE = (QK_NOPE + QK_ROPE) ** -0.5
LOG2E = math.log2(math.e)
V_ROWS = V_DIM + 16

LANES = 128
VMEM_LIMIT = 56 * 1024 * 1024

IN_TN = 512
SEG_Z = SSM_INNER
SEG_XBC = CONV_DIM
SEG_MLA = Q_LORA + KV_LORA + 512
SEG_G = MLA_INNER + 2 * D_MODEL
IN_SEGS = (SEG_Z, SEG_XBC, SEG_MLA, SEG_G)


def _cparams(sem, vmem=VMEM_LIMIT):
    return pltpu.CompilerParams(dimension_semantics=sem, vmem_limit_bytes=vmem)


def _const_spec(shape):
    nd = len(shape)
    return pl.BlockSpec(shape, lambda *_: (0,) * nd, pipeline_mode=pl.Buffered(1))


def _sigmoid(x):
    return 0.5 + 0.5 * jnp.tanh(0.5 * x)


def _silu(x):
    h = 0.5 * x
    return h + h * jnp.tanh(h)


def _rms(x, w):
    return x * lax.rsqrt(jnp.mean(x * x, axis=-1, keepdims=True) + EPS) * w


def _dot(a, b):
    return jnp.dot(a, b, preferred_element_type=F32)


def _dot_nt(a, b):
    return lax.dot_general(a, b, (((1,), (1,)), ((), ())), preferred_element_type=F32)


def _dot_tn(a, b):
    return lax.dot_general(a, b, (((0,), (0,)), ((), ())), preferred_element_type=F32)


def _split3(v):
    hi = v.astype(BF16)
    r1 = v - hi.astype(F32)
    mid = r1.astype(BF16)
    lo = (r1 - mid.astype(F32)).astype(BF16)
    return hi, mid, lo


def _inproj_kernel(x_ref, nw_ref, w_ref, *rest, bounds):
    outs, h_ref = rest[:-1], rest[-1]
    j = pl.program_id(1)

    @pl.when(j == 0)
    def _():
        h_ref[...] = _rms(x_ref[...], nw_ref[...]).astype(BF16)

    for o_ref, (s, e) in zip(outs, bounds):
        @pl.when((j >= s) & (j < e))
        def _(o_ref=o_ref):
            o_ref[...] = _dot(h_ref[...], w_ref[...])


def _inproj(x, norm_w, w_all, tm):
    rows = x.shape[0]
    counts = [s // IN_TN for s in IN_SEGS]
    starts = np.cumsum([0] + counts)
    bounds = tuple((int(starts[k]), int(starts[k + 1])) for k in range(len(counts)))

    def omap(k):
        s, n = bounds[k][0], counts[k]
        return lambda i, j: (i, jnp.clip(j - s, 0, n - 1))

    return pl.pallas_call(
        functools.partial(_inproj_kernel, bounds=bounds),
        grid=(rows // tm, int(starts[-1])),
        in_specs=[
            pl.BlockSpec((tm, D_MODEL), lambda i, j: (i, 0)),
            pl.BlockSpec((1, D_MODEL), lambda i, j: (0, 0)),
            pl.BlockSpec((D_MODEL, IN_TN), lambda i, j: (0, j)),
        ],
        out_specs=[pl.BlockSpec((tm, IN_TN), omap(k)) for k in range(len(counts))],
        out_shape=[jax.ShapeDtypeStruct((rows, s), F32) for s in IN_SEGS],
        scratch_shapes=[pltpu.VMEM((tm, D_MODEL), BF16)],
        compiler_params=_cparams(("arbitrary", "arbitrary")),
        name="inproj",
    )(x, norm_w, w_all)


CONV_PAD = 8


def _ssd_kernel(xbc_ref, z_ref, sm_ref, cprev_ref, s0_ref, cw_ref, cb_ref, dtb_ref, alog_ref, dsk_ref,
                nw_ref, ep_ref, eq_ref, y_ref, sout_ref, cout_ref, st_ref, xpad_ref, xc_ref, exp_p_ref,
                exp_q_ref, yd_ref, *, q):
    c = pl.program_id(1)
    nc = pl.num_programs(1)
    hp = LANES // q
    tiles_per_group = (SSM_HEADS // SSM_GROUPS) // hp
    tile_w = hp * SSM_HEAD_DIM
    lo = CONV_PAD - (CONV_WIDTH - 1)

    @pl.when(c == 0)
    def _():
        st_ref[...] = s0_ref[0].T
        xpad_ref[lo:CONV_PAD, :] = cprev_ref[0]

    xpad_ref[CONV_PAD:CONV_PAD + q, :] = xbc_ref[0]
    for s in range(0, CONV_DIM, 1024):
        acc = cb_ref[:, s:s + 1024] + xpad_ref[lo:lo + q, s:s + 1024] * cw_ref[0:1, s:s + 1024]
        for k in range(1, CONV_WIDTH):
            acc = acc + xpad_ref[lo + k:lo + k + q, s:s + 1024] * cw_ref[k:k + 1, s:s + 1024]
        xc_ref[:, s:s + 1024] = _silu(acc)
    tail_rows = xpad_ref[CONV_PAD + q - (CONV_WIDTH - 1):CONV_PAD + q, :]
    xpad_ref[lo:CONV_PAD, :] = tail_rows

    xdt = sm_ref[0][:, 0:SSM_HEADS] + dtb_ref[...]
    dt = jnp.maximum(xdt, 0.0) + jnp.log1p(jnp.exp(-jnp.abs(xdt)))
    adt = dt * (-jnp.exp(alog_ref[...]))
    ri = lax.broadcasted_iota(jnp.int32, (q, q), 0)
    ci = lax.broadcasted_iota(jnp.int32, (q, q), 1)
    tril = jnp.where(ci <= ri, 1.0, 0.0).astype(BF16)
    h3 = _split3(adt)
    acs = _dot(tril, h3[0]) + _dot(tril, h3[1]) + _dot(tril, h3[2])

    st3 = jnp.concatenate([piece.astype(F32) for piece in _split3(jnp.concatenate([acs, dt], axis=0))],
                          axis=1).astype(BF16)
    exp_p_ref[...] = _dot(st3, ep_ref[...])
    if q == SSM_HEAD_DIM:
        xq_ref = exp_p_ref
    else:
        exp_q_ref[...] = _dot(st3, eq_ref[...])
        xq_ref = exp_q_ref

    r128 = lax.broadcasted_iota(jnp.int32, (q, LANES), 0)
    c128 = lax.broadcasted_iota(jnp.int32, (q, LANES), 1)
    key128 = c128 & (q - 1)
    causal = key128 <= r128
    diag = key128 == r128
    br = lax.broadcasted_iota(jnp.int32, (LANES, tile_w), 0)
    bc = lax.broadcasted_iota(jnp.int32, (LANES, tile_w), 1)
    blockdiag = (br // q) == (bc // SSM_HEAD_DIM)

    for g in range(SSM_GROUPS):
        gs = g * GROUP_LANES
        b_g = xc_ref[:, SSM_INNER + g * SSM_STATE:SSM_INNER + (g + 1) * SSM_STATE]
        c_g = xc_ref[:, SSM_INNER + (SSM_GROUPS + g) * SSM_STATE:SSM_INNER + (SSM_GROUPS + g + 1) * SSM_STATE]
        b_bf = b_g.astype(BF16)
        c_bf = c_g.astype(BF16)
        cb_t = _dot_nt(c_bf, jnp.concatenate([b_bf] * hp, axis=0))

        for tt in range(tiles_per_group):
            t = g * tiles_per_group + tt
            a_t = xq_ref[0:q, t * LANES:(t + 1) * LANES]
            d_t = xq_ref[q:2 * q, t * LANES:(t + 1) * LANES]
            a_key = jnp.sum(jnp.where(diag, a_t, 0.0), axis=0, keepdims=True)
            d_key = jnp.sum(jnp.where(diag, d_t, 0.0), axis=0, keepdims=True)
            w_t = jnp.exp(jnp.where(causal, a_t - a_key, -jnp.inf)) * d_key * cb_t
            x_t = xc_ref[:, t * tile_w:(t + 1) * tile_w]
            rhs = jnp.where(blockdiag, jnp.concatenate([x_t] * hp, axis=0), 0.0).astype(BF16)
            yd_ref[:, t * tile_w:(t + 1) * tile_w] = _dot(w_t.astype(BF16), rhs)

        a_p = exp_p_ref[0:q, gs:gs + GROUP_LANES]
        d_p = exp_p_ref[q:2 * q, gs:gs + GROUP_LANES]
        a_last = exp_p_ref[q - 1:q, gs:gs + GROUP_LANES]
        xs_g = xc_ref[:, gs:gs + GROUP_LANES]
        st_g = st_ref[:, gs:gs + GROUP_LANES]
        y_off = _dot(c_bf, st_g.astype(BF16)) * jnp.exp(a_p)
        xt = (xs_g * (jnp.exp(a_last - a_p) * d_p)).astype(BF16)
        st_ref[:, gs:gs + GROUP_LANES] = st_g * jnp.exp(a_last) + _dot_tn(b_bf, xt)

        y = yd_ref[:, gs:gs + GROUP_LANES] + y_off + dsk_ref[:, gs:gs + GROUP_LANES] * xs_g
        y = y * _silu(z_ref[0][:, gs:gs + GROUP_LANES])
        y = y * lax.rsqrt(jnp.mean(y * y, axis=-1, keepdims=True) + EPS)
        y_ref[0, :, gs:gs + GROUP_LANES] = (y * nw_ref[:, gs:gs + GROUP_LANES]).astype(y_ref.dtype)

    @pl.when(c == nc - 1)
    def _():
        sout_ref[0] = st_ref[...].T
        cout_ref[0] = tail_rows


def _ssd(xbc, z, mla, conv_prev, s0, p, q):
    nb, lb = xbc.shape[0], xbc.shape[1]
    nc = lb // q
    kq = SSM_HEADS * q
    in_specs = [
        pl.BlockSpec((1, q, CONV_DIM), lambda b, c: (b, c, 0)),
        pl.BlockSpec((1, q, SSM_INNER), lambda b, c: (b, c, 0)),
        pl.BlockSpec((1, q, 512), lambda b, c: (b, c, (Q_LORA + KV_LORA) // 512)),
        pl.BlockSpec((1, CONV_WIDTH - 1, CONV_DIM), lambda b, c: (b, 0, 0)),
        pl.BlockSpec((1, SSM_INNER, SSM_STATE), lambda b, c: (b, 0, 0)),
        _const_spec((CONV_WIDTH, CONV_DIM)),
        _const_spec((1, CONV_DIM)),
        _const_spec((1, SSM_HEADS)),
        _const_spec((1, SSM_HEADS)),
        _const_spec((1, SSM_INNER)),
        _const_spec((1, SSM_INNER)),
        _const_spec((3 * SSM_HEADS, SSM_INNER)),
        _const_spec((3 * SSM_HEADS, kq)),
    ]
    out_specs = [
        pl.BlockSpec((1, q, SSM_INNER), lambda b, c: (b, c, 0)),
        pl.BlockSpec((1, SSM_INNER, SSM_STATE), lambda b, c: (b, 0, 0)),
        pl.BlockSpec((1, CONV_WIDTH - 1, CONV_DIM), lambda b, c: (b, 0, 0)),
    ]
    out_shape = [
        jax.ShapeDtypeStruct((nb, lb, SSM_INNER), BF16),
        jax.ShapeDtypeStruct((nb, SSM_INNER, SSM_STATE), F32),
        jax.ShapeDtypeStruct((nb, CONV_WIDTH - 1, CONV_DIM), F32),
    ]
    scratch = [
        pltpu.VMEM((SSM_STATE, SSM_INNER), F32),
        pltpu.VMEM((CONV_PAD + q, CONV_DIM), F32),
        pltpu.VMEM((q, CONV_DIM), F32),
        pltpu.VMEM((2 * q, SSM_INNER), F32),
        pltpu.VMEM((2 * q, kq), F32),
        pltpu.VMEM((q, SSM_INNER), F32),
    ]
    return pl.pallas_call(
        functools.partial(_ssd_kernel, q=q),
        grid=(nb, nc),
        in_specs=in_specs,
        out_specs=out_specs,
        out_shape=out_shape,
        scratch_shapes=scratch,
        compiler_params=_cparams(("arbitrary", "arbitrary")),
        name="ssd",
    )(xbc, z, mla, conv_prev, s0, p["conv_w"], p["conv_b"], p["dt_bias"], p["a_log"], p["d_skip_p"],
      p["ssm_norm_w"], p["e_p"], p["e_q%d" % q])


def _mla_latents(mla_ref, ct_ref, sn_ref, qnw_ref, kvnw_ref, ckv_ref, kr_ref):
    m = mla_ref[...]
    cqn = _rms(m[:, 0:Q_LORA], qnw_ref[...])
    ckv = _rms(m[:, Q_LORA:Q_LORA + KV_LORA], kvnw_ref[...])
    ckv_ref[...] = ckv
    small = m[:, Q_LORA + KV_LORA:]
    ct, sn = ct_ref[...], sn_ref[...]
    kr128 = small[:, 128:256] * ct + small[:, 256:384] * sn
    kr_ref[...] = kr128[:, 0:QK_ROPE]
    return cqn, ckv, kr128, ct, sn


def _mla_prompt_kernel(mla_ref, ct_ref, sn_ref, ctt_ref, snt_ref, qnw_ref, kvnw_ref, wqat_ref, wqbt_ref, wk_ref,
                       wvt_ref, ckv_ref, kr_ref, qt_ref, k_ref, vt_ref):
    cqn, ckv, kr128, _, _ = _mla_latents(mla_ref, ct_ref, sn_ref, qnw_ref, kvnw_ref, ckv_ref, kr_ref)
    cqn_t = cqn.T.astype(BF16)
    ckv_t = ckv.T.astype(BF16)
    ckv_bf = ckv.astype(BF16)
    kr_bf = kr128.astype(BF16)
    ctt, snt = ctt_ref[...], snt_ref[...]
    tm = ckv_bf.shape[0]
    ones_row = jnp.where(lax.broadcasted_iota(jnp.int32, (V_ROWS - V_DIM, tm), 0) == 0, 1.0, 0.0).astype(BF16)
    qs = SCALE * LOG2E
    for h in range(MLA_HEADS):
        qa = _dot(wqat_ref[h * QK_PAD:(h + 1) * QK_PAD, :], cqn_t)
        qb = _dot(wqbt_ref[h * LANES:(h + 1) * LANES, :], cqn_t)
        qt_ref[h, 0:QK_NOPE, :] = (qa[0:QK_NOPE] * qs).astype(BF16)
        qt_ref[h, QK_NOPE:QK_PAD, :] = ((qa[QK_NOPE:QK_PAD] * ctt + qb * snt) * qs).astype(BF16)
        k_ref[h, :, 0:QK_NOPE] = _dot(ckv_bf, wk_ref[:, h * QK_NOPE:(h + 1) * QK_NOPE]).astype(BF16)
        k_ref[h, :, QK_NOPE:QK_PAD] = kr_bf
        vt_ref[h, 0, 0:V_DIM, :] = _dot(wvt_ref[h * V_DIM:(h + 1) * V_DIM, :], ckv_t).astype(BF16)
        vt_ref[h, 0, V_DIM:V_ROWS, :] = ones_row


def _mla_prompt(mla, ct, sn, p, tm):
    rows = mla.shape[0]
    row = lambda w: pl.BlockSpec((tm, w), lambda i: (i, 0))
    col = pl.BlockSpec((LANES, tm), lambda i: (0, i))
    return pl.pallas_call(
        _mla_prompt_kernel,
        grid=(rows // tm,),
        in_specs=[row(SEG_MLA), row(LANES), row(LANES), col, col, _const_spec((1, Q_LORA)),
                  _const_spec((1, KV_LORA)),
                  _const_spec((MLA_HEADS * QK_PAD, Q_LORA)), _const_spec((MLA_HEADS * LANES, Q_LORA)),
                  _const_spec((KV_LORA, MLA_HEADS * QK_NOPE)), _const_spec((MLA_INNER, KV_LORA))],
        out_specs=[row(KV_LORA), row(QK_ROPE),
                   pl.BlockSpec((MLA_HEADS, QK_PAD, tm), lambda i: (0, 0, i)),
                   pl.BlockSpec((MLA_HEADS, tm, QK_PAD), lambda i: (0, i, 0)),
                   pl.BlockSpec((MLA_HEADS, 1, V_ROWS, tm), lambda i: (0, i, 0, 0))],
        out_shape=[jax.ShapeDtypeStruct((rows, KV_LORA), F32), jax.ShapeDtypeStruct((rows, QK_ROPE), F32),
                   jax.ShapeDtypeStruct((MLA_HEADS, QK_PAD, rows), BF16),
                   jax.ShapeDtypeStruct((MLA_HEADS, rows, QK_PAD), BF16),
                   jax.ShapeDtypeStruct((MLA_HEADS, rows // tm, V_ROWS, tm), BF16)],
        compiler_params=_cparams(("arbitrary",)),
        name="mla_prep_prompt",
    )(mla, ct, sn, ct.T, sn.T, p["q_norm_w"], p["kv_norm_w"], p["wq_a"].T, p["wq_b"].T, p["w_uk"], p["w_uv_t"])


def _mla_sample_kernel(mla_ref, ct_ref, sn_ref, qnw_ref, kvnw_ref, wqa_ref, wqb_ref, wk_ref,
                       ckv_ref, kr_ref, ql_ref, qr_ref):
    cqn, _, _, ct, sn = _mla_latents(mla_ref, ct_ref, sn_ref, qnw_ref, kvnw_ref, ckv_ref, kr_ref)
    cqn = cqn.astype(BF16)
    qa = _dot(cqn, wqa_ref[...])
    qb = _dot(cqn, wqb_ref[...])
    for h in range(MLA_HEADS):
        o = h * QK_PAD
        q_rope = qa[:, o + QK_NOPE:o + QK_PAD] * ct + qb[:, h * LANES:(h + 1) * LANES] * sn
        qr_ref[h] = (q_rope * SCALE).astype(BF16)
        q_nope = qa[:, o:o + QK_NOPE].astype(BF16)
        q_lat = _dot_nt(q_nope, wk_ref[:, h * QK_NOPE:(h + 1) * QK_NOPE])
        ql_ref[h] = (q_lat * SCALE).astype(BF16)


def _mla_sample(mla, ct, sn, p):
    rows = mla.shape[0]
    full = lambda w: pl.BlockSpec((rows, w), lambda i: (0, 0))
    return pl.pallas_call(
        _mla_sample_kernel,
        grid=(1,),
        in_specs=[full(SEG_MLA), full(LANES), full(LANES), _const_spec((1, Q_LORA)), _const_spec((1, KV_LORA)),
                  _const_spec((Q_LORA, MLA_HEADS * QK_PAD)), _const_spec((Q_LORA, MLA_HEADS * LANES)),
                  _const_spec((KV_LORA, MLA_HEADS * QK_NOPE))],
        out_specs=[full(KV_LORA), full(QK_ROPE),
                   pl.BlockSpec((MLA_HEADS, rows, KV_LORA), lambda i: (0, 0, 0)),
                   pl.BlockSpec((MLA_HEADS, rows, LANES), lambda i: (0, 0, 0))],
        out_shape=[jax.ShapeDtypeStruct((rows, KV_LORA), F32), jax.ShapeDtypeStruct((rows, QK_ROPE), F32),
                   jax.ShapeDtypeStruct((MLA_HEADS, rows, KV_LORA), BF16),
                   jax.ShapeDtypeStruct((MLA_HEADS, rows, LANES), BF16)],
        compiler_params=_cparams(("arbitrary",)),
        name="mla_prep_sample",
    )(mla, ct, sn, p["q_norm_w"], p["kv_norm_w"], p["wq_a"], p["wq_b"], p["w_uk"])


def _attn_kernel(qi_ref, kj_ref, qt_ref, k_ref, vt_ref, o_ref, m_ref, acc_ref, s0_ref, s1_ref, p0_ref,
                 p1_ref, a0_ref, a1_ref, *, tq, tks):
    s_refs, p_refs, a_refs = (s0_ref, s1_ref), (p0_ref, p1_ref), (a0_ref, a1_ref)
    n_sub = len(s_refs)
    t = pl.program_id(0)
    i = qi_ref[t]
    j = kj_ref[t]
    tk = n_sub * tks
    last_j = ((i + 1) * tq - 1) // tk

    @pl.when(j == 0)
    def _():
        m_ref[...] = jnp.full(m_ref.shape, -jnp.inf, F32)
        acc_ref[...] = jnp.zeros(acc_ref.shape, F32)

    def step(masked):
        if masked:
            q_chunk = (i * tq + lax.broadcasted_iota(jnp.int32, (tks, tq), 1)) // CHUNK
            k_chunk = (j * tk + lax.broadcasted_iota(jnp.int32, (tks, tq), 0)) // CHUNK
            visible = [k_chunk + (c * tks) // CHUNK <= q_chunk for c in range(n_sub)]

        def scores(h, c):
            s_refs[c][...] = _dot(k_ref[h, c * tks:(c + 1) * tks, :], qt_ref[h])

        def softmax(h, c):
            s = s_refs[c][...]
            if masked:
                s = jnp.where(visible[c], s, -jnp.inf)
            m_prev = m_ref[h]
            m_new = jnp.maximum(m_prev, jnp.max(s, axis=0, keepdims=True))
            m_ref[h] = m_new
            p_refs[c][...] = jnp.exp2(s - m_new).astype(BF16)
            a_refs[c][...] = jnp.exp2(m_prev - m_new)

        def values(h, c):
            acc_ref[h] = acc_ref[h] * a_refs[c][...] + _dot(vt_ref[h, c], p_refs[c][...])

        scores(0, 0)
        scores(0, 1)
        softmax(0, 0)

        for h in range(1, MLA_HEADS):
            scores(h, 0)
            softmax(h - 1, 1)
            values(h - 1, 0)
            scores(h, 1)
            softmax(h, 0)
            values(h - 1, 1)
        softmax(MLA_HEADS - 1, 1)
        values(MLA_HEADS - 1, 0)
        values(MLA_HEADS - 1, 1)

    @pl.when(j < last_j)
    def _():
        step(False)

    @pl.when(j == last_j)
    def _():
        step(True)
        for h in range(MLA_HEADS):
            den = acc_ref[h, V_DIM:V_DIM + 1, :]
            o_ref[:, h * V_DIM:(h + 1) * V_DIM] = (acc_ref[h, 0:V_DIM, :] * (1.0 / den)).T


def _attention_prompt(qt, kh, vt, tq):
    rows = kh.shape[1]
    n_sub, tks = 2, vt.shape[3]
    tk = tks * n_sub
    qi, kj = [], []
    for i in range(rows // tq):
        for j in range(((i + 1) * tq - 1) // tk + 1):
            qi.append(i)
            kj.append(j)
    qi = jnp.asarray(np.array(qi, np.int32))
    kj = jnp.asarray(np.array(kj, np.int32))
    grid_spec = pltpu.PrefetchScalarGridSpec(
        num_scalar_prefetch=2,
        grid=(int(qi.shape[0]),),
        in_specs=[
            pl.BlockSpec((MLA_HEADS, QK_PAD, tq), lambda t, qi, kj: (0, 0, qi[t])),
            pl.BlockSpec((MLA_HEADS, tk, QK_PAD), lambda t, qi, kj: (0, kj[t], 0)),
            pl.BlockSpec((MLA_HEADS, n_sub, V_ROWS, tks), lambda t, qi, kj: (0, kj[t], 0, 0)),
        ],
        out_specs=pl.BlockSpec((tq, MLA_INNER), lambda t, qi, kj: (qi[t], 0)),
        scratch_shapes=[
            pltpu.VMEM((MLA_HEADS, 1, tq), F32),
            pltpu.VMEM((MLA_HEADS, V_ROWS, tq), F32),
            pltpu.VMEM((tks, tq), F32), pltpu.VMEM((tks, tq), F32),
            pltpu.VMEM((tks, tq), BF16), pltpu.VMEM((tks, tq), BF16),
            pltpu.VMEM((1, tq), F32), pltpu.VMEM((1, tq), F32),
        ],
    )
    return pl.pallas_call(
        functools.partial(_attn_kernel, tq=tq, tks=tks),
        grid_spec=grid_spec,
        out_shape=jax.ShapeDtypeStruct((rows, MLA_INNER), F32),
        compiler_params=_cparams(("arbitrary",)),
        name="attn_prompt",
    )(qi, kj, qt, kh, vt)


def _attn_cached_kernel(ql_ref, qr_ref, ckv_c_ref, kr_c_ref, ckv_n_ref, kr_n_ref, wv_ref, o_ref, *, t_new):
    rows = MLA_HEADS * t_new
    ql = ql_ref[...].reshape(rows, KV_LORA)
    qr = qr_ref[...].reshape(rows, LANES)[:, 0:QK_ROPE]
    kc = ckv_c_ref[0].astype(BF16)
    kn = ckv_n_ref[0].astype(BF16)
    s_c = _dot_nt(ql, kc) + _dot_nt(qr, kr_c_ref[0].astype(BF16))
    s_n = _dot_nt(ql, kn) + _dot_nt(qr, kr_n_ref[0].astype(BF16))
    m = jnp.maximum(jnp.max(s_c, axis=1, keepdims=True), jnp.max(s_n, axis=1, keepdims=True))
    p_c = jnp.exp(s_c - m)
    p_n = jnp.exp(s_n - m)
    den = jnp.sum(p_c, axis=1, keepdims=True) + jnp.sum(p_n, axis=1, keepdims=True)
    o_lat = (_dot(p_c.astype(BF16), kc) + _dot(p_n.astype(BF16), kn)) / den
    o_bf = o_lat.astype(BF16)
    for h in range(MLA_HEADS):
        o_ref[0, :, h * V_DIM:(h + 1) * V_DIM] = _dot(o_bf[h * t_new:(h + 1) * t_new, :],
                                                        wv_ref[:, h * V_DIM:(h + 1) * V_DIM])


def _attention_cached(ql, qr, cache_kv, cache_kr, ckv_new, kr_new, w_uv):
    nb, past = cache_kv.shape[0], cache_kv.shape[1]
    t_new = ckv_new.shape[1]
    return pl.pallas_call(
        functools.partial(_attn_cached_kernel, t_new=t_new),
        grid=(nb,),
        in_specs=[
            pl.BlockSpec((MLA_HEADS, t_new, KV_LORA), lambda b: (0, b, 0)),
            pl.BlockSpec((MLA_HEADS, t_new, LANES), lambda b: (0, b, 0)),
            pl.BlockSpec((1, past, KV_LORA), lambda b: (b, 0, 0)),
            pl.BlockSpec((1, past, QK_ROPE), lambda b: (b, 0, 0)),
            pl.BlockSpec((1, t_new, KV_LORA), lambda b: (b, 0, 0)),
            pl.BlockSpec((1, t_new, QK_ROPE), lambda b: (b, 0, 0)),
            _const_spec((KV_LORA, MLA_INNER)),
        ],
        out_specs=pl.BlockSpec((1, t_new, MLA_INNER), lambda b: (b, 0, 0)),
        out_shape=jax.ShapeDtypeStruct((nb, t_new, MLA_INNER), F32),
        compiler_params=_cparams(("arbitrary",)),
        name="attn_cached",
    )(ql, qr, cache_kv, cache_kr, ckv_new, kr_new, w_uv)


def _merge_kernel(y_ref, attn_ref, gm_ref, gs_ref, ga_ref, wssm_ref, wmla_ref, o_ref):
    y_ssm = _dot(y_ref[...], wssm_ref[...])
    y_mla = _dot((attn_ref[...] * _silu(gm_ref[...])).astype(BF16), wmla_ref[...])
    o_ref[...] = (_sigmoid(gs_ref[...]) * y_ssm + _sigmoid(ga_ref[...]) * y_mla).astype(o_ref.dtype)


def _merge(y_norm, attn, gates, p, tm):
    rows = y_norm.shape[0]
    gate = lambda k: pl.BlockSpec((tm, D_MODEL), lambda i: (i, k))
    return pl.pallas_call(
        _merge_kernel,
        grid=(rows // tm,),
        in_specs=[pl.BlockSpec((tm, SSM_INNER), lambda i: (i, 0)), pl.BlockSpec((tm, MLA_INNER), lambda i: (i, 0)),
                  gate(0), gate(1), gate(2),
                  _const_spec((SSM_INNER, D_MODEL)), _const_spec((MLA_INNER, D_MODEL))],
        out_specs=pl.BlockSpec((tm, D_MODEL), lambda i: (i, 0)),
        out_shape=jax.ShapeDtypeStruct((rows, D_MODEL), BF16),
        compiler_params=_cparams(("arbitrary",)),
        name="merge",
    )(y_norm, attn, gates, gates, gates, p["w_ssm_out"], p["w_mla_out"])


def _outproj_kernel(x_ref, mixed_ref, w_ref, nw_ref, o_ref):
    x_out = x_ref[...] + _dot(mixed_ref[...], w_ref[...])
    o_ref[...] = _rms(x_out, nw_ref[...])


def _outproj(x, mixed, p, tm):
    rows = x.shape[0]
    row = pl.BlockSpec((tm, D_MODEL), lambda i: (i, 0))
    return pl.pallas_call(
        _outproj_kernel,
        grid=(rows // tm,),
        in_specs=[row, row, _const_spec((D_MODEL, D_MODEL)), _const_spec((1, D_MODEL))],
        out_specs=row,
        out_shape=jax.ShapeDtypeStruct((rows, D_MODEL), F32),
        compiler_params=_cparams(("arbitrary",)),
        name="outproj",
    )(x, mixed, p["w_out"], p["final_norm_w"])


def _prepare(norm_in_w, w_in, conv_w, conv_b, dt_bias, a_log, d_skip, ssm_norm_w, w_ssm_out, q_norm_w, w_q_up,
             kv_norm_w, w_kv_up, w_mla_out, w_out, final_norm_w, chunk_lens):
    offs = np.cumsum((0, SSM_INNER, CONV_DIM, SSM_HEADS, Q_LORA, KV_LORA, QK_ROPE, MLA_INNER, D_MODEL, D_MODEL))
    w_in_bf = w_in.astype(BF16)
    col = lambda k: w_in_bf[:, int(offs[k]):int(offs[k + 1])]
    half = QK_ROPE // 2
    kr_w = col(5)
    zeros = lambda n: jnp.zeros((D_MODEL, n), BF16)
    w_all = jnp.concatenate([
        col(0), col(1), col(3), col(4),
        col(2), zeros(64), kr_w, zeros(64), kr_w[:, half:], kr_w[:, :half], zeros(64 + 128),
        col(6), col(7), col(8)], axis=1)

    wq = w_q_up.reshape(Q_LORA, MLA_HEADS, QK_NOPE + QK_ROPE)
    wq_rope = wq[..., QK_NOPE:]
    zq = jnp.zeros((Q_LORA, MLA_HEADS, 64), w_q_up.dtype)
    wq_a = jnp.concatenate([wq, zq], axis=-1).reshape(Q_LORA, MLA_HEADS * QK_PAD).astype(BF16)
    wq_b = jnp.concatenate([wq_rope[..., half:], wq_rope[..., :half], zq], axis=-1)
    wq_b = wq_b.reshape(Q_LORA, MLA_HEADS * LANES).astype(BF16)

    wkv = w_kv_up.reshape(KV_LORA, MLA_HEADS, QK_NOPE + V_DIM)
    w_uk = wkv[..., :QK_NOPE].reshape(KV_LORA, MLA_HEADS * QK_NOPE).astype(BF16)
    w_uv_h = wkv[..., QK_NOPE:]
    w_uv = w_uv_h.reshape(KV_LORA, MLA_INNER).astype(BF16)
    w_uv_t = w_uv.T

    head_of_lane = np.arange(SSM_INNER) // SSM_HEAD_DIM
    p = dict(
        norm_in_w=norm_in_w.reshape(1, D_MODEL), w_all=w_all,
        conv_w=conv_w, conv_b=conv_b.reshape(1, CONV_DIM), dt_bias=dt_bias.reshape(1, SSM_HEADS),
        a_log=a_log.reshape(1, SSM_HEADS), d_skip_p=jnp.repeat(d_skip, SSM_HEAD_DIM).reshape(1, SSM_INNER),
        ssm_norm_w=ssm_norm_w.reshape(1, SSM_INNER), w_ssm_out=w_ssm_out.astype(BF16),
        q_norm_w=q_norm_w.reshape(1, Q_LORA), kv_norm_w=kv_norm_w.reshape(1, KV_LORA),
        wq_a=wq_a, wq_b=wq_b, w_uk=w_uk, w_uv=w_uv, w_uv_t=w_uv_t,
        w_mla_out=w_mla_out.astype(BF16), w_out=w_out.astype(BF16),
        final_norm_w=final_norm_w.reshape(1, D_MODEL),
        e_p=jnp.asarray(np.tile(head_of_lane[None, :] == np.arange(SSM_HEADS)[:, None], (3, 1)), BF16),
    )
    for q in chunk_lens:
        head_of_key_lane = np.arange(SSM_HEADS * q) // q
        p["e_q%d" % q] = jnp.asarray(np.tile(head_of_key_lane[None, :] == np.arange(SSM_HEADS)[:, None], (3, 1)),
                                     BF16)
    return p


def _rope_tables(pos):
    half = QK_ROPE // 2
    inv = 1.0 / (ROPE_THETA ** (jnp.arange(half, dtype=F32) * (2.0 / QK_ROPE)))
    ang = pos.astype(F32)[:, None] * inv[None, :]
    cos, sin = jnp.cos(ang), jnp.sin(ang)
    pad = jnp.zeros((pos.shape[0], LANES - QK_ROPE), F32)
    return jnp.concatenate([cos, cos, pad], axis=1), jnp.concatenate([-sin, sin, pad], axis=1)


def kernel(x_prompt, x_sample, cache_kv_latent, cache_k_rope, state_ssm, state_conv, norm_in_w, w_in, conv_w,
           conv_b, dt_bias, a_log, d_skip, ssm_norm_w, w_ssm_out, q_norm_w, w_q_up, kv_norm_w, w_kv_up, w_mla_out,
           w_out, final_norm_w):
    depth = w_in.shape[0]
    assert depth == 1 and x_prompt.shape[0] == 1
    seq = x_prompt.shape[1]
    nb, t_new = x_sample.shape[0], x_sample.shape[1]
    past = cache_kv_latent.shape[2]
    q_prompt, q_sample = min(CHUNK, seq), min(CHUNK, t_new)
    p = _prepare(norm_in_w[0], w_in[0], conv_w[0], conv_b[0], dt_bias[0], a_log[0], d_skip[0], ssm_norm_w[0],
                 w_ssm_out[0], q_norm_w[0], w_q_up[0], kv_norm_w[0], w_kv_up[0], w_mla_out[0], w_out[0],
                 final_norm_w, sorted({q_prompt, q_sample}))

    xp = x_prompt[0]
    z, xbc, mla, gates = _inproj(xp, p["norm_in_w"], p["w_all"], tm=min(1024, seq))
    y_norm, ssm_p, conv_p = _ssd(xbc[None], z[None], mla[None],
                                 jnp.zeros((1, CONV_WIDTH - 1, CONV_DIM), F32),
                                 jnp.zeros((1, SSM_INNER, SSM_STATE), F32), p, q_prompt)
    ct, sn = _rope_tables(jnp.arange(seq))
    ckv_p, kr_p, qt, kh, vt = _mla_prompt(mla, ct, sn, p, tm=min(512, seq // 2))
    attn = _attention_prompt(qt, kh, vt, tq=min(512, seq))
    mixed = _merge(y_norm[0], attn, gates, p, tm=min(256, seq))
    y_prompt = _outproj(xp, mixed, p, tm=min(256, seq))

    rows_s = nb * t_new
    xs = x_sample.reshape(rows_s, D_MODEL)
    z_s, xbc_s, mla_s, gates_s = _inproj(xs, p["norm_in_w"], p["w_all"], tm=rows_s)
    y_norm_s, ssm_s, conv_s = _ssd(xbc_s.reshape(nb, t_new, CONV_DIM), z_s.reshape(nb, t_new, SSM_INNER),
                                   mla_s.reshape(nb, t_new, SEG_MLA), state_conv[0],
                                   state_ssm[0].reshape(nb, SSM_INNER, SSM_STATE), p, q_sample)
    ct_s, sn_s = _rope_tables(past + jnp.arange(t_new))
    ckv_s, kr_s, ql, qr = _mla_sample(mla_s, jnp.tile(ct_s, (nb, 1)), jnp.tile(sn_s, (nb, 1)), p)
    ckv_s = ckv_s.reshape(nb, t_new, KV_LORA)
    kr_s = kr_s.reshape(nb, t_new, QK_ROPE)
    attn_s = _attention_cached(ql, qr, cache_kv_latent[0], cache_k_rope[0], ckv_s, kr_s, p["w_uv"])
    mixed_s = _merge(y_norm_s.reshape(rows_s, SSM_INNER), attn_s.reshape(rows_s, MLA_INNER), gates_s, p, tm=rows_s)
    y_sample = _outproj(xs, mixed_s, p, tm=rows_s).reshape(nb, t_new, D_MODEL)

    hshape = (SSM_HEADS, SSM_HEAD_DIM, SSM_STATE)
    return (y_prompt[None], y_sample,
            ckv_p[None, None], kr_p[None, None], ssm_p.reshape((1, 1) + hshape), conv_p[None],
            ckv_s[None], kr_s[None], ssm_s.reshape((1, nb) + hshape), conv_s[None])
```

```python
import functools
import math

import jax
import jax.numpy as jnp
import numpy as np
from jax import lax
from jax.experimental import pallas as pl
from jax.experimental.pallas import tpu as pltpu

F32 = jnp.float32
BF16 = jnp.bfloat16

D_MODEL = 2048
CHUNK = 64
SSM_INNER = 4096
SSM_HEAD_DIM = 64
SSM_HEADS = 64
SSM_GROUPS = 8
SSM_STATE = 128
CONV_WIDTH = 4
CONV_DIM = SSM_INNER + 2 * SSM_GROUPS * SSM_STATE
GROUP_LANES = SSM_INNER // SSM_GROUPS
MLA_HEADS = 16
Q_LORA = 512
KV_LORA = 512
QK_NOPE = 128
QK_ROPE = 64
V_DIM = 128
MLA_INNER = MLA_HEADS * V_DIM
ROPE_THETA = 10000.0
EPS = 1e-6
QK_PAD = 256
SCALE = (QK_NOPE + QK_ROPE) ** -0.5
LOG2E = math.log2(math.e)
V_ROWS = V_DIM + 16

LANES = 128
VMEM_LIMIT = 56 * 1024 * 1024

IN_TN = 512
SEG_Z = SSM_INNER
SEG_XBC = CONV_DIM
SEG_MLA = Q_LORA + KV_LORA + 512
SEG_G = MLA_INNER + 2 * D_MODEL
IN_SEGS = (SEG_Z, SEG_XBC, SEG_MLA, SEG_G)


def _cparams(sem, vmem=VMEM_LIMIT):
    return pltpu.CompilerParams(dimension_semantics=sem, vmem_limit_bytes=vmem)


def _const_spec(shape):
    nd = len(shape)
    return pl.BlockSpec(shape, lambda *_: (0,) * nd, pipeline_mode=pl.Buffered(1))


def _sigmoid(x):
    return 0.5 + 0.5 * jnp.tanh(0.5 * x)


def _silu(x):
    h = 0.5 * x
    return h + h * jnp.tanh(h)


def _rms(x, w):
    return x * lax.rsqrt(jnp.mean(x * x, axis=-1, keepdims=True) + EPS) * w


def _dot(a, b):
    return jnp.dot(a, b, preferred_element_type=F32)


def _dot_nt(a, b):
    return lax.dot_general(a, b, (((1,), (1,)), ((), ())), preferred_element_type=F32)


def _dot_tn(a, b):
    return lax.dot_general(a, b, (((0,), (0,)), ((), ())), preferred_element_type=F32)


def _split3(v):
    hi = v.astype(BF16)
    r1 = v - hi.astype(F32)
    mid = r1.astype(BF16)
    lo = (r1 - mid.astype(F32)).astype(BF16)
    return hi, mid, lo


def _inproj_kernel(x_ref, nw_ref, wm_ref, wt_ref, *rest, bounds, n_main):
    outs, h_ref = rest[:-1], rest[-1]
    j = pl.program_id(1)

    @pl.when(j == 0)
    def _():
        h_ref[...] = _rms(x_ref[...], nw_ref[...]).astype(BF16)

    for o_ref, (s, e) in zip(outs, bounds):
        w_ref = wm_ref if e <= n_main else wt_ref

        @pl.when((j >= s) & (j < e))
        def _(o_ref=o_ref, w_ref=w_ref):
            o_ref[...] = _dot(h_ref[...], w_ref[...])


def _inproj(x, norm_w, w_main, w_tail, tm):
    rows = x.shape[0]
    counts = [s // IN_TN for s in IN_SEGS]
    starts = np.cumsum([0] + counts)
    bounds = tuple((int(starts[k]), int(starts[k + 1])) for k in range(len(counts)))
    n_main, n_tail = bounds[1][1], int(starts[-1]) - bounds[1][1]

    def omap(k):
        s, n = bounds[k][0], counts[k]
        return lambda i, j: (i, jnp.clip(j - s, 0, n - 1))

    return pl.pallas_call(
        functools.partial(_inproj_kernel, bounds=bounds, n_main=n_main),
        grid=(rows // tm, int(starts[-1])),
        in_specs=[
            pl.BlockSpec((tm, D_MODEL), lambda i, j: (i, 0)),
            pl.BlockSpec((1, D_MODEL), lambda i, j: (0, 0)),
            pl.BlockSpec((D_MODEL, IN_TN), lambda i, j: (0, jnp.minimum(j, n_main - 1))),
            pl.BlockSpec((D_MODEL, IN_TN), lambda i, j: (0, jnp.clip(j - n_main, 0, n_tail - 1))),
        ],
        out_specs=[pl.BlockSpec((tm, IN_TN), omap(k)) for k in range(len(counts))],
        out_shape=[jax.ShapeDtypeStruct((rows, s), F32) for s in IN_SEGS],
        scratch_shapes=[pltpu.VMEM((tm, D_MODEL), BF16)],
        compiler_params=_cparams(("arbitrary", "arbitrary")),
        name="inproj",
    )(x, norm_w, w_main, w_tail)


CONV_PAD = 8
CONV_SLAB = 256


def _ssd_kernel(xbc_ref, z_ref, sm_ref, cprev_ref, s0_ref, cw_ref, cb_ref, dtb_ref, alog_ref, dsk_ref,
                nw_ref, ep_ref, eq_ref, y_ref, sout_ref, cout_ref, st_ref, xpad_ref, xc_ref, exp_p_ref,
                exp_q_ref, yd_ref, *, q, nsub):
    c = pl.program_id(1)
    nc = pl.num_programs(1)
    rows = nsub * q
    hp = LANES // q
    tiles_per_group = (SSM_HEADS // SSM_GROUPS) // hp
    tile_w = hp * SSM_HEAD_DIM
    lo = CONV_PAD - (CONV_WIDTH - 1)

    @pl.when(c == 0)
    def _():
        st_ref[...] = s0_ref[0].T
        xpad_ref[0:lo, :] = jnp.zeros((lo, CONV_DIM), F32)
        xpad_ref[lo:CONV_PAD, :] = cprev_ref[0]

    assert CONV_WIDTH == 4
    xpad_ref[CONV_PAD:CONV_PAD + rows, :] = xbc_ref[0]
    for s in range(0, CONV_DIM, CONV_SLAB):
        xw = xpad_ref[:, s:s + CONV_SLAB]
        x1 = pltpu.roll(xw, 1, 0)
        w = [cw_ref[k:k + 1, s:s + CONV_SLAB] for k in range(CONV_WIDTH)]
        u = (xw * w[3] + x1 * w[2]) + pltpu.roll(xw * w[1] + x1 * w[0], 2, 0)
        xc_ref[:, s:s + CONV_SLAB] = _silu(u[CONV_PAD:, :] + cb_ref[:, s:s + CONV_SLAB])
    tail_rows = xpad_ref[CONV_PAD + rows - (CONV_WIDTH - 1):CONV_PAD + rows, :]
    xpad_ref[lo:CONV_PAD, :] = tail_rows

    ri = lax.broadcasted_iota(jnp.int32, (q, q), 0)
    ci = lax.broadcasted_iota(jnp.int32, (q, q), 1)
    tril = jnp.where(ci <= ri, 1.0, 0.0).astype(BF16)
    r128 = lax.broadcasted_iota(jnp.int32, (q, LANES), 0)
    c128 = lax.broadcasted_iota(jnp.int32, (q, LANES), 1)
    key128 = c128 & (q - 1)
    causal = key128 <= r128
    diag = key128 == r128
    br = lax.broadcasted_iota(jnp.int32, (LANES, tile_w), 0)
    bc = lax.broadcasted_iota(jnp.int32, (LANES, tile_w), 1)
    blockdiag = (br // q) == (bc // SSM_HEAD_DIM)
    neg_a = -jnp.exp(alog_ref[...])

    for u in range(nsub):
        r0 = u * q
        xdt = sm_ref[0, r0:r0 + q, 0:SSM_HEADS] + dtb_ref[...]
        dt = jnp.maximum(xdt, 0.0) + jnp.log1p(jnp.exp(-jnp.abs(xdt)))
        h3 = _split3(dt * neg_a)
        acs = _dot(tril, h3[0]) + _dot(tril, h3[1]) + _dot(tril, h3[2])

        def expand(v, e_ref):
            pieces = jnp.concatenate([piece.astype(F32) for piece in _split3(v)], axis=1).astype(BF16)
            return _dot(pieces, e_ref[...])

        exp_q_ref[u] = expand(jnp.concatenate([acs, dt], axis=0), eq_ref)
        exp_p_ref[u] = expand(jnp.concatenate([jnp.exp(acs), jnp.exp(acs[q - 1:q, :] - acs) * dt], axis=0), ep_ref)

        for g in range(SSM_GROUPS):
            gs = g * GROUP_LANES
            bs = SSM_INNER + g * SSM_STATE
            cs = SSM_INNER + (SSM_GROUPS + g) * SSM_STATE
            b_bf = xc_ref[r0:r0 + q, bs:bs + SSM_STATE].astype(BF16)
            c_bf = xc_ref[r0:r0 + q, cs:cs + SSM_STATE].astype(BF16)
            cb_t = _dot_nt(c_bf, jnp.concatenate([b_bf] * hp, axis=0))

            for tt in range(tiles_per_group):
                t = g * tiles_per_group + tt
                a_t = exp_q_ref[u, 0:q, t * LANES:(t + 1) * LANES]
                d_t = exp_q_ref[u, q:2 * q, t * LANES:(t + 1) * LANES]
                a_key = jnp.sum(jnp.where(diag, a_t, 0.0), axis=0, keepdims=True)
                d_key = jnp.sum(jnp.where(diag, d_t, 0.0), axis=0, keepdims=True)
                w_t = jnp.exp(jnp.where(causal, a_t - a_key, -jnp.inf)) * d_key * cb_t
                x_t = xc_ref[r0:r0 + q, t * tile_w:(t + 1) * tile_w]
                rhs = jnp.where(blockdiag, jnp.concatenate([x_t] * hp, axis=0), 0.0).astype(BF16)
                yd_ref[r0:r0 + q, t * tile_w:(t + 1) * tile_w] = _dot(w_t.astype(BF16), rhs)

            decay = exp_p_ref[u, 0:q, gs:gs + GROUP_LANES]
            tail = exp_p_ref[u, q:2 * q, gs:gs + GROUP_LANES]
            xs_g = xc_ref[r0:r0 + q, gs:gs + GROUP_LANES]
            st_g = st_ref[:, gs:gs + GROUP_LANES]
            y_off = _dot(c_bf, st_g.astype(BF16)) * decay
            xt = (xs_g * tail).astype(BF16)
            st_ref[:, gs:gs + GROUP_LANES] = st_g * decay[q - 1:q, :] + _dot_tn(b_bf, xt)

            y = yd_ref[r0:r0 + q, gs:gs + GROUP_LANES] + y_off + dsk_ref[:, gs:gs + GROUP_LANES] * xs_g
            y = y * _silu(z_ref[0, r0:r0 + q, gs:gs + GROUP_LANES])
            y = y * lax.rsqrt(jnp.mean(y * y, axis=-1, keepdims=True) + EPS)
            y_ref[0, r0:r0 + q, gs:gs + GROUP_LANES] = (y * nw_ref[:, gs:gs + GROUP_LANES]).astype(y_ref.dtype)

    @pl.when(c == nc - 1)
    def _():
        sout_ref[0] = st_ref[...].T
        cout_ref[0] = tail_rows


def _ssd(xbc, z, mla, conv_prev, s0, p, q, nsub):
    nb, lb = xbc.shape[0], xbc.shape[1]
    rows = q * nsub
    nc = lb // rows
    kq = SSM_HEADS * q
    in_specs = [
        pl.BlockSpec((1, rows, CONV_DIM), lambda b, c: (b, c, 0)),
        pl.BlockSpec((1, rows, SSM_INNER), lambda b, c: (b, c, 0)),
        pl.BlockSpec((1, rows, 512), lambda b, c: (b, c, (Q_LORA + KV_LORA) // 512)),
        pl.BlockSpec((1, CONV_WIDTH - 1, CONV_DIM), lambda b, c: (b, 0, 0)),
        pl.BlockSpec((1, SSM_INNER, SSM_STATE), lambda b, c: (b, 0, 0)),
        _const_spec((CONV_WIDTH, CONV_DIM)),
        _const_spec((1, CONV_DIM)),
        _const_spec((1, SSM_HEADS)),
        _const_spec((1, SSM_HEADS)),
        _const_spec((1, SSM_INNER)),
        _const_spec((1, SSM_INNER)),
        _const_spec((3 * SSM_HEADS, SSM_INNER)),
        _const_spec((3 * SSM_HEADS, kq)),
    ]
    out_specs = [
        pl.BlockSpec((1, rows, SSM_INNER), lambda b, c: (b, c, 0)),
        pl.BlockSpec((1, SSM_INNER, SSM_STATE), lambda b, c: (b, 0, 0)),
        pl.BlockSpec((1, CONV_WIDTH - 1, CONV_DIM), lambda b, c: (b, 0, 0)),
    ]
    out_shape = [
        jax.ShapeDtypeStruct((nb, lb, SSM_INNER), BF16),
        jax.ShapeDtypeStruct((nb, SSM_INNER, SSM_STATE), F32),
        jax.ShapeDtypeStruct((nb, CONV_WIDTH - 1, CONV_DIM), F32),
    ]
    scratch = [
        pltpu.VMEM((SSM_STATE, SSM_INNER), F32),
        pltpu.VMEM((CONV_PAD + rows, CONV_DIM), F32),
        pltpu.VMEM((rows, CONV_DIM), F32),
        pltpu.VMEM((nsub, 2 * q, SSM_INNER), F32),
        pltpu.VMEM((nsub, 2 * q, kq), F32),
        pltpu.VMEM((rows, SSM_INNER), F32),
    ]
    return pl.pallas_call(
        functools.partial(_ssd_kernel, q=q, nsub=nsub),
        grid=(nb, nc),
        in_specs=in_specs,
        out_specs=out_specs,
        out_shape=out_shape,
        scratch_shapes=scratch,
        compiler_params=_cparams(("arbitrary", "arbitrary")),
        name="ssd",
    )(xbc, z, mla, conv_prev, s0, p["conv_w"], p["conv_b"], p["dt_bias"], p["a_log"], p["d_skip_p"],
      p["ssm_norm_w"], p["e_p"], p["e_q%d" % q])


def _mla_latents(mla_ref, ct_ref, sn_ref, qnw_ref, kvnw_ref, ckv_ref, kr_ref):
    m = mla_ref[...]
    cqn = _rms(m[:, 0:Q_LORA], qnw_ref[...])
    ckv = _rms(m[:, Q_LORA:Q_LORA + KV_LORA], kvnw_ref[...])
    ckv_ref[...] = ckv
    small = m[:, Q_LORA + KV_LORA:]
    ct, sn = ct_ref[...], sn_ref[...]
    kr128 = small[:, 128:256] * ct + small[:, 256:384] * sn
    kr_ref[...] = kr128[:, 0:QK_ROPE]
    return cqn, ckv, kr128, ct, sn


def _mla_prompt_kernel(mla_ref, ct_ref, sn_ref, ctt_ref, snt_ref, qnw_ref, kvnw_ref, wqat_ref, wqbt_ref, wk_ref,
                       wvt_ref, ckv_ref, kr_ref, qt_ref, k_ref, vt_ref):
    cqn, ckv, kr128, _, _ = _mla_latents(mla_ref, ct_ref, sn_ref, qnw_ref, kvnw_ref, ckv_ref, kr_ref)
    cqn_t = cqn.T.astype(BF16)
    ckv_t = ckv.T.astype(BF16)
    ckv_bf = ckv.astype(BF16)
    kr_bf = kr128.astype(BF16)
    ctt, snt = ctt_ref[...], snt_ref[...]
    tm = ckv_bf.shape[0]
    ones_row = jnp.where(lax.broadcasted_iota(jnp.int32, (V_ROWS - V_DIM, tm), 0) == 0, 1.0, 0.0).astype(BF16)
    qs = SCALE * LOG2E
    for h in range(MLA_HEADS):
        qa = _dot(wqat_ref[h * QK_PAD:(h + 1) * QK_PAD, :], cqn_t)
        qb = _dot(wqbt_ref[h * LANES:(h + 1) * LANES, :], cqn_t)
        qt_ref[h, 0:QK_NOPE, :] = (qa[0:QK_NOPE] * qs).astype(BF16)
        qt_ref[h, QK_NOPE:QK_PAD, :] = ((qa[QK_NOPE:QK_PAD] * ctt + qb * snt) * qs).astype(BF16)
        k_ref[h, :, 0:QK_NOPE] = _dot(ckv_bf, wk_ref[:, h * QK_NOPE:(h + 1) * QK_NOPE]).astype(BF16)
        k_ref[h, :, QK_NOPE:QK_PAD] = kr_bf
        vt_ref[h, 0, 0:V_DIM, :] = _dot(wvt_ref[h * V_DIM:(h + 1) * V_DIM, :], ckv_t).astype(BF16)
        vt_ref[h, 0, V_DIM:V_ROWS, :] = ones_row


def _mla_prompt(mla, ct, sn, p, tm):
    rows = mla.shape[0]
    row = lambda w: pl.BlockSpec((tm, w), lambda i: (i, 0))
    col = pl.BlockSpec((LANES, tm), lambda i: (0, i))
    return pl.pallas_call(
        _mla_prompt_kernel,
        grid=(rows // tm,),
        in_specs=[row(SEG_MLA), row(LANES), row(LANES), col, col, _const_spec((1, Q_LORA)),
                  _const_spec((1, KV_LORA)),
                  _const_spec((MLA_HEADS * QK_PAD, Q_LORA)), _const_spec((MLA_HEADS * LANES, Q_LORA)),
                  _const_spec((KV_LORA, MLA_HEADS * QK_NOPE)), _const_spec((MLA_INNER, KV_LORA))],
        out_specs=[row(KV_LORA), row(QK_ROPE),
                   pl.BlockSpec((MLA_HEADS, QK_PAD, tm), lambda i: (0, 0, i)),
                   pl.BlockSpec((MLA_HEADS, tm, QK_PAD), lambda i: (0, i, 0)),
                   pl.BlockSpec((MLA_HEADS, 1, V_ROWS, tm), lambda i: (0, i, 0, 0))],
        out_shape=[jax.ShapeDtypeStruct((rows, KV_LORA), F32), jax.ShapeDtypeStruct((rows, QK_ROPE), F32),
                   jax.ShapeDtypeStruct((MLA_HEADS, QK_PAD, rows), BF16),
                   jax.ShapeDtypeStruct((MLA_HEADS, rows, QK_PAD), BF16),
                   jax.ShapeDtypeStruct((MLA_HEADS, rows // tm, V_ROWS, tm), BF16)],
        compiler_params=_cparams(("arbitrary",)),
        name="mla_prep_prompt",
    )(mla, ct, sn, ct.T, sn.T, p["q_norm_w"], p["kv_norm_w"], p["wq_a"].T, p["wq_b"].T, p["w_uk"], p["w_uv_t"])


def _mla_sample_kernel(mla_ref, ct_ref, sn_ref, qnw_ref, kvnw_ref, wqa_ref, wqb_ref, wk_ref,
                       ckv_ref, kr_ref, ql_ref, qr_ref):
    cqn, _, _, ct, sn = _mla_latents(mla_ref, ct_ref, sn_ref, qnw_ref, kvnw_ref, ckv_ref, kr_ref)
    cqn = cqn.astype(BF16)
    qa = _dot(cqn, wqa_ref[...])
    qb = _dot(cqn, wqb_ref[...])
    for h in range(MLA_HEADS):
        o = h * QK_PAD
        q_rope = qa[:, o + QK_NOPE:o + QK_PAD] * ct + qb[:, h * LANES:(h + 1) * LANES] * sn
        qr_ref[h] = (q_rope * SCALE).astype(BF16)
        q_nope = qa[:, o:o + QK_NOPE].astype(BF16)
        q_lat = _dot_nt(q_nope, wk_ref[:, h * QK_NOPE:(h + 1) * QK_NOPE])
        ql_ref[h] = (q_lat * SCALE).astype(BF16)


def _mla_sample(mla, ct, sn, p):
    rows = mla.shape[0]
    full = lambda w: pl.BlockSpec((rows, w), lambda i: (0, 0))
    return pl.pallas_call(
        _mla_sample_kernel,
        grid=(1,),
        in_specs=[full(SEG_MLA), full(LANES), full(LANES), _const_spec((1, Q_LORA)), _const_spec((1, KV_LORA)),
                  _const_spec((Q_LORA, MLA_HEADS * QK_PAD)), _const_spec((Q_LORA, MLA_HEADS * LANES)),
                  _const_spec((KV_LORA, MLA_HEADS * QK_NOPE))],
        out_specs=[full(KV_LORA), full(QK_ROPE),
                   pl.BlockSpec((MLA_HEADS, rows, KV_LORA), lambda i: (0, 0, 0)),
                   pl.BlockSpec((MLA_HEADS, rows, LANES), lambda i: (0, 0, 0))],
        out_shape=[jax.ShapeDtypeStruct((rows, KV_LORA), F32), jax.ShapeDtypeStruct((rows, QK_ROPE), F32),
                   jax.ShapeDtypeStruct((MLA_HEADS, rows, KV_LORA), BF16),
                   jax.ShapeDtypeStruct((MLA_HEADS, rows, LANES), BF16)],
        compiler_params=_cparams(("arbitrary",)),
        name="mla_prep_sample",
    )(mla, ct, sn, p["q_norm_w"], p["kv_norm_w"], p["wq_a"], p["wq_b"], p["w_uk"])


def _attn_kernel(qi_ref, kj_ref, qt_ref, k_ref, vt_ref, o_ref, m_ref, acc_ref, s0_ref, s1_ref, p0_ref,
                 p1_ref, a0_ref, a1_ref, *, tq, tks):
    s_refs, p_refs, a_refs = (s0_ref, s1_ref), (p0_ref, p1_ref), (a0_ref, a1_ref)
    n_sub = len(s_refs)
    t = pl.program_id(0)
    i = qi_ref[t]
    j = kj_ref[t]
    tk = n_sub * tks
    last_j = ((i + 1) * tq - 1) // tk

    @pl.when(j == 0)
    def _():
        m_ref[...] = jnp.full(m_ref.shape, -jnp.inf, F32)
        acc_ref[...] = jnp.zeros(acc_ref.shape, F32)

    def step(masked):
        if masked:
            q_chunk = (i * tq + lax.broadcasted_iota(jnp.int32, (tks, tq), 1)) // CHUNK
            k_chunk = (j * tk + lax.broadcasted_iota(jnp.int32, (tks, tq), 0)) // CHUNK
            visible = [k_chunk + (c * tks) // CHUNK <= q_chunk for c in range(n_sub)]

        def scores(h, c):
            s_refs[c][...] = _dot(k_ref[h, c * tks:(c + 1) * tks, :], qt_ref[h])

        def softmax(h, c):
            s = s_refs[c][...]
            if masked:
                s = jnp.where(visible[c], s, -jnp.inf)
            m_prev = m_ref[h]
            m_new = jnp.maximum(m_prev, jnp.max(s, axis=0, keepdims=True))
            m_ref[h] = m_new
            p_refs[c][...] = jnp.exp2(s - m_new).astype(BF16)
            a_refs[c][...] = jnp.exp2(m_prev - m_new)

        def values(h, c):
            acc_ref[h] = acc_ref[h] * a_refs[c][...] + _dot(vt_ref[h, c], p_refs[c][...])

        scores(0, 0)
        scores(0, 1)
        softmax(0, 0)

        for h in range(1, MLA_HEADS):
            scores(h, 0)
            softmax(h - 1, 1)
            values(h - 1, 0)
            scores(h, 1)
            softmax(h, 0)
            values(h - 1, 1)
        softmax(MLA_HEADS - 1, 1)
        values(MLA_HEADS - 1, 0)
        values(MLA_HEADS - 1, 1)

    @pl.when(j < last_j)
    def _():
        step(False)

    @pl.when(j == last_j)
    def _():
        step(True)
        for h in range(MLA_HEADS):
            den = acc_ref[h, V_DIM:V_DIM + 1, :]
            o_ref[:, h * V_DIM:(h + 1) * V_DIM] = (acc_ref[h, 0:V_DIM, :] * (1.0 / den)).T


def _attention_prompt(qt, kh, vt, tq):
    rows = kh.shape[1]
    n_sub, tks = 2, vt.shape[3]
    tk = tks * n_sub
    qi, kj = [], []
    for i in range(rows // tq):
        for j in range(((i + 1) * tq - 1) // tk + 1):
            qi.append(i)
            kj.append(j)
    qi = jnp.asarray(np.array(qi, np.int32))
    kj = jnp.asarray(np.array(kj, np.int32))
    grid_spec = pltpu.PrefetchScalarGridSpec(
        num_scalar_prefetch=2,
        grid=(int(qi.shape[0]),),
        in_specs=[
            pl.BlockSpec((MLA_HEADS, QK_PAD, tq), lambda t, qi, kj: (0, 0, qi[t])),
            pl.BlockSpec((MLA_HEADS, tk, QK_PAD), lambda t, qi, kj: (0, kj[t], 0)),
            pl.BlockSpec((MLA_HEADS, n_sub, V_ROWS, tks), lambda t, qi, kj: (0, kj[t], 0, 0)),
        ],
        out_specs=pl.BlockSpec((tq, MLA_INNER), lambda t, qi, kj: (qi[t], 0)),
        scratch_shapes=[
            pltpu.VMEM((MLA_HEADS, 1, tq), F32),
            pltpu.VMEM((MLA_HEADS, V_ROWS, tq), F32),
            pltpu.VMEM((tks, tq), F32), pltpu.VMEM((tks, tq), F32),
            pltpu.VMEM((tks, tq), BF16), pltpu.VMEM((tks, tq), BF16),
            pltpu.VMEM((1, tq), F32), pltpu.VMEM((1, tq), F32),
        ],
    )
    return pl.pallas_call(
        functools.partial(_attn_kernel, tq=tq, tks=tks),
        grid_spec=grid_spec,
        out_shape=jax.ShapeDtypeStruct((rows, MLA_INNER), F32),
        compiler_params=_cparams(("arbitrary",)),
        name="attn_prompt",
    )(qi, kj, qt, kh, vt)


def _attn_cached_kernel(ql_ref, qr_ref, ckv_c_ref, kr_c_ref, ckv_n_ref, kr_n_ref, wv_ref, o_ref, *, t_new):
    rows = MLA_HEADS * t_new
    ql = ql_ref[...].reshape(rows, KV_LORA)
    qr = qr_ref[...].reshape(rows, LANES)[:, 0:QK_ROPE]
    kc = ckv_c_ref[0].astype(BF16)
    kn = ckv_n_ref[0].astype(BF16)
    s_c = _dot_nt(ql, kc) + _dot_nt(qr, kr_c_ref[0].astype(BF16))
    s_n = _dot_nt(ql, kn) + _dot_nt(qr, kr_n_ref[0].astype(BF16))
    m = jnp.maximum(jnp.max(s_c, axis=1, keepdims=True), jnp.max(s_n, axis=1, keepdims=True))
    p_c = jnp.exp(s_c - m)
    p_n = jnp.exp(s_n - m)
    den = jnp.sum(p_c, axis=1, keepdims=True) + jnp.sum(p_n, axis=1, keepdims=True)
    o_lat = (_dot(p_c.astype(BF16), kc) + _dot(p_n.astype(BF16), kn)) / den
    o_bf = o_lat.astype(BF16)
    for h in range(MLA_HEADS):
        o_ref[0, :, h * V_DIM:(h + 1) * V_DIM] = _dot(o_bf[h * t_new:(h + 1) * t_new, :],
                                                        wv_ref[:, h * V_DIM:(h + 1) * V_DIM])


def _attention_cached(ql, qr, cache_kv, cache_kr, ckv_new, kr_new, w_uv):
    nb, past = cache_kv.shape[0], cache_kv.shape[1]
    t_new = ckv_new.shape[1]
    return pl.pallas_call(
        functools.partial(_attn_cached_kernel, t_new=t_new),
        grid=(nb,),
        in_specs=[
            pl.BlockSpec((MLA_HEADS, t_new, KV_LORA), lambda b: (0, b, 0)),
            pl.BlockSpec((MLA_HEADS, t_new, LANES), lambda b: (0, b, 0)),
            pl.BlockSpec((1, past, KV_LORA), lambda b: (b, 0, 0)),
            pl.BlockSpec((1, past, QK_ROPE), lambda b: (b, 0, 0)),
            pl.BlockSpec((1, t_new, KV_LORA), lambda b: (b, 0, 0)),
            pl.BlockSpec((1, t_new, QK_ROPE), lambda b: (b, 0, 0)),
            _const_spec((KV_LORA, MLA_INNER)),
        ],
        out_specs=pl.BlockSpec((1, t_new, MLA_INNER), lambda b: (b, 0, 0)),
        out_shape=jax.ShapeDtypeStruct((nb, t_new, MLA_INNER), F32),
        compiler_params=_cparams(("arbitrary",)),
        name="attn_cached",
    )(ql, qr, cache_kv, cache_kr, ckv_new, kr_new, w_uv)


def _merge_kernel(y_ref, attn_ref, gm_ref, gs_ref, ga_ref, wssm_ref, wmla_ref, o_ref):
    y_ssm = _dot(y_ref[...], wssm_ref[...])
    y_mla = _dot((attn_ref[...] * _silu(gm_ref[...])).astype(BF16), wmla_ref[...])
    o_ref[...] = (_sigmoid(gs_ref[...]) * y_ssm + _sigmoid(ga_ref[...]) * y_mla).astype(o_ref.dtype)


def _merge(y_norm, attn, gates, p, tm):
    rows = y_norm.shape[0]
    gate = lambda k: pl.BlockSpec((tm, D_MODEL), lambda i: (i, k))
    return pl.pallas_call(
        _merge_kernel,
        grid=(rows // tm,),
        in_specs=[pl.BlockSpec((tm, SSM_INNER), lambda i: (i, 0)), pl.BlockSpec((tm, MLA_INNER), lambda i: (i, 0)),
                  gate(0), gate(1), gate(2),
                  _const_spec((SSM_INNER, D_MODEL)), _const_spec((MLA_INNER, D_MODEL))],
        out_specs=pl.BlockSpec((tm, D_MODEL), lambda i: (i, 0)),
        out_shape=jax.ShapeDtypeStruct((rows, D_MODEL), BF16),
        compiler_params=_cparams(("arbitrary",)),
        name="merge",
    )(y_norm, attn, gates, gates, gates, p["w_ssm_out"], p["w_mla_out"])


def _outproj_kernel(x_ref, mixed_ref, w_ref, nw_ref, o_ref):
    x_out = x_ref[...] + _dot(mixed_ref[...], w_ref[...])
    o_ref[...] = _rms(x_out, nw_ref[...])


def _outproj(x, mixed, p, tm):
    rows = x.shape[0]
    row = pl.BlockSpec((tm, D_MODEL), lambda i: (i, 0))
    return pl.pallas_call(
        _outproj_kernel,
        grid=(rows // tm,),
        in_specs=[row, row, _const_spec((D_MODEL, D_MODEL)), _const_spec((1, D_MODEL))],
        out_specs=row,
        out_shape=jax.ShapeDtypeStruct((rows, D_MODEL), F32),
        compiler_params=_cparams(("arbitrary",)),
        name="outproj",
    )(x, mixed, p["w_out"], p["final_norm_w"])


def _prepare(norm_in_w, w_in, conv_w, conv_b, dt_bias, a_log, d_skip, ssm_norm_w, w_ssm_out, q_norm_w, w_q_up,
             kv_norm_w, w_kv_up, w_mla_out, w_out, final_norm_w, chunk_lens):
    offs = np.cumsum((0, SSM_INNER, CONV_DIM, SSM_HEADS, Q_LORA, KV_LORA, QK_ROPE, MLA_INNER, D_MODEL, D_MODEL))
    w_in_bf = w_in.astype(BF16)
    col = lambda k: w_in_bf[:, int(offs[k]):int(offs[k + 1])]
    half = QK_ROPE // 2
    kr_w = col(5)
    zeros = lambda n: jnp.zeros((D_MODEL, n), BF16)
    w_tail = jnp.concatenate([
        col(3), col(4),
        col(2), zeros(64), kr_w, zeros(64), kr_w[:, half:], kr_w[:, :half], zeros(64 + 128),
        col(6), col(7), col(8)], axis=1)

    wq = w_q_up.reshape(Q_LORA, MLA_HEADS, QK_NOPE + QK_ROPE)
    wq_rope = wq[..., QK_NOPE:]
    zq = jnp.zeros((Q_LORA, MLA_HEADS, 64), w_q_up.dtype)
    wq_a = jnp.concatenate([wq, zq], axis=-1).reshape(Q_LORA, MLA_HEADS * QK_PAD).astype(BF16)
    wq_b = jnp.concatenate([wq_rope[..., half:], wq_rope[..., :half], zq], axis=-1)
    wq_b = wq_b.reshape(Q_LORA, MLA_HEADS * LANES).astype(BF16)

    wkv = w_kv_up.reshape(KV_LORA, MLA_HEADS, QK_NOPE + V_DIM)
    w_uk = wkv[..., :QK_NOPE].reshape(KV_LORA, MLA_HEADS * QK_NOPE).astype(BF16)
    w_uv_h = wkv[..., QK_NOPE:]
    w_uv = w_uv_h.reshape(KV_LORA, MLA_INNER).astype(BF16)
    w_uv_t = w_uv.T

    head_of_lane = np.arange(SSM_INNER) // SSM_HEAD_DIM
    p = dict(
        norm_in_w=norm_in_w.reshape(1, D_MODEL), w_main=w_in_bf, w_tail=w_tail,
        conv_w=conv_w, conv_b=conv_b.reshape(1, CONV_DIM), dt_bias=dt_bias.reshape(1, SSM_HEADS),
        a_log=a_log.reshape(1, SSM_HEADS), d_skip_p=jnp.repeat(d_skip, SSM_HEAD_DIM).reshape(1, SSM_INNER),
        ssm_norm_w=ssm_norm_w.reshape(1, SSM_INNER), w_ssm_out=w_ssm_out.astype(BF16),
        q_norm_w=q_norm_w.reshape(1, Q_LORA), kv_norm_w=kv_norm_w.reshape(1, KV_LORA),
        wq_a=wq_a, wq_b=wq_b, w_uk=w_uk, w_uv=w_uv, w_uv_t=w_uv_t,
        w_mla_out=w_mla_out.astype(BF16), w_out=w_out.astype(BF16),
        final_norm_w=final_norm_w.reshape(1, D_MODEL),
        e_p=jnp.asarray(np.tile(head_of_lane[None, :] == np.arange(SSM_HEADS)[:, None], (3, 1)), BF16),
    )
    for q in chunk_lens:
        head_of_key_lane = np.arange(SSM_HEADS * q) // q
        p["e_q%d" % q] = jnp.asarray(np.tile(head_of_key_lane[None, :] == np.arange(SSM_HEADS)[:, None], (3, 1)),
                                     BF16)
    return p


def _rope_tables(pos):
    half = QK_ROPE // 2
    inv = 1.0 / (ROPE_THETA ** (jnp.arange(half, dtype=F32) * (2.0 / QK_ROPE)))
    ang = pos.astype(F32)[:, None] * inv[None, :]
    cos, sin = jnp.cos(ang), jnp.sin(ang)
    pad = jnp.zeros((pos.shape[0], LANES - QK_ROPE), F32)
    return jnp.concatenate([cos, cos, pad], axis=1), jnp.concatenate([-sin, sin, pad], axis=1)


def kernel(x_prompt, x_sample, cache_kv_latent, cache_k_rope, state_ssm, state_conv, norm_in_w, w_in, conv_w,
           conv_b, dt_bias, a_log, d_skip, ssm_norm_w, w_ssm_out, q_norm_w, w_q_up, kv_norm_w, w_kv_up, w_mla_out,
           w_out, final_norm_w):
    depth = w_in.shape[0]
    assert depth == 1 and x_prompt.shape[0] == 1
    seq = x_prompt.shape[1]
    nb, t_new = x_sample.shape[0], x_sample.shape[1]
    past = cache_kv_latent.shape[2]
    q_prompt, q_sample = min(CHUNK, seq), min(CHUNK, t_new)
    p = _prepare(norm_in_w[0], w_in[0], conv_w[0], conv_b[0], dt_bias[0], a_log[0], d_skip[0], ssm_norm_w[0],
                 w_ssm_out[0], q_norm_w[0], w_q_up[0], kv_norm_w[0], w_kv_up[0], w_mla_out[0], w_out[0],
                 final_norm_w, sorted({q_prompt, q_sample}))

    xp = x_prompt[0]
    z, xbc, mla, gates = _inproj(xp, p["norm_in_w"], p["w_main"], p["w_tail"], tm=min(1024, seq))
    y_norm, ssm_p, conv_p = _ssd(xbc[None], z[None], mla[None],
                                 jnp.zeros((1, CONV_WIDTH - 1, CONV_DIM), F32),
                                 jnp.zeros((1, SSM_INNER, SSM_STATE), F32), p, q_prompt,
                                 nsub=4 if seq % (4 * q_prompt) == 0 else 1)
    ct, sn = _rope_tables(jnp.arange(seq))
    ckv_p, kr_p, qt, kh, vt = _mla_prompt(mla, ct, sn, p, tm=min(512, seq // 2))
    attn = _attention_prompt(qt, kh, vt, tq=min(512, seq))
    mixed = _merge(y_norm[0], attn, gates, p, tm=min(256, seq))
    y_prompt = _outproj(xp, mixed, p, tm=min(256, seq))

    rows_s = nb * t_new
    xs = x_sample.reshape(rows_s, D_MODEL)
    z_s, xbc_s, mla_s, gates_s = _inproj(xs, p["norm_in_w"], p["w_main"], p["w_tail"], tm=rows_s)
    y_norm_s, ssm_s, conv_s = _ssd(xbc_s.reshape(nb, t_new, CONV_DIM), z_s.reshape(nb, t_new, SSM_INNER),
                                   mla_s.reshape(nb, t_new, SEG_MLA), state_conv[0],
                                   state_ssm[0].reshape(nb, SSM_INNER, SSM_STATE), p, q_sample, nsub=1)
    ct_s, sn_s = _rope_tables(past + jnp.arange(t_new))
    ckv_s, kr_s, ql, qr = _mla_sample(mla_s, jnp.tile(ct_s, (nb, 1)), jnp.tile(sn_s, (nb, 1)), p)
    ckv_s = ckv_s.reshape(nb, t_new, KV_LORA)
    kr_s = kr_s.reshape(nb, t_new, QK_ROPE)
    attn_s = _attention_cached(ql, qr, cache_kv_latent[0], cache_k_rope[0], ckv_s, kr_s, p["w_uv"])
    mixed_s = _merge(y_norm_s.reshape(rows_s, SSM_INNER), attn_s.reshape(rows_s, MLA_INNER), gates_s, p, tm=rows_s)
    y_sample = _outproj(xs, mixed_s, p, tm=rows_s).reshape(nb, t_new, D_MODEL)

    hshape = (SSM_HEADS, SSM_HEAD_DIM, SSM_STATE)
    return (y_prompt[None], y_sample,
            ckv_p[None, None], kr_p[None, None], ssm_p.reshape((1, 1) + hshape), conv_p[None],
            ckv_s[None], kr_s[None], ssm_s.reshape((1, nb) + hshape), conv_s[None])
```

```python
import functools
import math

import jax
import jax.numpy as jnp
import numpy as np
from jax import lax
from jax.experimental import pallas as pl
from jax.experimental.pallas import tpu as pltpu

F32 = jnp.float32
BF16 = jnp.bfloat16

D_MODEL = 2048
CHUNK = 64
SSM_INNER = 4096
SSM_HEAD_DIM = 64
SSM_HEADS = 64
SSM_GROUPS = 8
SSM_STATE = 128
CONV_WIDTH = 4
CONV_DIM = SSM_INNER + 2 * SSM_GROUPS * SSM_STATE
GROUP_LANES = SSM_INNER // SSM_GROUPS
MLA_HEADS = 16
Q_LORA = 512
KV_LORA = 512
QK_NOPE = 128
QK_ROPE = 64
V_DIM = 128
MLA_INNER = MLA_HEADS * V_DIM
ROPE_THETA = 10000.0
EPS = 1e-6
QK_PAD = 256
SCALE = (QK_NOPE + QK_ROPE) ** -0.5
LOG2E = math.log2(math.e)
V_ROWS = V_DIM + 16

LANES = 128
VMEM_LIMIT = 56 * 1024 * 1024

IN_TN = 512
SEG_Z = SSM_INNER
SEG_XBC = CONV_DIM
SEG_MLA = Q_LORA + KV_LORA + 512
SEG_G = MLA_INNER + 2 * D_MODEL
IN_SEGS = (SEG_Z, SEG_XBC, SEG_MLA, SEG_G)
IN_SEG_DTYPES = (BF16, BF16, F32, BF16)


def _cparams(sem, vmem=VMEM_LIMIT):
    return pltpu.CompilerParams(dimension_semantics=sem, vmem_limit_bytes=vmem)


def _const_spec(shape):
    nd = len(shape)
    return pl.BlockSpec(shape, lambda *_: (0,) * nd, pipeline_mode=pl.Buffered(1))


def _sigmoid(x):
    return 0.5 + 0.5 * jnp.tanh(0.5 * x)


def _silu(x):
    h = 0.5 * x
    return h + h * jnp.tanh(h)


def _rms(x, w):
    return x * lax.rsqrt(jnp.mean(x * x, axis=-1, keepdims=True) + EPS) * w


def _dot(a, b):
    return jnp.dot(a, b, preferred_element_type=F32)


def _dot_nt(a, b):
    return lax.dot_general(a, b, (((1,), (1,)), ((), ())), preferred_element_type=F32)


def _dot_tn(a, b):
    return lax.dot_general(a, b, (((0,), (0,)), ((), ())), preferred_element_type=F32)


def _split3(v):
    hi = v.astype(BF16)
    r1 = v - hi.astype(F32)
    mid = r1.astype(BF16)
    lo = (r1 - mid.astype(F32)).astype(BF16)
    return hi, mid, lo


def _inproj_kernel(x_ref, nw_ref, wm_ref, wt_ref, *rest, bounds, n_main):
    outs, h_ref = rest[:-1], rest[-1]
    j = pl.program_id(1)

    @pl.when(j == 0)
    def _():
        h_ref[...] = _rms(x_ref[...], nw_ref[...]).astype(BF16)

    for o_ref, (s, e) in zip(outs, bounds):
        w_ref = wm_ref if e <= n_main else wt_ref

        @pl.when((j >= s) & (j < e))
        def _(o_ref=o_ref, w_ref=w_ref):
            o_ref[...] = _dot(h_ref[...], w_ref[...]).astype(o_ref.dtype)


def _inproj(x, norm_w, w_main, w_tail, tm):
    rows = x.shape[0]
    counts = [s // IN_TN for s in IN_SEGS]
    starts = np.cumsum([0] + counts)
    bounds = tuple((int(starts[k]), int(starts[k + 1])) for k in range(len(counts)))
    n_main, n_tail = bounds[1][1], int(starts[-1]) - bounds[1][1]

    def omap(k):
        s, n = bounds[k][0], counts[k]
        return lambda i, j: (i, jnp.clip(j - s, 0, n - 1))

    return pl.pallas_call(
        functools.partial(_inproj_kernel, bounds=bounds, n_main=n_main),
        grid=(rows // tm, int(starts[-1])),
        in_specs=[
            pl.BlockSpec((tm, D_MODEL), lambda i, j: (i, 0)),
            pl.BlockSpec((1, D_MODEL), lambda i, j: (0, 0)),
            pl.BlockSpec((D_MODEL, IN_TN), lambda i, j: (0, jnp.minimum(j, n_main - 1))),
            pl.BlockSpec((D_MODEL, IN_TN), lambda i, j: (0, jnp.clip(j - n_main, 0, n_tail - 1))),
        ],
        out_specs=[pl.BlockSpec((tm, IN_TN), omap(k)) for k in range(len(counts))],
        out_shape=[jax.ShapeDtypeStruct((rows, s), dt) for s, dt in zip(IN_SEGS, IN_SEG_DTYPES)],
        scratch_shapes=[pltpu.VMEM((tm, D_MODEL), BF16)],
        compiler_params=_cparams(("arbitrary", "arbitrary")),
        name="inproj",
    )(x, norm_w, w_main, w_tail)


CONV_PAD = 8
CONV_SLAB = 256


def _ssd_kernel(xbc_ref, z_ref, sm_ref, cprev_ref, s0_ref, cw_ref, cb_ref, dtb_ref, alog_ref, dsk_ref,
                nw_ref, ep_ref, eq_ref, y_ref, sout_ref, cout_ref, st_ref, xpad_ref, xc_ref, exp_p_ref,
                exp_q_ref, yd_ref, *, q, nsub):
    c = pl.program_id(1)
    nc = pl.num_programs(1)
    rows = nsub * q
    hp = LANES // q
    tiles_per_group = (SSM_HEADS // SSM_GROUPS) // hp
    tile_w = hp * SSM_HEAD_DIM
    lo = CONV_PAD - (CONV_WIDTH - 1)

    @pl.when(c == 0)
    def _():
        st_ref[...] = s0_ref[0].T
        xpad_ref[0:lo, :] = jnp.zeros((lo, CONV_DIM), F32)
        xpad_ref[lo:CONV_PAD, :] = cprev_ref[0]

    assert CONV_WIDTH == 4
    xpad_ref[CONV_PAD:CONV_PAD + rows, :] = xbc_ref[0].astype(F32)
    for s in range(0, CONV_DIM, CONV_SLAB):
        xw = xpad_ref[:, s:s + CONV_SLAB]
        x1 = pltpu.roll(xw, 1, 0)
        w = [cw_ref[k:k + 1, s:s + CONV_SLAB] for k in range(CONV_WIDTH)]
        u = (xw * w[3] + x1 * w[2]) + pltpu.roll(xw * w[1] + x1 * w[0], 2, 0)
        xc_ref[:, s:s + CONV_SLAB] = _silu(u[CONV_PAD:, :] + cb_ref[:, s:s + CONV_SLAB])
    tail_rows = xpad_ref[CONV_PAD + rows - (CONV_WIDTH - 1):CONV_PAD + rows, :]
    xpad_ref[lo:CONV_PAD, :] = tail_rows

    ri = lax.broadcasted_iota(jnp.int32, (q, q), 0)
    ci = lax.broadcasted_iota(jnp.int32, (q, q), 1)
    tril = jnp.where(ci <= ri, 1.0, 0.0).astype(BF16)
    r128 = lax.broadcasted_iota(jnp.int32, (q, LANES), 0)
    c128 = lax.broadcasted_iota(jnp.int32, (q, LANES), 1)
    key128 = c128 & (q - 1)
    causal = key128 <= r128
    diag = key128 == r128
    br = lax.broadcasted_iota(jnp.int32, (LANES, tile_w), 0)
    bc = lax.broadcasted_iota(jnp.int32, (LANES, tile_w), 1)
    blockdiag = (br // q) == (bc // SSM_HEAD_DIM)
    neg_a = -jnp.exp(alog_ref[...])

    for u in range(nsub):
        r0 = u * q
        xdt = sm_ref[0, r0:r0 + q, 0:SSM_HEADS] + dtb_ref[...]
        dt = jnp.maximum(xdt, 0.0) + jnp.log1p(jnp.exp(-jnp.abs(xdt)))
        h3 = _split3(dt * neg_a)
        acs = _dot(tril, h3[0]) + _dot(tril, h3[1]) + _dot(tril, h3[2])

        def expand(v, e_ref):
            pieces = jnp.concatenate([piece.astype(F32) for piece in _split3(v)], axis=1).astype(BF16)
            return _dot(pieces, e_ref[...])

        exp_q_ref[u] = expand(jnp.concatenate([acs, dt], axis=0), eq_ref)
        exp_p_ref[u] = expand(jnp.concatenate([jnp.exp(acs), jnp.exp(acs[q - 1:q, :] - acs) * dt], axis=0), ep_ref)

        for g in range(SSM_GROUPS):
            gs = g * GROUP_LANES
            bs = SSM_INNER + g * SSM_STATE
            cs = SSM_INNER + (SSM_GROUPS + g) * SSM_STATE
            b_bf = xc_ref[r0:r0 + q, bs:bs + SSM_STATE].astype(BF16)
            c_bf = xc_ref[r0:r0 + q, cs:cs + SSM_STATE].astype(BF16)
            cb_t = _dot_nt(c_bf, jnp.concatenate([b_bf] * hp, axis=0))

            for tt in range(tiles_per_group):
                t = g * tiles_per_group + tt
                a_t = exp_q_ref[u, 0:q, t * LANES:(t + 1) * LANES]
                d_t = exp_q_ref[u, q:2 * q, t * LANES:(t + 1) * LANES]
                a_key = jnp.sum(jnp.where(diag, a_t, 0.0), axis=0, keepdims=True)
                d_key = jnp.sum(jnp.where(diag, d_t, 0.0), axis=0, keepdims=True)
                w_t = jnp.exp(jnp.where(causal, a_t - a_key, -jnp.inf)) * d_key * cb_t
                x_t = xc_ref[r0:r0 + q, t * tile_w:(t + 1) * tile_w]
                rhs = jnp.where(blockdiag, jnp.concatenate([x_t] * hp, axis=0), 0.0).astype(BF16)
                yd_ref[r0:r0 + q, t * tile_w:(t + 1) * tile_w] = _dot(w_t.astype(BF16), rhs)

            decay = exp_p_ref[u, 0:q, gs:gs + GROUP_LANES]
            tail = exp_p_ref[u, q:2 * q, gs:gs + GROUP_LANES]
            xs_g = xc_ref[r0:r0 + q, gs:gs + GROUP_LANES]
            st_g = st_ref[:, gs:gs + GROUP_LANES]
            y_off = _dot(c_bf, st_g.astype(BF16)) * decay
            xt = (xs_g * tail).astype(BF16)
            st_ref[:, gs:gs + GROUP_LANES] = st_g * decay[q - 1:q, :] + _dot_tn(b_bf, xt)

            y = yd_ref[r0:r0 + q, gs:gs + GROUP_LANES] + y_off + dsk_ref[:, gs:gs + GROUP_LANES] * xs_g
            y = y * _silu(z_ref[0, r0:r0 + q, gs:gs + GROUP_LANES].astype(F32))
            y = y * lax.rsqrt(jnp.mean(y * y, axis=-1, keepdims=True) + EPS)
            y_ref[0, r0:r0 + q, gs:gs + GROUP_LANES] = (y * nw_ref[:, gs:gs + GROUP_LANES]).astype(y_ref.dtype)

    @pl.when(c == nc - 1)
    def _():
        sout_ref[0] = st_ref[...].T
        cout_ref[0] = tail_rows


def _ssd(xbc, z, mla, conv_prev, s0, p, q, nsub):
    nb, lb = xbc.shape[0], xbc.shape[1]
    rows = q * nsub
    nc = lb // rows
    kq = SSM_HEADS * q
    in_specs = [
        pl.BlockSpec((1, rows, CONV_DIM), lambda b, c: (b, c, 0)),
        pl.BlockSpec((1, rows, SSM_INNER), lambda b, c: (b, c, 0)),
        pl.BlockSpec((1, rows, 512), lambda b, c: (b, c, (Q_LORA + KV_LORA) // 512)),
        pl.BlockSpec((1, CONV_WIDTH - 1, CONV_DIM), lambda b, c: (b, 0, 0)),
        pl.BlockSpec((1, SSM_INNER, SSM_STATE), lambda b, c: (b, 0, 0)),
        _const_spec((CONV_WIDTH, CONV_DIM)),
        _const_spec((1, CONV_DIM)),
        _const_spec((1, SSM_HEADS)),
        _const_spec((1, SSM_HEADS)),
        _const_spec((1, SSM_INNER)),
        _const_spec((1, SSM_INNER)),
        _const_spec((3 * SSM_HEADS, SSM_INNER)),
        _const_spec((3 * SSM_HEADS, kq)),
    ]
    out_specs = [
        pl.BlockSpec((1, rows, SSM_INNER), lambda b, c: (b, c, 0)),
        pl.BlockSpec((1, SSM_INNER, SSM_STATE), lambda b, c: (b, 0, 0)),
        pl.BlockSpec((1, CONV_WIDTH - 1, CONV_DIM), lambda b, c: (b, 0, 0)),
    ]
    out_shape = [
        jax.ShapeDtypeStruct((nb, lb, SSM_INNER), BF16),
        jax.ShapeDtypeStruct((nb, SSM_INNER, SSM_STATE), F32),
        jax.ShapeDtypeStruct((nb, CONV_WIDTH - 1, CONV_DIM), F32),
    ]
    scratch = [
        pltpu.VMEM((SSM_STATE, SSM_INNER), F32),
        pltpu.VMEM((CONV_PAD + rows, CONV_DIM), F32),
        pltpu.VMEM((rows, CONV_DIM), F32),
        pltpu.VMEM((nsub, 2 * q, SSM_INNER), F32),
        pltpu.VMEM((nsub, 2 * q, kq), F32),
        pltpu.VMEM((rows, SSM_INNER), F32),
    ]
    return pl.pallas_call(
        functools.partial(_ssd_kernel, q=q, nsub=nsub),
        grid=(nb, nc),
        in_specs=in_specs,
        out_specs=out_specs,
        out_shape=out_shape,
        scratch_shapes=scratch,
        compiler_params=_cparams(("arbitrary", "arbitrary")),
        name="ssd",
    )(xbc, z, mla, conv_prev, s0, p["conv_w"], p["conv_b"], p["dt_bias"], p["a_log"], p["d_skip_p"],
      p["ssm_norm_w"], p["e_p"], p["e_q%d" % q])


def _mla_latents(mla_ref, ct_ref, sn_ref, qnw_ref, kvnw_ref, ckv_ref, kr_ref):
    m = mla_ref[...]
    cqn = _rms(m[:, 0:Q_LORA], qnw_ref[...])
    ckv = _rms(m[:, Q_LORA:Q_LORA + KV_LORA], kvnw_ref[...])
    ckv_ref[...] = ckv
    small = m[:, Q_LORA + KV_LORA:]
    ct, sn = ct_ref[...], sn_ref[...]
    kr128 = small[:, 128:256] * ct + small[:, 256:384] * sn
    kr_ref[...] = kr128[:, 0:QK_ROPE]
    return cqn, ckv, kr128, ct, sn


def _mla_prompt_kernel(mla_ref, ct_ref, sn_ref, ctt_ref, snt_ref, qnw_ref, kvnw_ref, wqat_ref, wqbt_ref, wk_ref,
                       wvt_ref, ckv_ref, kr_ref, qt_ref, kn_ref, krb_ref, vt_ref):
    cqn, ckv, kr128, _, _ = _mla_latents(mla_ref, ct_ref, sn_ref, qnw_ref, kvnw_ref, ckv_ref, kr_ref)
    cqn_t = cqn.T.astype(BF16)
    ckv_t = ckv.T.astype(BF16)
    ckv_bf = ckv.astype(BF16)
    krb_ref[...] = kr128.astype(BF16)
    ctt, snt = ctt_ref[...], snt_ref[...]
    tm = ckv_bf.shape[0]
    ones_row = jnp.where(lax.broadcasted_iota(jnp.int32, (V_ROWS - V_DIM, tm), 0) == 0, 1.0, 0.0).astype(BF16)
    qs = SCALE * LOG2E
    for h in range(MLA_HEADS):
        qa = _dot(wqat_ref[h * QK_PAD:(h + 1) * QK_PAD, :], cqn_t)
        qb = _dot(wqbt_ref[h * LANES:(h + 1) * LANES, :], cqn_t)
        qt_ref[h, 0:QK_NOPE, :] = (qa[0:QK_NOPE] * qs).astype(BF16)
        qt_ref[h, QK_NOPE:QK_PAD, :] = ((qa[QK_NOPE:QK_PAD] * ctt + qb * snt) * qs).astype(BF16)
        kn_ref[h] = _dot(ckv_bf, wk_ref[:, h * QK_NOPE:(h + 1) * QK_NOPE]).astype(BF16)
        vt_ref[h, 0, 0:V_DIM, :] = _dot(wvt_ref[h * V_DIM:(h + 1) * V_DIM, :], ckv_t).astype(BF16)
        vt_ref[h, 0, V_DIM:V_ROWS, :] = ones_row


def _mla_prompt(mla, ct, sn, p, tm):
    rows = mla.shape[0]
    row = lambda w: pl.BlockSpec((tm, w), lambda i: (i, 0))
    col = pl.BlockSpec((LANES, tm), lambda i: (0, i))
    return pl.pallas_call(
        _mla_prompt_kernel,
        grid=(rows // tm,),
        in_specs=[row(SEG_MLA), row(LANES), row(LANES), col, col, _const_spec((1, Q_LORA)),
                  _const_spec((1, KV_LORA)),
                  _const_spec((MLA_HEADS * QK_PAD, Q_LORA)), _const_spec((MLA_HEADS * LANES, Q_LORA)),
                  _const_spec((KV_LORA, MLA_HEADS * QK_NOPE)), _const_spec((MLA_INNER, KV_LORA))],
        out_specs=[row(KV_LORA), row(QK_ROPE),
                   pl.BlockSpec((MLA_HEADS, QK_PAD, tm), lambda i: (0, 0, i)),
                   pl.BlockSpec((MLA_HEADS, tm, QK_NOPE), lambda i: (0, i, 0)), row(LANES),
                   pl.BlockSpec((MLA_HEADS, 1, V_ROWS, tm), lambda i: (0, i, 0, 0))],
        out_shape=[jax.ShapeDtypeStruct((rows, KV_LORA), F32), jax.ShapeDtypeStruct((rows, QK_ROPE), F32),
                   jax.ShapeDtypeStruct((MLA_HEADS, QK_PAD, rows), BF16),
                   jax.ShapeDtypeStruct((MLA_HEADS, rows, QK_NOPE), BF16),
                   jax.ShapeDtypeStruct((rows, LANES), BF16),
                   jax.ShapeDtypeStruct((MLA_HEADS, rows // tm, V_ROWS, tm), BF16)],
        compiler_params=_cparams(("arbitrary",)),
        name="mla_prep_prompt",
    )(mla, ct, sn, ct.T, sn.T, p["q_norm_w"], p["kv_norm_w"], p["wq_a"].T, p["wq_b"].T, p["w_uk"], p["w_uv_t"])


def _mla_sample_kernel(mla_ref, ct_ref, sn_ref, qnw_ref, kvnw_ref, wqa_ref, wqb_ref, wk_ref,
                       ckv_ref, kr_ref, ql_ref, qr_ref):
    cqn, _, _, ct, sn = _mla_latents(mla_ref, ct_ref, sn_ref, qnw_ref, kvnw_ref, ckv_ref, kr_ref)
    cqn = cqn.astype(BF16)
    qa = _dot(cqn, wqa_ref[...])
    qb = _dot(cqn, wqb_ref[...])
    for h in range(MLA_HEADS):
        o = h * QK_PAD
        q_rope = qa[:, o + QK_NOPE:o + QK_PAD] * ct + qb[:, h * LANES:(h + 1) * LANES] * sn
        qr_ref[h] = (q_rope * SCALE).astype(BF16)
        q_nope = qa[:, o:o + QK_NOPE].astype(BF16)
        q_lat = _dot_nt(q_nope, wk_ref[:, h * QK_NOPE:(h + 1) * QK_NOPE])
        ql_ref[h] = (q_lat * SCALE).astype(BF16)


def _mla_sample(mla, ct, sn, p):
    rows = mla.shape[0]
    full = lambda w: pl.BlockSpec((rows, w), lambda i: (0, 0))
    return pl.pallas_call(
        _mla_sample_kernel,
        grid=(1,),
        in_specs=[full(SEG_MLA), full(LANES), full(LANES), _const_spec((1, Q_LORA)), _const_spec((1, KV_LORA)),
                  _const_spec((Q_LORA, MLA_HEADS * QK_PAD)), _const_spec((Q_LORA, MLA_HEADS * LANES)),
                  _const_spec((KV_LORA, MLA_HEADS * QK_NOPE))],
        out_specs=[full(KV_LORA), full(QK_ROPE),
                   pl.BlockSpec((MLA_HEADS, rows, KV_LORA), lambda i: (0, 0, 0)),
                   pl.BlockSpec((MLA_HEADS, rows, LANES), lambda i: (0, 0, 0))],
        out_shape=[jax.ShapeDtypeStruct((rows, KV_LORA), F32), jax.ShapeDtypeStruct((rows, QK_ROPE), F32),
                   jax.ShapeDtypeStruct((MLA_HEADS, rows, KV_LORA), BF16),
                   jax.ShapeDtypeStruct((MLA_HEADS, rows, LANES), BF16)],
        compiler_params=_cparams(("arbitrary",)),
        name="mla_prep_sample",
    )(mla, ct, sn, p["q_norm_w"], p["kv_norm_w"], p["wq_a"], p["wq_b"], p["w_uk"])


def _attn_kernel(qi_ref, kj_ref, qt_ref, kn_ref, kr_ref, vt_ref, o_ref, m_ref, acc_ref, s0_ref, s1_ref, p0_ref,
                 p1_ref, a0_ref, a1_ref, *, tq, tks):
    s_refs, p_refs, a_refs = (s0_ref, s1_ref), (p0_ref, p1_ref), (a0_ref, a1_ref)
    n_sub = len(s_refs)
    t = pl.program_id(0)
    i = qi_ref[t]
    j = kj_ref[t]
    tk = n_sub * tks
    last_j = ((i + 1) * tq - 1) // tk

    @pl.when(j == 0)
    def _():
        m_ref[...] = jnp.full(m_ref.shape, -jnp.inf, F32)
        acc_ref[...] = jnp.zeros(acc_ref.shape, F32)

    def step(masked):
        if masked:
            q_chunk = (i * tq + lax.broadcasted_iota(jnp.int32, (tks, tq), 1)) // CHUNK
            k_chunk = (j * tk + lax.broadcasted_iota(jnp.int32, (tks, tq), 0)) // CHUNK
            visible = [k_chunk + (c * tks) // CHUNK <= q_chunk for c in range(n_sub)]

        def scores(h, c):
            k_h = jnp.concatenate([kn_ref[h, c * tks:(c + 1) * tks, :], kr_ref[c * tks:(c + 1) * tks, :]], axis=1)
            s_refs[c][...] = _dot(k_h, qt_ref[h])

        def softmax(h, c):
            s = s_refs[c][...]
            if masked:
                s = jnp.where(visible[c], s, -jnp.inf)
            m_prev = m_ref[h]
            m_new = jnp.maximum(m_prev, jnp.max(s, axis=0, keepdims=True))
            m_ref[h] = m_new
            p_refs[c][...] = jnp.exp2(s - m_new).astype(BF16)
            a_refs[c][...] = jnp.exp2(m_prev - m_new)

        def values(h, c):
            acc_ref[h] = acc_ref[h] * a_refs[c][...] + _dot(vt_ref[h, c], p_refs[c][...])

        scores(0, 0)
        scores(0, 1)
        softmax(0, 0)

        for h in range(1, MLA_HEADS):
            scores(h, 0)
            softmax(h - 1, 1)
            values(h - 1, 0)
            scores(h, 1)
            softmax(h, 0)
            values(h - 1, 1)
        softmax(MLA_HEADS - 1, 1)
        values(MLA_HEADS - 1, 0)
        values(MLA_HEADS - 1, 1)

    @pl.when(j < last_j)
    def _():
        step(False)

    @pl.when(j == last_j)
    def _():
        step(True)
        for h in range(MLA_HEADS):
            den = acc_ref[h, V_DIM:V_DIM + 1, :]
            o_ref[:, h * V_DIM:(h + 1) * V_DIM] = (acc_ref[h, 0:V_DIM, :] * (1.0 / den)).T


def _attention_prompt(qt, kn, kr, vt, tq):
    rows = kn.shape[1]
    n_sub, tks = 2, vt.shape[3]
    tk = tks * n_sub
    qi, kj = [], []
    for i in range(rows // tq):
        for j in range(((i + 1) * tq - 1) // tk + 1):
            qi.append(i)
            kj.append(j)
    qi = jnp.asarray(np.array(qi, np.int32))
    kj = jnp.asarray(np.array(kj, np.int32))
    grid_spec = pltpu.PrefetchScalarGridSpec(
        num_scalar_prefetch=2,
        grid=(int(qi.shape[0]),),
        in_specs=[
            pl.BlockSpec((MLA_HEADS, QK_PAD, tq), lambda t, qi, kj: (0, 0, qi[t])),
            pl.BlockSpec((MLA_HEADS, tk, QK_NOPE), lambda t, qi, kj: (0, kj[t], 0)),
            pl.BlockSpec((tk, LANES), lambda t, qi, kj: (kj[t], 0)),
            pl.BlockSpec((MLA_HEADS, n_sub, V_ROWS, tks), lambda t, qi, kj: (0, kj[t], 0, 0)),
        ],
        out_specs=pl.BlockSpec((tq, MLA_INNER), lambda t, qi, kj: (qi[t], 0)),
        scratch_shapes=[
            pltpu.VMEM((MLA_HEADS, 1, tq), F32),
            pltpu.VMEM((MLA_HEADS, V_ROWS, tq), F32),
            pltpu.VMEM((tks, tq), F32), pltpu.VMEM((tks, tq), F32),
            pltpu.VMEM((tks, tq), BF16), pltpu.VMEM((tks, tq), BF16),
            pltpu.VMEM((1, tq), F32), pltpu.VMEM((1, tq), F32),
        ],
    )
    return pl.pallas_call(
        functools.partial(_attn_kernel, tq=tq, tks=tks),
        grid_spec=grid_spec,
        out_shape=jax.ShapeDtypeStruct((rows, MLA_INNER), F32),
        compiler_params=_cparams(("arbitrary",)),
        name="attn_prompt",
    )(qi, kj, qt, kn, kr, vt)


def _attn_cached_kernel(ql_ref, qr_ref, ckv_c_ref, kr_c_ref, ckv_n_ref, kr_n_ref, wv_ref, o_ref, *, t_new):
    rows = MLA_HEADS * t_new
    ql = ql_ref[...].reshape(rows, KV_LORA)
    qr = qr_ref[...].reshape(rows, LANES)[:, 0:QK_ROPE]
    kc = ckv_c_ref[0].astype(BF16)
    kn = ckv_n_ref[0].astype(BF16)
    s_c = _dot_nt(ql, kc) + _dot_nt(qr, kr_c_ref[0].astype(BF16))
    s_n = _dot_nt(ql, kn) + _dot_nt(qr, kr_n_ref[0].astype(BF16))
    m = jnp.maximum(jnp.max(s_c, axis=1, keepdims=True), jnp.max(s_n, axis=1, keepdims=True))
    p_c = jnp.exp(s_c - m)
    p_n = jnp.exp(s_n - m)
    den = jnp.sum(p_c, axis=1, keepdims=True) + jnp.sum(p_n, axis=1, keepdims=True)
    o_lat = (_dot(p_c.astype(BF16), kc) + _dot(p_n.astype(BF16), kn)) / den
    o_bf = o_lat.astype(BF16)
    for h in range(MLA_HEADS):
        o_ref[0, :, h * V_DIM:(h + 1) * V_DIM] = _dot(o_bf[h * t_new:(h + 1) * t_new, :],
                                                        wv_ref[:, h * V_DIM:(h + 1) * V_DIM])


def _attention_cached(ql, qr, cache_kv, cache_kr, ckv_new, kr_new, w_uv):
    nb, past = cache_kv.shape[0], cache_kv.shape[1]
    t_new = ckv_new.shape[1]
    return pl.pallas_call(
        functools.partial(_attn_cached_kernel, t_new=t_new),
        grid=(nb,),
        in_specs=[
            pl.BlockSpec((MLA_HEADS, t_new, KV_LORA), lambda b: (0, b, 0)),
            pl.BlockSpec((MLA_HEADS, t_new, LANES), lambda b: (0, b, 0)),
            pl.BlockSpec((1, past, KV_LORA), lambda b: (b, 0, 0)),
            pl.BlockSpec((1, past, QK_ROPE), lambda b: (b, 0, 0)),
            pl.BlockSpec((1, t_new, KV_LORA), lambda b: (b, 0, 0)),
            pl.BlockSpec((1, t_new, QK_ROPE), lambda b: (b, 0, 0)),
            _const_spec((KV_LORA, MLA_INNER)),
        ],
        out_specs=pl.BlockSpec((1, t_new, MLA_INNER), lambda b: (b, 0, 0)),
        out_shape=jax.ShapeDtypeStruct((nb, t_new, MLA_INNER), F32),
        compiler_params=_cparams(("arbitrary",)),
        name="attn_cached",
    )(ql, qr, cache_kv, cache_kr, ckv_new, kr_new, w_uv)


def _merge_kernel(y_ref, attn_ref, gm_ref, gs_ref, ga_ref, wssm_ref, wmla_ref, o_ref):
    y_ssm = _dot(y_ref[...], wssm_ref[...])
    y_mla = _dot((attn_ref[...] * _silu(gm_ref[...].astype(F32))).astype(BF16), wmla_ref[...])
    o_ref[...] = (_sigmoid(gs_ref[...].astype(F32)) * y_ssm
                  + _sigmoid(ga_ref[...].astype(F32)) * y_mla).astype(o_ref.dtype)


def _merge(y_norm, attn, gates, p, tm):
    rows = y_norm.shape[0]
    gate = lambda k: pl.BlockSpec((tm, D_MODEL), lambda i: (i, k))
    return pl.pallas_call(
        _merge_kernel,
        grid=(rows // tm,),
        in_specs=[pl.BlockSpec((tm, SSM_INNER), lambda i: (i, 0)), pl.BlockSpec((tm, MLA_INNER), lambda i: (i, 0)),
                  gate(0), gate(1), gate(2),
                  _const_spec((SSM_INNER, D_MODEL)), _const_spec((MLA_INNER, D_MODEL))],
        out_specs=pl.BlockSpec((tm, D_MODEL), lambda i: (i, 0)),
        out_shape=jax.ShapeDtypeStruct((rows, D_MODEL), BF16),
        compiler_params=_cparams(("arbitrary",)),
        name="merge",
    )(y_norm, attn, gates, gates, gates, p["w_ssm_out"], p["w_mla_out"])


def _outproj_kernel(x_ref, mixed_ref, w_ref, nw_ref, o_ref):
    x_out = x_ref[...] + _dot(mixed_ref[...], w_ref[...])
    o_ref[...] = _rms(x_out, nw_ref[...])


def _outproj(x, mixed, p, tm):
    rows = x.shape[0]
    row = pl.BlockSpec((tm, D_MODEL), lambda i: (i, 0))
    return pl.pallas_call(
        _outproj_kernel,
        grid=(rows // tm,),
        in_specs=[row, row, _const_spec((D_MODEL, D_MODEL)), _const_spec((1, D_MODEL))],
        out_specs=row,
        out_shape=jax.ShapeDtypeStruct((rows, D_MODEL), F32),
        compiler_params=_cparams(("arbitrary",)),
        name="outproj",
    )(x, mixed, p["w_out"], p["final_norm_w"])


def _prepare(norm_in_w, w_in, conv_w, conv_b, dt_bias, a_log, d_skip, ssm_norm_w, w_ssm_out, q_norm_w, w_q_up,
             kv_norm_w, w_kv_up, w_mla_out, w_out, final_norm_w, chunk_lens):
    offs = np.cumsum((0, SSM_INNER, CONV_DIM, SSM_HEADS, Q_LORA, KV_LORA, QK_ROPE, MLA_INNER, D_MODEL, D_MODEL))
    w_in_bf = w_in.astype(BF16)
    col = lambda k: w_in_bf[:, int(offs[k]):int(offs[k + 1])]
    half = QK_ROPE // 2
    kr_w = col(5)
    zeros = lambda n: jnp.zeros((D_MODEL, n), BF16)
    w_tail = jnp.concatenate([
        col(3), col(4),
        col(2), zeros(64), kr_w, zeros(64), kr_w[:, half:], kr_w[:, :half], zeros(64 + 128),
        col(6), col(7), col(8)], axis=1)

    wq = w_q_up.reshape(Q_LORA, MLA_HEADS, QK_NOPE + QK_ROPE)
    wq_rope = wq[..., QK_NOPE:]
    zq = jnp.zeros((Q_LORA, MLA_HEADS, 64), w_q_up.dtype)
    wq_a = jnp.concatenate([wq, zq], axis=-1).reshape(Q_LORA, MLA_HEADS * QK_PAD).astype(BF16)
    wq_b = jnp.concatenate([wq_rope[..., half:], wq_rope[..., :half], zq], axis=-1)
    wq_b = wq_b.reshape(Q_LORA, MLA_HEADS * LANES).astype(BF16)

    wkv = w_kv_up.reshape(KV_LORA, MLA_HEADS, QK_NOPE + V_DIM)
    w_uk = wkv[..., :QK_NOPE].reshape(KV_LORA, MLA_HEADS * QK_NOPE).astype(BF16)
    w_uv_h = wkv[..., QK_NOPE:]
    w_uv = w_uv_h.reshape(KV_LORA, MLA_INNER).astype(BF16)
    w_uv_t = w_uv.T

    head_of_lane = np.arange(SSM_INNER) // SSM_HEAD_DIM
    p = dict(
        norm_in_w=norm_in_w.reshape(1, D_MODEL), w_main=w_in_bf, w_tail=w_tail,
        conv_w=conv_w, conv_b=conv_b.reshape(1, CONV_DIM), dt_bias=dt_bias.reshape(1, SSM_HEADS),
        a_log=a_log.reshape(1, SSM_HEADS), d_skip_p=jnp.repeat(d_skip, SSM_HEAD_DIM).reshape(1, SSM_INNER),
        ssm_norm_w=ssm_norm_w.reshape(1, SSM_INNER), w_ssm_out=w_ssm_out.astype(BF16),
        q_norm_w=q_norm_w.reshape(1, Q_LORA), kv_norm_w=kv_norm_w.reshape(1, KV_LORA),
        wq_a=wq_a, wq_b=wq_b, w_uk=w_uk, w_uv=w_uv, w_uv_t=w_uv_t,
        w_mla_out=w_mla_out.astype(BF16), w_out=w_out.astype(BF16),
        final_norm_w=final_norm_w.reshape(1, D_MODEL),
        e_p=jnp.asarray(np.tile(head_of_lane[None, :] == np.arange(SSM_HEADS)[:, None], (3, 1)), BF16),
    )
    for q in chunk_lens:
        head_of_key_lane = np.arange(SSM_HEADS * q) // q
        p["e_q%d" % q] = jnp.asarray(np.tile(head_of_key_lane[None, :] == np.arange(SSM_HEADS)[:, None], (3, 1)),
                                     BF16)
    return p


def _rope_tables(pos):
    half = QK_ROPE // 2
    inv = 1.0 / (ROPE_THETA ** (jnp.arange(half, dtype=F32) * (2.0 / QK_ROPE)))
    ang = pos.astype(F32)[:, None] * inv[None, :]
    cos, sin = jnp.cos(ang), jnp.sin(ang)
    pad = jnp.zeros((pos.shape[0], LANES - QK_ROPE), F32)
    return jnp.concatenate([cos, cos, pad], axis=1), jnp.concatenate([-sin, sin, pad], axis=1)


def kernel(x_prompt, x_sample, cache_kv_latent, cache_k_rope, state_ssm, state_conv, norm_in_w, w_in, conv_w,
           conv_b, dt_bias, a_log, d_skip, ssm_norm_w, w_ssm_out, q_norm_w, w_q_up, kv_norm_w, w_kv_up, w_mla_out,
           w_out, final_norm_w):
    depth = w_in.shape[0]
    assert depth == 1 and x_prompt.shape[0] == 1
    seq = x_prompt.shape[1]
    nb, t_new = x_sample.shape[0], x_sample.shape[1]
    past = cache_kv_latent.shape[2]
    q_prompt, q_sample = min(CHUNK, seq), min(CHUNK, t_new)
    p = _prepare(norm_in_w[0], w_in[0], conv_w[0], conv_b[0], dt_bias[0], a_log[0], d_skip[0], ssm_norm_w[0],
                 w_ssm_out[0], q_norm_w[0], w_q_up[0], kv_norm_w[0], w_kv_up[0], w_mla_out[0], w_out[0],
                 final_norm_w, sorted({q_prompt, q_sample}))

    xp = x_prompt[0]
    z, xbc, mla, gates = _inproj(xp, p["norm_in_w"], p["w_main"], p["w_tail"], tm=min(1024, seq))
    y_norm, ssm_p, conv_p = _ssd(xbc[None], z[None], mla[None],
                                 jnp.zeros((1, CONV_WIDTH - 1, CONV_DIM), F32),
                                 jnp.zeros((1, SSM_INNER, SSM_STATE), F32), p, q_prompt,
                                 nsub=4 if seq % (4 * q_prompt) == 0 else 1)
    ct, sn = _rope_tables(jnp.arange(seq))
    ckv_p, kr_p, qt, kn, krb, vt = _mla_prompt(mla, ct, sn, p, tm=min(512, seq // 2))
    attn = _attention_prompt(qt, kn, krb, vt, tq=min(512, seq))
    mixed = _merge(y_norm[0], attn, gates, p, tm=min(256, seq))
    y_prompt = _outproj(xp, mixed, p, tm=min(256, seq))

    rows_s = nb * t_new
    xs = x_sample.reshape(rows_s, D_MODEL)
    z_s, xbc_s, mla_s, gates_s = _inproj(xs, p["norm_in_w"], p["w_main"], p["w_tail"], tm=rows_s)
    y_norm_s, ssm_s, conv_s = _ssd(xbc_s.reshape(nb, t_new, CONV_DIM), z_s.reshape(nb, t_new, SSM_INNER),
                                   mla_s.reshape(nb, t_new, SEG_MLA), state_conv[0],
                                   state_ssm[0].reshape(nb, SSM_INNER, SSM_STATE), p, q_sample, nsub=1)
    ct_s, sn_s = _rope_tables(past + jnp.arange(t_new))
    ckv_s, kr_s, ql, qr = _mla_sample(mla_s, jnp.tile(ct_s, (nb, 1)), jnp.tile(sn_s, (nb, 1)), p)
    ckv_s = ckv_s.reshape(nb, t_new, KV_LORA)
    kr_s = kr_s.reshape(nb, t_new, QK_ROPE)
    attn_s = _attention_cached(ql, qr, cache_kv_latent[0], cache_k_rope[0], ckv_s, kr_s, p["w_uv"])
    mixed_s = _merge(y_norm_s.reshape(rows_s, SSM_INNER), attn_s.reshape(rows_s, MLA_INNER), gates_s, p, tm=rows_s)
    y_sample = _outproj(xs, mixed_s, p, tm=rows_s).reshape(nb, t_new, D_MODEL)

    hshape = (SSM_HEADS, SSM_HEAD_DIM, SSM_STATE)
    return (y_prompt[None], y_sample,
            ckv_p[None, None], kr_p[None, None], ssm_p.reshape((1, 1) + hshape), conv_p[None],
            ckv_s[None], kr_s[None], ssm_s.reshape((1, nb) + hshape), conv_s[None])
```

```python
import functools
import math

import jax
import jax.numpy as jnp
import numpy as np
from jax import lax
from jax.experimental import pallas as pl
from jax.experimental.pallas import tpu as pltpu

F32 = jnp.float32
BF16 = jnp.bfloat16

D_MODEL = 2048
CHUNK = 64
SSM_INNER = 4096
SSM_HEAD_DIM = 64
SSM_HEADS = 64
SSM_GROUPS = 8
SSM_STATE = 128
CONV_WIDTH = 4
CONV_DIM = SSM_INNER + 2 * SSM_GROUPS * SSM_STATE
GROUP_LANES = SSM_INNER // SSM_GROUPS
MLA_HEADS = 16
Q_LORA = 512
KV_LORA = 512
QK_NOPE = 128
QK_ROPE = 64
V_DIM = 128
MLA_INNER = MLA_HEADS * V_DIM
ROPE_THETA = 10000.0
EPS = 1e-6
QK_PAD = 256
SCALE = (QK_NOPE + QK_ROPE) ** -0.5
LOG2E = math.log2(math.e)
V_ROWS = V_DIM + 16
HEADS_PER_TRIP = 5

LANES = 128
VMEM_LIMIT = 56 * 1024 * 1024

IN_TN = 512
SEG_Z = SSM_INNER
SEG_XBC = CONV_DIM
SEG_MLA = Q_LORA + KV_LORA + 512
SEG_G = MLA_INNER + 2 * D_MODEL
IN_SEGS = (SEG_Z, SEG_XBC, SEG_MLA, SEG_G)


def _cparams(sem, vmem=VMEM_LIMIT):
    return pltpu.CompilerParams(dimension_semantics=sem, vmem_limit_bytes=vmem)


def _const_spec(shape):
    nd = len(shape)
    return pl.BlockSpec(shape, lambda *_: (0,) * nd, pipeline_mode=pl.Buffered(1))


def _sigmoid(x):
    return 0.5 + 0.5 * jnp.tanh(0.5 * x)


def _silu(x):
    h = 0.5 * x
    return h + h * jnp.tanh(h)


def _rms(x, w):
    return x * lax.rsqrt(jnp.mean(x * x, axis=-1, keepdims=True) + EPS) * w


def _dot(a, b):
    return jnp.dot(a, b, preferred_element_type=F32)


def _dot_nt(a, b):
    return lax.dot_general(a, b, (((1,), (1,)), ((), ())), preferred_element_type=F32)


def _dot_tn(a, b):
    return lax.dot_general(a, b, (((0,), (0,)), ((), ())), preferred_element_type=F32)


def _split3(v):
    hi = v.astype(BF16)
    r1 = v - hi.astype(F32)
    mid = r1.astype(BF16)
    lo = (r1 - mid.astype(F32)).astype(BF16)
    return hi, mid, lo


def _inproj_kernel(x_ref, nw_ref, wm_ref, wt_ref, *rest, bounds, n_main):
    outs, h_ref = rest[:-1], rest[-1]
    j = pl.program_id(1)

    @pl.when(j == 0)
    def _():
        h_ref[...] = _rms(x_ref[...], nw_ref[...]).astype(BF16)

    for o_ref, (s, e) in zip(outs, bounds):
        w_ref = wm_ref if e <= n_main else wt_ref

        @pl.when((j >= s) & (j < e))
        def _(o_ref=o_ref, w_ref=w_ref):
            o_ref[...] = _dot(h_ref[...], w_ref[...])


def _inproj(x, norm_w, w_main, w_tail, tm):
    rows = x.shape[0]
    counts = [s // IN_TN for s in IN_SEGS]
    starts = np.cumsum([0] + counts)
    bounds = tuple((int(starts[k]), int(starts[k + 1])) for k in range(len(counts)))
    n_main, n_tail = bounds[1][1], int(starts[-1]) - bounds[1][1]

    def omap(k):
        s, n = bounds[k][0], counts[k]
        return lambda i, j: (i, jnp.clip(j - s, 0, n - 1))

    return pl.pallas_call(
        functools.partial(_inproj_kernel, bounds=bounds, n_main=n_main),
        grid=(rows // tm, int(starts[-1])),
        in_specs=[
            pl.BlockSpec((tm, D_MODEL), lambda i, j: (i, 0)),
            pl.BlockSpec((1, D_MODEL), lambda i, j: (0, 0)),
            pl.BlockSpec((D_MODEL, IN_TN), lambda i, j: (0, jnp.minimum(j, n_main - 1))),
            pl.BlockSpec((D_MODEL, IN_TN), lambda i, j: (0, jnp.clip(j - n_main, 0, n_tail - 1))),
        ],
        out_specs=[pl.BlockSpec((tm, IN_TN), omap(k)) for k in range(len(counts))],
        out_shape=[jax.ShapeDtypeStruct((rows, s), F32) for s in IN_SEGS],
        scratch_shapes=[pltpu.VMEM((tm, D_MODEL), BF16)],
        compiler_params=_cparams(("arbitrary", "arbitrary")),
        name="inproj",
    )(x, norm_w, w_main, w_tail)


CONV_PAD = 8
CONV_SLAB = 256


def _ssd_kernel(xbc_ref, z_ref, sm_ref, cprev_ref, s0_ref, cw_ref, cb_ref, dtb_ref, alog_ref, dsk_ref,
                nw_ref, ep_ref, eq_ref, y_ref, sout_ref, cout_ref, st_ref, xpad_ref, xc_ref, exp_p_ref,
                exp_q_ref, yd_ref, *, q, nsub):
    c = pl.program_id(1)
    nc = pl.num_programs(1)
    rows = nsub * q
    hp = LANES // q
    tiles_per_group = (SSM_HEADS // SSM_GROUPS) // hp
    tile_w = hp * SSM_HEAD_DIM
    lo = CONV_PAD - (CONV_WIDTH - 1)

    @pl.when(c == 0)
    def _():
        st_ref[...] = s0_ref[0].T
        xpad_ref[0:lo, :] = jnp.zeros((lo, CONV_DIM), F32)
        xpad_ref[lo:CONV_PAD, :] = cprev_ref[0]

    assert CONV_WIDTH == 4
    xpad_ref[CONV_PAD:CONV_PAD + rows, :] = xbc_ref[0]
    for s in range(0, CONV_DIM, CONV_SLAB):
        xw = xpad_ref[:, s:s + CONV_SLAB]
        x1 = pltpu.roll(xw, 1, 0)
        w = [cw_ref[k:k + 1, s:s + CONV_SLAB] for k in range(CONV_WIDTH)]
        u = (xw * w[3] + x1 * w[2]) + pltpu.roll(xw * w[1] + x1 * w[0], 2, 0)
        xc_ref[:, s:s + CONV_SLAB] = _silu(u[CONV_PAD:, :] + cb_ref[:, s:s + CONV_SLAB])
    tail_rows = xpad_ref[CONV_PAD + rows - (CONV_WIDTH - 1):CONV_PAD + rows, :]
    xpad_ref[lo:CONV_PAD, :] = tail_rows

    ri = lax.broadcasted_iota(jnp.int32, (q, q), 0)
    ci = lax.broadcasted_iota(jnp.int32, (q, q), 1)
    tril = jnp.where(ci <= ri, 1.0, 0.0).astype(BF16)
    r128 = lax.broadcasted_iota(jnp.int32, (q, LANES), 0)
    c128 = lax.broadcasted_iota(jnp.int32, (q, LANES), 1)
    key128 = c128 & (q - 1)
    causal = key128 <= r128
    diag = key128 == r128
    br = lax.broadcasted_iota(jnp.int32, (LANES, tile_w), 0)
    bc = lax.broadcasted_iota(jnp.int32, (LANES, tile_w), 1)
    blockdiag = (br // q) == (bc // SSM_HEAD_DIM)
    neg_a = -jnp.exp(alog_ref[...])

    for u in range(nsub):
        r0 = u * q
        xdt = sm_ref[0, r0:r0 + q, 0:SSM_HEADS] + dtb_ref[...]
        dt = jnp.maximum(xdt, 0.0) + jnp.log1p(jnp.exp(-jnp.abs(xdt)))
        h3 = _split3(dt * neg_a)
        acs = _dot(tril, h3[0]) + _dot(tril, h3[1]) + _dot(tril, h3[2])

        def expand(v, e_ref):
            pieces = jnp.concatenate([piece.astype(F32) for piece in _split3(v)], axis=1).astype(BF16)
            return _dot(pieces, e_ref[...])

        exp_q_ref[u] = expand(jnp.concatenate([acs, dt], axis=0), eq_ref)
        exp_p_ref[u] = expand(jnp.concatenate([jnp.exp(acs), jnp.exp(acs[q - 1:q, :] - acs) * dt], axis=0), ep_ref)

        for g in range(SSM_GROUPS):
            gs = g * GROUP_LANES
            bs = SSM_INNER + g * SSM_STATE
            cs = SSM_INNER + (SSM_GROUPS + g) * SSM_STATE
            b_bf = xc_ref[r0:r0 + q, bs:bs + SSM_STATE].astype(BF16)
            c_bf = xc_ref[r0:r0 + q, cs:cs + SSM_STATE].astype(BF16)
            cb_t = _dot_nt(c_bf, jnp.concatenate([b_bf] * hp, axis=0))

            for tt in range(tiles_per_group):
                t = g * tiles_per_group + tt
                a_t = exp_q_ref[u, 0:q, t * LANES:(t + 1) * LANES]
                d_t = exp_q_ref[u, q:2 * q, t * LANES:(t + 1) * LANES]
                a_key = jnp.sum(jnp.where(diag, a_t, 0.0), axis=0, keepdims=True)
                d_key = jnp.sum(jnp.where(diag, d_t, 0.0), axis=0, keepdims=True)
                w_t = jnp.exp(jnp.where(causal, a_t - a_key, -jnp.inf)) * d_key * cb_t
                x_t = xc_ref[r0:r0 + q, t * tile_w:(t + 1) * tile_w]
                rhs = jnp.where(blockdiag, jnp.concatenate([x_t] * hp, axis=0), 0.0).astype(BF16)
                yd_ref[r0:r0 + q, t * tile_w:(t + 1) * tile_w] = _dot(w_t.astype(BF16), rhs)

            decay = exp_p_ref[u, 0:q, gs:gs + GROUP_LANES]
            tail = exp_p_ref[u, q:2 * q, gs:gs + GROUP_LANES]
            xs_g = xc_ref[r0:r0 + q, gs:gs + GROUP_LANES]
            st_g = st_ref[:, gs:gs + GROUP_LANES]
            y_off = _dot(c_bf, st_g.astype(BF16)) * decay
            xt = (xs_g * tail).astype(BF16)
            st_ref[:, gs:gs + GROUP_LANES] = st_g * decay[q - 1:q, :] + _dot_tn(b_bf, xt)

            y = yd_ref[r0:r0 + q, gs:gs + GROUP_LANES] + y_off + dsk_ref[:, gs:gs + GROUP_LANES] * xs_g
            y = y * _silu(z_ref[0, r0:r0 + q, gs:gs + GROUP_LANES])
            y = y * lax.rsqrt(jnp.mean(y * y, axis=-1, keepdims=True) + EPS)
            y_ref[0, r0:r0 + q, gs:gs + GROUP_LANES] = (y * nw_ref[:, gs:gs + GROUP_LANES]).astype(y_ref.dtype)

    @pl.when(c == nc - 1)
    def _():
        sout_ref[0] = st_ref[...].T
        cout_ref[0] = tail_rows


def _ssd(xbc, z, mla, conv_prev, s0, p, q, nsub):
    nb, lb = xbc.shape[0], xbc.shape[1]
    rows = q * nsub
    nc = lb // rows
    kq = SSM_HEADS * q
    in_specs = [
        pl.BlockSpec((1, rows, CONV_DIM), lambda b, c: (b, c, 0)),
        pl.BlockSpec((1, rows, SSM_INNER), lambda b, c: (b, c, 0)),
        pl.BlockSpec((1, rows, 512), lambda b, c: (b, c, (Q_LORA + KV_LORA) // 512)),
        pl.BlockSpec((1, CONV_WIDTH - 1, CONV_DIM), lambda b, c: (b, 0, 0)),
        pl.BlockSpec((1, SSM_INNER, SSM_STATE), lambda b, c: (b, 0, 0)),
        _const_spec((CONV_WIDTH, CONV_DIM)),
        _const_spec((1, CONV_DIM)),
        _const_spec((1, SSM_HEADS)),
        _const_spec((1, SSM_HEADS)),
        _const_spec((1, SSM_INNER)),
        _const_spec((1, SSM_INNER)),
        _const_spec((3 * SSM_HEADS, SSM_INNER)),
        _const_spec((3 * SSM_HEADS, kq)),
    ]
    out_specs = [
        pl.BlockSpec((1, rows, SSM_INNER), lambda b, c: (b, c, 0)),
        pl.BlockSpec((1, SSM_INNER, SSM_STATE), lambda b, c: (b, 0, 0)),
        pl.BlockSpec((1, CONV_WIDTH - 1, CONV_DIM), lambda b, c: (b, 0, 0)),
    ]
    out_shape = [
        jax.ShapeDtypeStruct((nb, lb, SSM_INNER), BF16),
        jax.ShapeDtypeStruct((nb, SSM_INNER, SSM_STATE), F32),
        jax.ShapeDtypeStruct((nb, CONV_WIDTH - 1, CONV_DIM), F32),
    ]
    scratch = [
        pltpu.VMEM((SSM_STATE, SSM_INNER), F32),
        pltpu.VMEM((CONV_PAD + rows, CONV_DIM), F32),
        pltpu.VMEM((rows, CONV_DIM), F32),
        pltpu.VMEM((nsub, 2 * q, SSM_INNER), F32),
        pltpu.VMEM((nsub, 2 * q, kq), F32),
        pltpu.VMEM((rows, SSM_INNER), F32),
    ]
    return pl.pallas_call(
        functools.partial(_ssd_kernel, q=q, nsub=nsub),
        grid=(nb, nc),
        in_specs=in_specs,
        out_specs=out_specs,
        out_shape=out_shape,
        scratch_shapes=scratch,
        compiler_params=_cparams(("arbitrary", "arbitrary")),
        name="ssd",
    )(xbc, z, mla, conv_prev, s0, p["conv_w"], p["conv_b"], p["dt_bias"], p["a_log"], p["d_skip_p"],
      p["ssm_norm_w"], p["e_p"], p["e_q%d" % q])


def _mla_latents(mla_ref, ct_ref, sn_ref, qnw_ref, kvnw_ref, ckv_ref, kr_ref):
    m = mla_ref[...]
    cqn = _rms(m[:, 0:Q_LORA], qnw_ref[...])
    ckv = _rms(m[:, Q_LORA:Q_LORA + KV_LORA], kvnw_ref[...])
    ckv_ref[...] = ckv
    small = m[:, Q_LORA + KV_LORA:]
    ct, sn = ct_ref[...], sn_ref[...]
    kr128 = small[:, 128:256] * ct + small[:, 256:384] * sn
    kr_ref[...] = kr128[:, 0:QK_ROPE]
    return cqn, ckv, kr128, ct, sn


def _mla_prompt_kernel(mla_ref, ct_ref, sn_ref, ctt_ref, snt_ref, qnw_ref, kvnw_ref, wqat_ref, wqbt_ref, wk_ref,
                       wvt_ref, ckv_ref, kr_ref, qt_ref, kn_ref, krb_ref, vt_ref):
    cqn, ckv, kr128, _, _ = _mla_latents(mla_ref, ct_ref, sn_ref, qnw_ref, kvnw_ref, ckv_ref, kr_ref)
    cqn_t = cqn.T.astype(BF16)
    ckv_t = ckv.T.astype(BF16)
    ckv_bf = ckv.astype(BF16)
    krb_ref[...] = kr128.astype(BF16)
    ctt, snt = ctt_ref[...], snt_ref[...]
    tm = ckv_bf.shape[0]
    ones_row = jnp.where(lax.broadcasted_iota(jnp.int32, (V_ROWS - V_DIM, tm), 0) == 0, 1.0, 0.0).astype(BF16)
    qs = SCALE * LOG2E
    for h in range(MLA_HEADS):
        qa = _dot(wqat_ref[h * QK_PAD:(h + 1) * QK_PAD, :], cqn_t)
        qb = _dot(wqbt_ref[h * LANES:(h + 1) * LANES, :], cqn_t)
        qt_ref[h, 0:QK_NOPE, :] = (qa[0:QK_NOPE] * qs).astype(BF16)
        qt_ref[h, QK_NOPE:QK_PAD, :] = ((qa[QK_NOPE:QK_PAD] * ctt + qb * snt) * qs).astype(BF16)
        kn_ref[h] = _dot(ckv_bf, wk_ref[:, h * QK_NOPE:(h + 1) * QK_NOPE]).astype(BF16)
        vt_ref[h, 0, 0:V_DIM, :] = _dot(wvt_ref[h * V_DIM:(h + 1) * V_DIM, :], ckv_t).astype(BF16)
        vt_ref[h, 0, V_DIM:V_ROWS, :] = ones_row


def _mla_prompt(mla, ct, sn, p, tm):
    rows = mla.shape[0]
    row = lambda w: pl.BlockSpec((tm, w), lambda i: (i, 0))
    col = pl.BlockSpec((LANES, tm), lambda i: (0, i))
    return pl.pallas_call(
        _mla_prompt_kernel,
        grid=(rows // tm,),
        in_specs=[row(SEG_MLA), row(LANES), row(LANES), col, col, _const_spec((1, Q_LORA)),
                  _const_spec((1, KV_LORA)),
                  _const_spec((MLA_HEADS * QK_PAD, Q_LORA)), _const_spec((MLA_HEADS * LANES, Q_LORA)),
                  _const_spec((KV_LORA, MLA_HEADS * QK_NOPE)), _const_spec((MLA_INNER, KV_LORA))],
        out_specs=[row(KV_LORA), row(QK_ROPE),
                   pl.BlockSpec((MLA_HEADS, QK_PAD, tm), lambda i: (0, 0, i)),
                   pl.BlockSpec((MLA_HEADS, tm, QK_NOPE), lambda i: (0, i, 0)), row(LANES),
                   pl.BlockSpec((MLA_HEADS, 1, V_ROWS, tm), lambda i: (0, i, 0, 0))],
        out_shape=[jax.ShapeDtypeStruct((rows, KV_LORA), F32), jax.ShapeDtypeStruct((rows, QK_ROPE), F32),
                   jax.ShapeDtypeStruct((MLA_HEADS, QK_PAD, rows), BF16),
                   jax.ShapeDtypeStruct((MLA_HEADS, rows, QK_NOPE), BF16),
                   jax.ShapeDtypeStruct((rows, LANES), BF16),
                   jax.ShapeDtypeStruct((MLA_HEADS, rows // tm, V_ROWS, tm), BF16)],
        compiler_params=_cparams(("arbitrary",)),
        name="mla_prep_prompt",
    )(mla, ct, sn, ct.T, sn.T, p["q_norm_w"], p["kv_norm_w"], p["wq_a"].T, p["wq_b"].T, p["w_uk"], p["w_uv_t"])


def _mla_sample_kernel(mla_ref, ct_ref, sn_ref, qnw_ref, kvnw_ref, wqa_ref, wqb_ref, wk_ref,
                       ckv_ref, kr_ref, ql_ref, qr_ref):
    cqn, _, _, ct, sn = _mla_latents(mla_ref, ct_ref, sn_ref, qnw_ref, kvnw_ref, ckv_ref, kr_ref)
    cqn = cqn.astype(BF16)
    qa = _dot(cqn, wqa_ref[...])
    qb = _dot(cqn, wqb_ref[...])
    for h in range(MLA_HEADS):
        o = h * QK_PAD
        q_rope = qa[:, o + QK_NOPE:o + QK_PAD] * ct + qb[:, h * LANES:(h + 1) * LANES] * sn
        qr_ref[h] = (q_rope * SCALE).astype(BF16)
        q_nope = qa[:, o:o + QK_NOPE].astype(BF16)
        q_lat = _dot_nt(q_nope, wk_ref[:, h * QK_NOPE:(h + 1) * QK_NOPE])
        ql_ref[h] = (q_lat * SCALE).astype(BF16)


def _mla_sample(mla, ct, sn, p):
    rows = mla.shape[0]
    full = lambda w: pl.BlockSpec((rows, w), lambda i: (0, 0))
    return pl.pallas_call(
        _mla_sample_kernel,
        grid=(1,),
        in_specs=[full(SEG_MLA), full(LANES), full(LANES), _const_spec((1, Q_LORA)), _const_spec((1, KV_LORA)),
                  _const_spec((Q_LORA, MLA_HEADS * QK_PAD)), _const_spec((Q_LORA, MLA_HEADS * LANES)),
                  _const_spec((KV_LORA, MLA_HEADS * QK_NOPE))],
        out_specs=[full(KV_LORA), full(QK_ROPE),
                   pl.BlockSpec((MLA_HEADS, rows, KV_LORA), lambda i: (0, 0, 0)),
                   pl.BlockSpec((MLA_HEADS, rows, LANES), lambda i: (0, 0, 0))],
        out_shape=[jax.ShapeDtypeStruct((rows, KV_LORA), F32), jax.ShapeDtypeStruct((rows, QK_ROPE), F32),
                   jax.ShapeDtypeStruct((MLA_HEADS, rows, KV_LORA), BF16),
                   jax.ShapeDtypeStruct((MLA_HEADS, rows, LANES), BF16)],
        compiler_params=_cparams(("arbitrary",)),
        name="mla_prep_sample",
    )(mla, ct, sn, p["q_norm_w"], p["kv_norm_w"], p["wq_a"], p["wq_b"], p["w_uk"])


def _attn_kernel(qi_ref, kj_ref, qt_ref, kn_ref, kr_ref, vt_ref, o_ref, m_ref, acc_ref, s0_ref, s1_ref, p0_ref,
                 p1_ref, a0_ref, a1_ref, *, tq, tks):
    s_refs, p_refs, a_refs = (s0_ref, s1_ref), (p0_ref, p1_ref), (a0_ref, a1_ref)
    n_sub = len(s_refs)
    t = pl.program_id(0)
    i = qi_ref[t]
    j = kj_ref[t]
    tk = n_sub * tks
    last_j = ((i + 1) * tq - 1) // tk

    @pl.when(j == 0)
    def _():
        m_ref[...] = jnp.full(m_ref.shape, -jnp.inf, F32)
        acc_ref[...] = jnp.zeros(acc_ref.shape, F32)

    def step(masked):
        if masked:
            q_chunk = (i * tq + lax.broadcasted_iota(jnp.int32, (tks, tq), 1)) // CHUNK
            k_chunk = (j * tk + lax.broadcasted_iota(jnp.int32, (tks, tq), 0)) // CHUNK
            visible = [k_chunk + (c * tks) // CHUNK <= q_chunk for c in range(n_sub)]

        def scores(h, c):
            k_h = jnp.concatenate([kn_ref[h, c * tks:(c + 1) * tks, :], kr_ref[c * tks:(c + 1) * tks, :]], axis=1)
            s_refs[c][...] = _dot(k_h, qt_ref[h])

        def softmax(h, c):
            s = s_refs[c][...]
            if masked:
                s = jnp.where(visible[c], s, -jnp.inf)
            m_prev = m_ref[h]
            m_new = jnp.maximum(m_prev, jnp.max(s, axis=0, keepdims=True))
            m_ref[h] = m_new
            p_refs[c][...] = jnp.exp2(s - m_new).astype(BF16)
            a_refs[c][...] = jnp.exp2(m_prev - m_new)

        def values(h, c):
            acc_ref[h] = acc_ref[h] * a_refs[c][...] + _dot(vt_ref[h, c], p_refs[c][...])

        scores(0, 0)
        scores(0, 1)
        softmax(0, 0)

        def body(u, carry):
            for e in range(HEADS_PER_TRIP):
                h = u * HEADS_PER_TRIP + 1 + e
                scores(h, 0)
                softmax(h - 1, 1)
                values(h - 1, 0)
                scores(h, 1)
                softmax(h, 0)
                values(h - 1, 1)
            return carry

        lax.fori_loop(0, (MLA_HEADS - 1) // HEADS_PER_TRIP, body, 0)
        softmax(MLA_HEADS - 1, 1)
        values(MLA_HEADS - 1, 0)
        values(MLA_HEADS - 1, 1)

    @pl.when(j < last_j)
    def _():
        step(False)

    @pl.when(j == last_j)
    def _():
        step(True)
        for h in range(MLA_HEADS):
            den = acc_ref[h, V_DIM:V_DIM + 1, :]
            o_ref[:, h * V_DIM:(h + 1) * V_DIM] = (acc_ref[h, 0:V_DIM, :] * (1.0 / den)).T


def _attention_prompt(qt, kn, kr, vt, tq):
    rows = kn.shape[1]
    n_sub, tks = 2, vt.shape[3]
    tk = tks * n_sub
    qi, kj = [], []
    for i in range(rows // tq):
        for j in range(((i + 1) * tq - 1) // tk + 1):
            qi.append(i)
            kj.append(j)
    qi = jnp.asarray(np.array(qi, np.int32))
    kj = jnp.asarray(np.array(kj, np.int32))
    grid_spec = pltpu.PrefetchScalarGridSpec(
        num_scalar_prefetch=2,
        grid=(int(qi.shape[0]),),
        in_specs=[
            pl.BlockSpec((MLA_HEADS, QK_PAD, tq), lambda t, qi, kj: (0, 0, qi[t])),
            pl.BlockSpec((MLA_HEADS, tk, QK_NOPE), lambda t, qi, kj: (0, kj[t], 0)),
            pl.BlockSpec((tk, LANES), lambda t, qi, kj: (kj[t], 0)),
            pl.BlockSpec((MLA_HEADS, n_sub, V_ROWS, tks), lambda t, qi, kj: (0, kj[t], 0, 0)),
        ],
        out_specs=pl.BlockSpec((tq, MLA_INNER), lambda t, qi, kj: (qi[t], 0)),
        scratch_shapes=[
            pltpu.VMEM((MLA_HEADS, 1, tq), F32),
            pltpu.VMEM((MLA_HEADS, V_ROWS, tq), F32),
            pltpu.VMEM((tks, tq), F32), pltpu.VMEM((tks, tq), F32),
            pltpu.VMEM((tks, tq), BF16), pltpu.VMEM((tks, tq), BF16),
            pltpu.VMEM((1, tq), F32), pltpu.VMEM((1, tq), F32),
        ],
    )
    return pl.pallas_call(
        functools.partial(_attn_kernel, tq=tq, tks=tks),
        grid_spec=grid_spec,
        out_shape=jax.ShapeDtypeStruct((rows, MLA_INNER), F32),
        compiler_params=_cparams(("arbitrary",)),
        name="attn_prompt",
    )(qi, kj, qt, kn, kr, vt)


def _attn_cached_kernel(ql_ref, qr_ref, ckv_c_ref, kr_c_ref, ckv_n_ref, kr_n_ref, wv_ref, o_ref, *, t_new):
    rows = MLA_HEADS * t_new
    ql = ql_ref[...].reshape(rows, KV_LORA)
    qr = qr_ref[...].reshape(rows, LANES)[:, 0:QK_ROPE]
    kc = ckv_c_ref[0].astype(BF16)
    kn = ckv_n_ref[0].astype(BF16)
    s_c = _dot_nt(ql, kc) + _dot_nt(qr, kr_c_ref[0].astype(BF16))
    s_n = _dot_nt(ql, kn) + _dot_nt(qr, kr_n_ref[0].astype(BF16))
    m = jnp.maximum(jnp.max(s_c, axis=1, keepdims=True), jnp.max(s_n, axis=1, keepdims=True))
    p_c = jnp.exp(s_c - m)
    p_n = jnp.exp(s_n - m)
    den = jnp.sum(p_c, axis=1, keepdims=True) + jnp.sum(p_n, axis=1, keepdims=True)
    o_lat = (_dot(p_c.astype(BF16), kc) + _dot(p_n.astype(BF16), kn)) / den
    o_bf = o_lat.astype(BF16)
    for h in range(MLA_HEADS):
        o_ref[0, :, h * V_DIM:(h + 1) * V_DIM] = _dot(o_bf[h * t_new:(h + 1) * t_new, :],
                                                        wv_ref[:, h * V_DIM:(h + 1) * V_DIM])


def _attention_cached(ql, qr, cache_kv, cache_kr, ckv_new, kr_new, w_uv):
    nb, past = cache_kv.shape[0], cache_kv.shape[1]
    t_new = ckv_new.shape[1]
    return pl.pallas_call(
        functools.partial(_attn_cached_kernel, t_new=t_new),
        grid=(nb,),
        in_specs=[
            pl.BlockSpec((MLA_HEADS, t_new, KV_LORA), lambda b: (0, b, 0)),
            pl.BlockSpec((MLA_HEADS, t_new, LANES), lambda b: (0, b, 0)),
            pl.BlockSpec((1, past, KV_LORA), lambda b: (b, 0, 0)),
            pl.BlockSpec((1, past, QK_ROPE), lambda b: (b, 0, 0)),
            pl.BlockSpec((1, t_new, KV_LORA), lambda b: (b, 0, 0)),
            pl.BlockSpec((1, t_new, QK_ROPE), lambda b: (b, 0, 0)),
            _const_spec((KV_LORA, MLA_INNER)),
        ],
        out_specs=pl.BlockSpec((1, t_new, MLA_INNER), lambda b: (b, 0, 0)),
        out_shape=jax.ShapeDtypeStruct((nb, t_new, MLA_INNER), F32),
        compiler_params=_cparams(("arbitrary",)),
        name="attn_cached",
    )(ql, qr, cache_kv, cache_kr, ckv_new, kr_new, w_uv)


def _merge_kernel(y_ref, attn_ref, gm_ref, gs_ref, ga_ref, wssm_ref, wmla_ref, o_ref):
    y_ssm = _dot(y_ref[...], wssm_ref[...])
    y_mla = _dot((attn_ref[...] * _silu(gm_ref[...])).astype(BF16), wmla_ref[...])
    o_ref[...] = (_sigmoid(gs_ref[...]) * y_ssm + _sigmoid(ga_ref[...]) * y_mla).astype(o_ref.dtype)


def _merge(y_norm, attn, gates, p, tm):
    rows = y_norm.shape[0]
    gate = lambda k: pl.BlockSpec((tm, D_MODEL), lambda i: (i, k))
    return pl.pallas_call(
        _merge_kernel,
        grid=(rows // tm,),
        in_specs=[pl.BlockSpec((tm, SSM_INNER), lambda i: (i, 0)), pl.BlockSpec((tm, MLA_INNER), lambda i: (i, 0)),
                  gate(0), gate(1), gate(2),
                  _const_spec((SSM_INNER, D_MODEL)), _const_spec((MLA_INNER, D_MODEL))],
        out_specs=pl.BlockSpec((tm, D_MODEL), lambda i: (i, 0)),
        out_shape=jax.ShapeDtypeStruct((rows, D_MODEL), BF16),
        compiler_params=_cparams(("arbitrary",)),
        name="merge",
    )(y_norm, attn, gates, gates, gates, p["w_ssm_out"], p["w_mla_out"])


def _outproj_kernel(x_ref, mixed_ref, w_ref, nw_ref, o_ref):
    x_out = x_ref[...] + _dot(mixed_ref[...], w_ref[...])
    o_ref[...] = _rms(x_out, nw_ref[...])


def _outproj(x, mixed, p, tm):
    rows = x.shape[0]
    row = pl.BlockSpec((tm, D_MODEL), lambda i: (i, 0))
    return pl.pallas_call(
        _outproj_kernel,
        grid=(rows // tm,),
        in_specs=[row, row, _const_spec((D_MODEL, D_MODEL)), _const_spec((1, D_MODEL))],
        out_specs=row,
        out_shape=jax.ShapeDtypeStruct((rows, D_MODEL), F32),
        compiler_params=_cparams(("arbitrary",)),
        name="outproj",
    )(x, mixed, p["w_out"], p["final_norm_w"])


def _prepare(norm_in_w, w_in, conv_w, conv_b, dt_bias, a_log, d_skip, ssm_norm_w, w_ssm_out, q_norm_w, w_q_up,
             kv_norm_w, w_kv_up, w_mla_out, w_out, final_norm_w, chunk_lens):
    offs = np.cumsum((0, SSM_INNER, CONV_DIM, SSM_HEADS, Q_LORA, KV_LORA, QK_ROPE, MLA_INNER, D_MODEL, D_MODEL))
    w_in_bf = w_in.astype(BF16)
    col = lambda k: w_in_bf[:, int(offs[k]):int(offs[k + 1])]
    half = QK_ROPE // 2
    kr_w = col(5)
    zeros = lambda n: jnp.zeros((D_MODEL, n), BF16)
    w_tail = jnp.concatenate([
        col(3), col(4),
        col(2), zeros(64), kr_w, zeros(64), kr_w[:, half:], kr_w[:, :half], zeros(64 + 128),
        col(6), col(7), col(8)], axis=1)

    wq = w_q_up.reshape(Q_LORA, MLA_HEADS, QK_NOPE + QK_ROPE)
    wq_rope = wq[..., QK_NOPE:]
    zq = jnp.zeros((Q_LORA, MLA_HEADS, 64), w_q_up.dtype)
    wq_a = jnp.concatenate([wq, zq], axis=-1).reshape(Q_LORA, MLA_HEADS * QK_PAD).astype(BF16)
    wq_b = jnp.concatenate([wq_rope[..., half:], wq_rope[..., :half], zq], axis=-1)
    wq_b = wq_b.reshape(Q_LORA, MLA_HEADS * LANES).astype(BF16)

    wkv = w_kv_up.reshape(KV_LORA, MLA_HEADS, QK_NOPE + V_DIM)
    w_uk = wkv[..., :QK_NOPE].reshape(KV_LORA, MLA_HEADS * QK_NOPE).astype(BF16)
    w_uv_h = wkv[..., QK_NOPE:]
    w_uv = w_uv_h.reshape(KV_LORA, MLA_INNER).astype(BF16)
    w_uv_t = w_uv.T

    head_of_lane = np.arange(SSM_INNER) // SSM_HEAD_DIM
    p = dict(
        norm_in_w=norm_in_w.reshape(1, D_MODEL), w_main=w_in_bf, w_tail=w_tail,
        conv_w=conv_w, conv_b=conv_b.reshape(1, CONV_DIM), dt_bias=dt_bias.reshape(1, SSM_HEADS),
        a_log=a_log.reshape(1, SSM_HEADS), d_skip_p=jnp.repeat(d_skip, SSM_HEAD_DIM).reshape(1, SSM_INNER),
        ssm_norm_w=ssm_norm_w.reshape(1, SSM_INNER), w_ssm_out=w_ssm_out.astype(BF16),
        q_norm_w=q_norm_w.reshape(1, Q_LORA), kv_norm_w=kv_norm_w.reshape(1, KV_LORA),
        wq_a=wq_a, wq_b=wq_b, w_uk=w_uk, w_uv=w_uv, w_uv_t=w_uv_t,
        w_mla_out=w_mla_out.astype(BF16), w_out=w_out.astype(BF16),
        final_norm_w=final_norm_w.reshape(1, D_MODEL),
        e_p=jnp.asarray(np.tile(head_of_lane[None, :] == np.arange(SSM_HEADS)[:, None], (3, 1)), BF16),
    )
    for q in chunk_lens:
        head_of_key_lane = np.arange(SSM_HEADS * q) // q
        p["e_q%d" % q] = jnp.asarray(np.tile(head_of_key_lane[None, :] == np.arange(SSM_HEADS)[:, None], (3, 1)),
                                     BF16)
    return p


def _rope_tables(pos):
    half = QK_ROPE // 2
    inv = 1.0 / (ROPE_THETA ** (jnp.arange(half, dtype=F32) * (2.0 / QK_ROPE)))
    ang = pos.astype(F32)[:, None] * inv[None, :]
    cos, sin = jnp.cos(ang), jnp.sin(ang)
    pad = jnp.zeros((pos.shape[0], LANES - QK_ROPE), F32)
    return jnp.concatenate([cos, cos, pad], axis=1), jnp.concatenate([-sin, sin, pad], axis=1)


def kernel(x_prompt, x_sample, cache_kv_latent, cache_k_rope, state_ssm, state_conv, norm_in_w, w_in, conv_w,
           conv_b, dt_bias, a_log, d_skip, ssm_norm_w, w_ssm_out, q_norm_w, w_q_up, kv_norm_w, w_kv_up, w_mla_out,
           w_out, final_norm_w):
    depth = w_in.shape[0]
    assert depth == 1 and x_prompt.shape[0] == 1
    seq = x_prompt.shape[1]
    nb, t_new = x_sample.shape[0], x_sample.shape[1]
    past = cache_kv_latent.shape[2]
    q_prompt, q_sample = min(CHUNK, seq), min(CHUNK, t_new)
    p = _prepare(norm_in_w[0], w_in[0], conv_w[0], conv_b[0], dt_bias[0], a_log[0], d_skip[0], ssm_norm_w[0],
                 w_ssm_out[0], q_norm_w[0], w_q_up[0], kv_norm_w[0], w_kv_up[0], w_mla_out[0], w_out[0],
                 final_norm_w, sorted({q_prompt, q_sample}))

    xp = x_prompt[0]
    z, xbc, mla, gates = _inproj(xp, p["norm_in_w"], p["w_main"], p["w_tail"], tm=min(1024, seq))
    y_norm, ssm_p, conv_p = _ssd(xbc[None], z[None], mla[None],
                                 jnp.zeros((1, CONV_WIDTH - 1, CONV_DIM), F32),
                                 jnp.zeros((1, SSM_INNER, SSM_STATE), F32), p, q_prompt,
                                 nsub=4 if seq % (4 * q_prompt) == 0 else 1)
    ct, sn = _rope_tables(jnp.arange(seq))
    ckv_p, kr_p, qt, kn, krb, vt = _mla_prompt(mla, ct, sn, p, tm=min(512, seq // 2))
    attn = _attention_prompt(qt, kn, krb, vt, tq=min(512, seq))
    mixed = _merge(y_norm[0], attn, gates, p, tm=min(256, seq))
    y_prompt = _outproj(xp, mixed, p, tm=min(256, seq))

    rows_s = nb * t_new
    xs = x_sample.reshape(rows_s, D_MODEL)
    z_s, xbc_s, mla_s, gates_s = _inproj(xs, p["norm_in_w"], p["w_main"], p["w_tail"], tm=rows_s)
    y_norm_s, ssm_s, conv_s = _ssd(xbc_s.reshape(nb, t_new, CONV_DIM), z_s.reshape(nb, t_new, SSM_INNER),
                                   mla_s.reshape(nb, t_new, SEG_MLA), state_conv[0],
                                   state_ssm[0].reshape(nb, SSM_INNER, SSM_STATE), p, q_sample, nsub=1)
    ct_s, sn_s = _rope_tables(past + jnp.arange(t_new))
    ckv_s, kr_s, ql, qr = _mla_sample(mla_s, jnp.tile(ct_s, (nb, 1)), jnp.tile(sn_s, (nb, 1)), p)
    ckv_s = ckv_s.reshape(nb, t_new, KV_LORA)
    kr_s = kr_s.reshape(nb, t_new, QK_ROPE)
    attn_s = _attention_cached(ql, qr, cache_kv_latent[0], cache_k_rope[0], ckv_s, kr_s, p["w_uv"])
    mixed_s = _merge(y_norm_s.reshape(rows_s, SSM_INNER), attn_s.reshape(rows_s, MLA_INNER), gates_s, p, tm=rows_s)
    y_sample = _outproj(xs, mixed_s, p, tm=rows_s).reshape(nb, t_new, D_MODEL)

    hshape = (SSM_HEADS, SSM_HEAD_DIM, SSM_STATE)
    return (y_prompt[None], y_sample,
            ckv_p[None, None], kr_p[None, None], ssm_p.reshape((1, 1) + hshape), conv_p[None],
            ckv_s[None], kr_s[None], ssm_s.reshape((1, nb) + hshape), conv_s[None])
```

```python
import functools
import math

import jax
import jax.numpy as jnp
import numpy as np
from jax import lax
from jax.experimental import pallas as pl
from jax.experimental.pallas import tpu as pltpu

F32 = jnp.float32
BF16 = jnp.bfloat16

D_MODEL = 2048
CHUNK = 64
SSM_INNER = 4096
SSM_HEAD_DIM = 64
SSM_HEADS = 64
SSM_GROUPS = 8
SSM_STATE = 128
CONV_WIDTH = 4
CONV_DIM = SSM_INNER + 2 * SSM_GROUPS * SSM_STATE
GROUP_LANES = SSM_INNER // SSM_GROUPS
MLA_HEADS = 16
Q_LORA = 512
KV_LORA = 512
QK_NOPE = 128
QK_ROPE = 64
V_DIM = 128
MLA_INNER = MLA_HEADS * V_DIM
ROPE_THETA = 10000.0
EPS = 1e-6
QK_PAD = 256
SCALE = (QK_NOPE + QK_ROPE) ** -0.5
LOG2E = math.log2(math.e)
V_ROWS = V_DIM + 16
LANES = 128
VMEM_LIMIT = 56 * 1024 * 1024

IN_TN = 1024
SEG_MLA = 2048
COL_XBC = 0
COL_GATES = COL_XBC + CONV_DIM
COL_Z = COL_GATES + MLA_INNER + 2 * D_MODEL
COL_MLA = COL_Z + SSM_INNER
PROJ_COLS = COL_MLA + SEG_MLA
COL_SMALL = COL_MLA + Q_LORA + KV_LORA


def _cparams(sem, vmem=VMEM_LIMIT):
    return pltpu.CompilerParams(dimension_semantics=sem, vmem_limit_bytes=vmem)


def _const_spec(shape):
    nd = len(shape)
    return pl.BlockSpec(shape, lambda *_: (0,) * nd, pipeline_mode=pl.Buffered(1))


def _sigmoid(x):
    return 0.5 + 0.5 * jnp.tanh(0.5 * x)


def _silu(x):
    h = 0.5 * x
    return h + h * jnp.tanh(h)


def _rms(x, w):
    return x * lax.rsqrt(jnp.mean(x * x, axis=-1, keepdims=True) + EPS) * w


def _dot(a, b):
    return jnp.dot(a, b, preferred_element_type=F32)


def _dot_nt(a, b):
    return lax.dot_general(a, b, (((1,), (1,)), ((), ())), preferred_element_type=F32)


def _dot_tn(a, b):
    return lax.dot_general(a, b, (((0,), (0,)), ((), ())), preferred_element_type=F32)


def _split3(v):
    hi = v.astype(BF16)
    r1 = v - hi.astype(F32)
    mid = r1.astype(BF16)
    lo = (r1 - mid.astype(F32)).astype(BF16)
    return hi, mid, lo


def _inproj_kernel(x_ref, nw_ref, wm_ref, wt_ref, o_ref, h_ref, *, from_main):
    j = pl.program_id(1)

    @pl.when(j == 0)
    def _():
        h_ref[...] = _rms(x_ref[...], nw_ref[...]).astype(BF16)

    is_main = functools.reduce(jnp.logical_or, [(j >= lo) & (j < hi) for lo, hi in from_main])

    @pl.when(is_main)
    def _():
        o_ref[...] = _dot(h_ref[...], wm_ref[...])

    @pl.when(jnp.logical_not(is_main))
    def _():
        o_ref[...] = _dot(h_ref[...], wt_ref[...])


def _inproj(x, norm_w, w_main, w_tail, tm):
    rows = x.shape[0]
    t = lambda cols: cols // IN_TN
    n_xbc, n_g, n_z = t(CONV_DIM), t(MLA_INNER + 2 * D_MODEL), t(SSM_INNER)
    j_g, j_z, j_mla = t(COL_GATES), t(COL_Z), t(COL_MLA)
    src_xbc, src_z = t(SSM_INNER), 0

    def main_map(i, j):
        return 0, jnp.where(j < j_z, src_xbc + jnp.minimum(j, n_xbc - 1), src_z + jnp.minimum(j - j_z, n_z - 1))

    def tail_map(i, j):
        return 0, jnp.where(j < j_mla, jnp.clip(j - j_g, 0, n_g - 1), n_g + j - j_mla)

    return pl.pallas_call(
        functools.partial(_inproj_kernel, from_main=((0, n_xbc), (j_z, j_z + n_z))),
        grid=(rows // tm, t(PROJ_COLS)),
        in_specs=[
            pl.BlockSpec((tm, D_MODEL), lambda i, j: (i, 0)),
            pl.BlockSpec((1, D_MODEL), lambda i, j: (0, 0)),
            pl.BlockSpec((D_MODEL, IN_TN), main_map),
            pl.BlockSpec((D_MODEL, IN_TN), tail_map),
        ],
        out_specs=pl.BlockSpec((tm, IN_TN), lambda i, j: (i, j)),
        out_shape=jax.ShapeDtypeStruct((rows, PROJ_COLS), F32),
        scratch_shapes=[pltpu.VMEM((tm, D_MODEL), BF16)],
        compiler_params=_cparams(("arbitrary", "arbitrary")),
        name="inproj",
    )(x, norm_w, w_main, w_tail)


CONV_PAD = 8
CONV_SLAB = 256


def _ssd_kernel(xbc_ref, z_ref, sm_ref, cprev_ref, s0_ref, cw_ref, cb_ref, dtb_ref, alog_ref, dsk_ref,
                nw_ref, ep_ref, eq_ref, y_ref, sout_ref, cout_ref, st_ref, xpad_ref, xc_ref, exp_p_ref,
                exp_q_ref, yd_ref, *, q, nsub):
    c = pl.program_id(1)
    nc = pl.num_programs(1)
    rows = nsub * q
    hp = LANES // q
    tiles_per_group = (SSM_HEADS // SSM_GROUPS) // hp
    tile_w = hp * SSM_HEAD_DIM
    lo = CONV_PAD - (CONV_WIDTH - 1)

    @pl.when(c == 0)
    def _():
        st_ref[...] = s0_ref[0].T
        xpad_ref[0:lo, :] = jnp.zeros((lo, CONV_DIM), F32)
        xpad_ref[lo:CONV_PAD, :] = cprev_ref[0]

    assert CONV_WIDTH == 4
    xpad_ref[CONV_PAD:CONV_PAD + rows, :] = xbc_ref[0]
    for s in range(0, CONV_DIM, CONV_SLAB):
        xw = xpad_ref[:, s:s + CONV_SLAB]
        x1 = pltpu.roll(xw, 1, 0)
        w = [cw_ref[k:k + 1, s:s + CONV_SLAB] for k in range(CONV_WIDTH)]
        u = (xw * w[3] + x1 * w[2]) + pltpu.roll(xw * w[1] + x1 * w[0], 2, 0)
        xc_ref[:, s:s + CONV_SLAB] = _silu(u[CONV_PAD:, :] + cb_ref[:, s:s + CONV_SLAB])
    tail_rows = xpad_ref[CONV_PAD + rows - (CONV_WIDTH - 1):CONV_PAD + rows, :]
    xpad_ref[lo:CONV_PAD, :] = tail_rows

    ri = lax.broadcasted_iota(jnp.int32, (q, q), 0)
    ci = lax.broadcasted_iota(jnp.int32, (q, q), 1)
    tril = jnp.where(ci <= ri, 1.0, 0.0).astype(BF16)
    r128 = lax.broadcasted_iota(jnp.int32, (q, LANES), 0)
    c128 = lax.broadcasted_iota(jnp.int32, (q, LANES), 1)
    key128 = c128 & (q - 1)
    causal = key128 <= r128
    diag = key128 == r128
    br = lax.broadcasted_iota(jnp.int32, (LANES, tile_w), 0)
    bc = lax.broadcasted_iota(jnp.int32, (LANES, tile_w), 1)
    blockdiag = (br // q) == (bc // SSM_HEAD_DIM)
    neg_a = -jnp.exp(alog_ref[...])

    for u in range(nsub):
        r0 = u * q
        xdt = sm_ref[0, r0:r0 + q, 0:SSM_HEADS] + dtb_ref[...]
        dt = jnp.maximum(xdt, 0.0) + jnp.log1p(jnp.exp(-jnp.abs(xdt)))
        h3 = _split3(dt * neg_a)
        acs = _dot(tril, h3[0]) + _dot(tril, h3[1]) + _dot(tril, h3[2])

        def expand(v, e_ref):
            pieces = jnp.concatenate([piece.astype(F32) for piece in _split3(v)], axis=1).astype(BF16)
            return _dot(pieces, e_ref[...])

        exp_q_ref[u] = expand(jnp.concatenate([acs, dt], axis=0), eq_ref)
        exp_p_ref[u] = expand(jnp.concatenate([jnp.exp(acs), jnp.exp(acs[q - 1:q, :] - acs) * dt], axis=0), ep_ref)

        for g in range(SSM_GROUPS):
            gs = g * GROUP_LANES
            bs = SSM_INNER + g * SSM_STATE
            cs = SSM_INNER + (SSM_GROUPS + g) * SSM_STATE
            b_bf = xc_ref[r0:r0 + q, bs:bs + SSM_STATE].astype(BF16)
            c_bf = xc_ref[r0:r0 + q, cs:cs + SSM_STATE].astype(BF16)
            cb_t = _dot_nt(c_bf, jnp.concatenate([b_bf] * hp, axis=0))

            for tt in range(tiles_per_group):
                t = g * tiles_per_group + tt
                a_t = exp_q_ref[u, 0:q, t * LANES:(t + 1) * LANES]
                d_t = exp_q_ref[u, q:2 * q, t * LANES:(t + 1) * LANES]
                a_key = jnp.sum(jnp.where(diag, a_t, 0.0), axis=0, keepdims=True)
                d_key = jnp.sum(jnp.where(diag, d_t, 0.0), axis=0, keepdims=True)
                w_t = jnp.exp(jnp.where(causal, a_t - a_key, -jnp.inf)) * d_key * cb_t
                x_t = xc_ref[r0:r0 + q, t * tile_w:(t + 1) * tile_w]
                rhs = jnp.where(blockdiag, jnp.concatenate([x_t] * hp, axis=0), 0.0).astype(BF16)
                yd_ref[r0:r0 + q, t * tile_w:(t + 1) * tile_w] = _dot(w_t.astype(BF16), rhs)

            decay = exp_p_ref[u, 0:q, gs:gs + GROUP_LANES]
            tail = exp_p_ref[u, q:2 * q, gs:gs + GROUP_LANES]
            xs_g = xc_ref[r0:r0 + q, gs:gs + GROUP_LANES]
            st_g = st_ref[:, gs:gs + GROUP_LANES]
            y_off = _dot(c_bf, st_g.astype(BF16)) * decay
            xt = (xs_g * tail).astype(BF16)
            st_ref[:, gs:gs + GROUP_LANES] = st_g * decay[q - 1:q, :] + _dot_tn(b_bf, xt)

            y = yd_ref[r0:r0 + q, gs:gs + GROUP_LANES] + y_off + dsk_ref[:, gs:gs + GROUP_LANES] * xs_g
            y = y * _silu(z_ref[0, r0:r0 + q, gs:gs + GROUP_LANES])
            y = y * lax.rsqrt(jnp.mean(y * y, axis=-1, keepdims=True) + EPS)
            y_ref[0, r0:r0 + q, gs:gs + GROUP_LANES] = (y * nw_ref[:, gs:gs + GROUP_LANES]).astype(y_ref.dtype)

    @pl.when(c == nc - 1)
    def _():
        sout_ref[0] = st_ref[...].T
        cout_ref[0] = tail_rows


def _ssd(proj, conv_prev, s0, p, q, nsub):
    nb, lb = proj.shape[0], proj.shape[1]
    rows = q * nsub
    nc = lb // rows
    kq = SSM_HEADS * q
    in_specs = [
        pl.BlockSpec((1, rows, CONV_DIM), lambda b, c: (b, c, COL_XBC // CONV_DIM)),
        pl.BlockSpec((1, rows, SSM_INNER), lambda b, c: (b, c, COL_Z // SSM_INNER)),
        pl.BlockSpec((1, rows, 512), lambda b, c: (b, c, COL_SMALL // 512)),
        pl.BlockSpec((1, CONV_WIDTH - 1, CONV_DIM), lambda b, c: (b, 0, 0)),
        pl.BlockSpec((1, SSM_INNER, SSM_STATE), lambda b, c: (b, 0, 0)),
        _const_spec((CONV_WIDTH, CONV_DIM)),
        _const_spec((1, CONV_DIM)),
        _const_spec((1, SSM_HEADS)),
        _const_spec((1, SSM_HEADS)),
        _const_spec((1, SSM_INNER)),
        _const_spec((1, SSM_INNER)),
        _const_spec((3 * SSM_HEADS, SSM_INNER)),
        _const_spec((3 * SSM_HEADS, kq)),
    ]
    out_specs = [
        pl.BlockSpec((1, rows, SSM_INNER), lambda b, c: (b, c, 0)),
        pl.BlockSpec((1, SSM_INNER, SSM_STATE), lambda b, c: (b, 0, 0)),
        pl.BlockSpec((1, CONV_WIDTH - 1, CONV_DIM), lambda b, c: (b, 0, 0)),
    ]
    out_shape = [
        jax.ShapeDtypeStruct((nb, lb, SSM_INNER), BF16),
        jax.ShapeDtypeStruct((nb, SSM_INNER, SSM_STATE), F32),
        jax.ShapeDtypeStruct((nb, CONV_WIDTH - 1, CONV_DIM), F32),
    ]
    scratch = [
        pltpu.VMEM((SSM_STATE, SSM_INNER), F32),
        pltpu.VMEM((CONV_PAD + rows, CONV_DIM), F32),
        pltpu.VMEM((rows, CONV_DIM), F32),
        pltpu.VMEM((nsub, 2 * q, SSM_INNER), F32),
        pltpu.VMEM((nsub, 2 * q, kq), F32),
        pltpu.VMEM((rows, SSM_INNER), F32),
    ]
    return pl.pallas_call(
        functools.partial(_ssd_kernel, q=q, nsub=nsub),
        grid=(nb, nc),
        in_specs=in_specs,
        out_specs=out_specs,
        out_shape=out_shape,
        scratch_shapes=scratch,
        compiler_params=_cparams(("arbitrary", "arbitrary")),
        name="ssd",
    )(proj, proj, proj, conv_prev, s0, p["conv_w"], p["conv_b"], p["dt_bias"], p["a_log"], p["d_skip_p"],
      p["ssm_norm_w"], p["e_p"], p["e_q%d" % q])


def _mla_latents(mla_ref, ct_ref, sn_ref, qnw_ref, kvnw_ref, ckv_ref, kr_ref):
    m = mla_ref[...]
    cqn = _rms(m[:, 0:Q_LORA], qnw_ref[...])
    ckv = _rms(m[:, Q_LORA:Q_LORA + KV_LORA], kvnw_ref[...])
    ckv_ref[...] = ckv
    small = m[:, Q_LORA + KV_LORA:]
    ct, sn = ct_ref[...], sn_ref[...]
    kr128 = small[:, 128:256] * ct + small[:, 256:384] * sn
    kr_ref[...] = kr128[:, 0:QK_ROPE]
    return cqn, ckv, kr128, ct, sn


def _mla_prompt_kernel(mla_ref, ct_ref, sn_ref, ctt_ref, snt_ref, qnw_ref, kvnw_ref, wqat_ref, wqbt_ref, wk_ref,
                       wvt_ref, ckv_ref, kr_ref, qt_ref, kn_ref, krb_ref, vt_ref):
    cqn, ckv, kr128, _, _ = _mla_latents(mla_ref, ct_ref, sn_ref, qnw_ref, kvnw_ref, ckv_ref, kr_ref)
    cqn_t = cqn.T.astype(BF16)
    ckv_t = ckv.T.astype(BF16)
    ckv_bf = ckv.astype(BF16)
    krb_ref[...] = kr128.astype(BF16)
    ctt, snt = ctt_ref[...], snt_ref[...]
    tm = ckv_bf.shape[0]
    ones_row = jnp.where(lax.broadcasted_iota(jnp.int32, (V_ROWS - V_DIM, tm), 0) == 0, 1.0, 0.0).astype(BF16)
    qs = SCALE * LOG2E
    for h in range(MLA_HEADS):
        qa = _dot(wqat_ref[h * QK_PAD:(h + 1) * QK_PAD, :], cqn_t)
        qb = _dot(wqbt_ref[h * LANES:(h + 1) * LANES, :], cqn_t)
        qt_ref[h, 0:QK_NOPE, :] = (qa[0:QK_NOPE] * qs).astype(BF16)
        qt_ref[h, QK_NOPE:QK_PAD, :] = ((qa[QK_NOPE:QK_PAD] * ctt + qb * snt) * qs).astype(BF16)
        kn_ref[h] = _dot(ckv_bf, wk_ref[:, h * QK_NOPE:(h + 1) * QK_NOPE]).astype(BF16)
        vt_ref[h, 0, 0:V_DIM, :] = _dot(wvt_ref[h * V_DIM:(h + 1) * V_DIM, :], ckv_t).astype(BF16)
        vt_ref[h, 0, V_DIM:V_ROWS, :] = ones_row


def _mla_prompt(proj, ct, sn, p, tm):
    rows = proj.shape[0]
    row = lambda w: pl.BlockSpec((tm, w), lambda i: (i, 0))
    mla = pl.BlockSpec((tm, SEG_MLA), lambda i: (i, COL_MLA // SEG_MLA))
    col = pl.BlockSpec((LANES, tm), lambda i: (0, i))
    return pl.pallas_call(
        _mla_prompt_kernel,
        grid=(rows // tm,),
        in_specs=[mla, row(LANES), row(LANES), col, col, _const_spec((1, Q_LORA)),
                  _const_spec((1, KV_LORA)),
                  _const_spec((MLA_HEADS * QK_PAD, Q_LORA)), _const_spec((MLA_HEADS * LANES, Q_LORA)),
                  _const_spec((KV_LORA, MLA_HEADS * QK_NOPE)), _const_spec((MLA_INNER, KV_LORA))],
        out_specs=[row(KV_LORA), row(QK_ROPE),
                   pl.BlockSpec((MLA_HEADS, QK_PAD, tm), lambda i: (0, 0, i)),
                   pl.BlockSpec((MLA_HEADS, tm, QK_NOPE), lambda i: (0, i, 0)), row(LANES),
                   pl.BlockSpec((MLA_HEADS, 1, V_ROWS, tm), lambda i: (0, i, 0, 0))],
        out_shape=[jax.ShapeDtypeStruct((rows, KV_LORA), F32), jax.ShapeDtypeStruct((rows, QK_ROPE), F32),
                   jax.ShapeDtypeStruct((MLA_HEADS, QK_PAD, rows), BF16),
                   jax.ShapeDtypeStruct((MLA_HEADS, rows, QK_NOPE), BF16),
                   jax.ShapeDtypeStruct((rows, LANES), BF16),
                   jax.ShapeDtypeStruct((MLA_HEADS, rows // tm, V_ROWS, tm), BF16)],
        compiler_params=_cparams(("arbitrary",)),
        name="mla_prep_prompt",
    )(proj, ct, sn, ct.T, sn.T, p["q_norm_w"], p["kv_norm_w"], p["wq_a"].T, p["wq_b"].T, p["w_uk"], p["w_uv_t"])


def _mla_sample_kernel(mla_ref, ct_ref, sn_ref, qnw_ref, kvnw_ref, wqa_ref, wqb_ref, wk_ref,
                       ckv_ref, kr_ref, ql_ref, qr_ref):
    cqn, _, _, ct, sn = _mla_latents(mla_ref, ct_ref, sn_ref, qnw_ref, kvnw_ref, ckv_ref, kr_ref)
    cqn = cqn.astype(BF16)
    qa = _dot(cqn, wqa_ref[...])
    qb = _dot(cqn, wqb_ref[...])
    for h in range(MLA_HEADS):
        o = h * QK_PAD
        q_rope = qa[:, o + QK_NOPE:o + QK_PAD] * ct + qb[:, h * LANES:(h + 1) * LANES] * sn
        qr_ref[h] = (q_rope * SCALE).astype(BF16)
        q_nope = qa[:, o:o + QK_NOPE].astype(BF16)
        q_lat = _dot_nt(q_nope, wk_ref[:, h * QK_NOPE:(h + 1) * QK_NOPE])
        ql_ref[h] = (q_lat * SCALE).astype(BF16)


def _mla_sample(proj, ct, sn, p):
    rows = proj.shape[0]
    full = lambda w: pl.BlockSpec((rows, w), lambda i: (0, 0))
    mla = pl.BlockSpec((rows, SEG_MLA), lambda i: (0, COL_MLA // SEG_MLA))
    return pl.pallas_call(
        _mla_sample_kernel,
        grid=(1,),
        in_specs=[mla, full(LANES), full(LANES), _const_spec((1, Q_LORA)), _const_spec((1, KV_LORA)),
                  _const_spec((Q_LORA, MLA_HEADS * QK_PAD)), _const_spec((Q_LORA, MLA_HEADS * LANES)),
                  _const_spec((KV_LORA, MLA_HEADS * QK_NOPE))],
        out_specs=[full(KV_LORA), full(QK_ROPE),
                   pl.BlockSpec((MLA_HEADS, rows, KV_LORA), lambda i: (0, 0, 0)),
                   pl.BlockSpec((MLA_HEADS, rows, LANES), lambda i: (0, 0, 0))],
        out_shape=[jax.ShapeDtypeStruct((rows, KV_LORA), F32), jax.ShapeDtypeStruct((rows, QK_ROPE), F32),
                   jax.ShapeDtypeStruct((MLA_HEADS, rows, KV_LORA), BF16),
                   jax.ShapeDtypeStruct((MLA_HEADS, rows, LANES), BF16)],
        compiler_params=_cparams(("arbitrary",)),
        name="mla_prep_sample",
    )(proj, ct, sn, p["q_norm_w"], p["kv_norm_w"], p["wq_a"], p["wq_b"], p["w_uk"])


def _attn_kernel(qi_ref, kj_ref, qt_ref, kn_ref, kr_ref, vt_ref, o_ref, m_ref, acc_ref, s0_ref, s1_ref, p0_ref,
                 p1_ref, a0_ref, a1_ref, *, tq, tks):
    s_refs, p_refs, a_refs = (s0_ref, s1_ref), (p0_ref, p1_ref), (a0_ref, a1_ref)
    n_sub = len(s_refs)
    t = pl.program_id(0)
    i = qi_ref[t]
    j = kj_ref[t]
    tk = n_sub * tks
    last_j = ((i + 1) * tq - 1) // tk

    @pl.when(j == 0)
    def _():
        m_ref[...] = jnp.full(m_ref.shape, -jnp.inf, F32)
        acc_ref[...] = jnp.zeros(acc_ref.shape, F32)

    def step(masked):
        if masked:
            q_chunk = (i * tq + lax.broadcasted_iota(jnp.int32, (tks, tq), 1)) // CHUNK
            k_chunk = (j * tk + lax.broadcasted_iota(jnp.int32, (tks, tq), 0)) // CHUNK
            visible = [k_chunk + (c * tks) // CHUNK <= q_chunk for c in range(n_sub)]

        def scores(h, c):
            k_h = jnp.concatenate([kn_ref[h, c * tks:(c + 1) * tks, :], kr_ref[c * tks:(c + 1) * tks, :]], axis=1)
            s_refs[c][...] = _dot(k_h, qt_ref[h])

        def softmax(h, c):
            s = s_refs[c][...]
            if masked:
                s = jnp.where(visible[c], s, -jnp.inf)
            m_prev = m_ref[h]
            m_new = jnp.maximum(m_prev, jnp.max(s, axis=0, keepdims=True))
            m_ref[h] = m_new
            p_refs[c][...] = jnp.exp2(s - m_new).astype(BF16)
            a_refs[c][...] = jnp.exp2(m_prev - m_new)

        def values(h, c):
            acc_ref[h] = acc_ref[h] * a_refs[c][...] + _dot(vt_ref[h, c], p_refs[c][...])

        scores(0, 0)
        scores(0, 1)
        softmax(0, 0)

        for h in range(1, MLA_HEADS):
            scores(h, 0)
            softmax(h - 1, 1)
            values(h - 1, 0)
            scores(h, 1)
            softmax(h, 0)
            values(h - 1, 1)
        softmax(MLA_HEADS - 1, 1)
        values(MLA_HEADS - 1, 0)
        values(MLA_HEADS - 1, 1)

    @pl.when(j < last_j)
    def _():
        step(False)

    @pl.when(j == last_j)
    def _():
        step(True)
        for h in range(MLA_HEADS):
            den = acc_ref[h, V_DIM:V_DIM + 1, :]
            o_ref[:, h * V_DIM:(h + 1) * V_DIM] = (acc_ref[h, 0:V_DIM, :] * (1.0 / den)).T


def _attention_prompt(qt, kn, kr, vt, tq):
    rows = kn.shape[1]
    n_sub, tks = 2, vt.shape[3]
    tk = tks * n_sub
    qi, kj = [], []
    for i in range(rows // tq):
        for j in range(((i + 1) * tq - 1) // tk + 1):
            qi.append(i)
            kj.append(j)
    qi = jnp.asarray(np.array(qi, np.int32))
    kj = jnp.asarray(np.array(kj, np.int32))
    grid_spec = pltpu.PrefetchScalarGridSpec(
        num_scalar_prefetch=2,
        grid=(int(qi.shape[0]),),
        in_specs=[
            pl.BlockSpec((MLA_HEADS, QK_PAD, tq), lambda t, qi, kj: (0, 0, qi[t])),
            pl.BlockSpec((MLA_HEADS, tk, QK_NOPE), lambda t, qi, kj: (0, kj[t], 0)),
            pl.BlockSpec((tk, LANES), lambda t, qi, kj: (kj[t], 0)),
            pl.BlockSpec((MLA_HEADS, n_sub, V_ROWS, tks), lambda t, qi, kj: (0, kj[t], 0, 0)),
        ],
        out_specs=pl.BlockSpec((tq, MLA_INNER), lambda t, qi, kj: (qi[t], 0)),
        scratch_shapes=[
            pltpu.VMEM((MLA_HEADS, 1, tq), F32),
            pltpu.VMEM((MLA_HEADS, V_ROWS, tq), F32),
            pltpu.VMEM((tks, tq), F32), pltpu.VMEM((tks, tq), F32),
            pltpu.VMEM((tks, tq), BF16), pltpu.VMEM((tks, tq), BF16),
            pltpu.VMEM((1, tq), F32), pltpu.VMEM((1, tq), F32),
        ],
    )
    return pl.pallas_call(
        functools.partial(_attn_kernel, tq=tq, tks=tks),
        grid_spec=grid_spec,
        out_shape=jax.ShapeDtypeStruct((rows, MLA_INNER), F32),
        compiler_params=_cparams(("arbitrary",)),
        name="attn_prompt",
    )(qi, kj, qt, kn, kr, vt)


def _attn_cached_kernel(ql_ref, qr_ref, ckv_c_ref, kr_c_ref, ckv_n_ref, kr_n_ref, wv_ref, o_ref, *, t_new):
    rows = MLA_HEADS * t_new
    ql = ql_ref[...].reshape(rows, KV_LORA)
    qr = qr_ref[...].reshape(rows, LANES)[:, 0:QK_ROPE]
    kc = ckv_c_ref[0].astype(BF16)
    kn = ckv_n_ref[0].astype(BF16)
    s_c = _dot_nt(ql, kc) + _dot_nt(qr, kr_c_ref[0].astype(BF16))
    s_n = _dot_nt(ql, kn) + _dot_nt(qr, kr_n_ref[0].astype(BF16))
    m = jnp.maximum(jnp.max(s_c, axis=1, keepdims=True), jnp.max(s_n, axis=1, keepdims=True))
    p_c = jnp.exp(s_c - m)
    p_n = jnp.exp(s_n - m)
    den = jnp.sum(p_c, axis=1, keepdims=True) + jnp.sum(p_n, axis=1, keepdims=True)
    o_lat = (_dot(p_c.astype(BF16), kc) + _dot(p_n.astype(BF16), kn)) / den
    o_bf = o_lat.astype(BF16)
    for h in range(MLA_HEADS):
        o_ref[0, :, h * V_DIM:(h + 1) * V_DIM] = _dot(o_bf[h * t_new:(h + 1) * t_new, :],
                                                        wv_ref[:, h * V_DIM:(h + 1) * V_DIM])


def _attention_cached(ql, qr, cache_kv, cache_kr, ckv_new, kr_new, w_uv):
    nb, past = cache_kv.shape[0], cache_kv.shape[1]
    t_new = ckv_new.shape[1]
    return pl.pallas_call(
        functools.partial(_attn_cached_kernel, t_new=t_new),
        grid=(nb,),
        in_specs=[
            pl.BlockSpec((MLA_HEADS, t_new, KV_LORA), lambda b: (0, b, 0)),
            pl.BlockSpec((MLA_HEADS, t_new, LANES), lambda b: (0, b, 0)),
            pl.BlockSpec((1, past, KV_LORA), lambda b: (b, 0, 0)),
            pl.BlockSpec((1, past, QK_ROPE), lambda b: (b, 0, 0)),
            pl.BlockSpec((1, t_new, KV_LORA), lambda b: (b, 0, 0)),
            pl.BlockSpec((1, t_new, QK_ROPE), lambda b: (b, 0, 0)),
            _const_spec((KV_LORA, MLA_INNER)),
        ],
        out_specs=pl.BlockSpec((1, t_new, MLA_INNER), lambda b: (b, 0, 0)),
        out_shape=jax.ShapeDtypeStruct((nb, t_new, MLA_INNER), F32),
        compiler_params=_cparams(("arbitrary",)),
        name="attn_cached",
    )(ql, qr, cache_kv, cache_kr, ckv_new, kr_new, w_uv)


def _merge_kernel(y_ref, attn_ref, gm_ref, gs_ref, ga_ref, wssm_ref, wmla_ref, o_ref):
    y_ssm = _dot(y_ref[...], wssm_ref[...])
    y_mla = _dot((attn_ref[...] * _silu(gm_ref[...])).astype(BF16), wmla_ref[...])
    o_ref[...] = (_sigmoid(gs_ref[...]) * y_ssm + _sigmoid(ga_ref[...]) * y_mla).astype(o_ref.dtype)


def _merge(y_norm, attn, proj, p, tm):
    rows = y_norm.shape[0]
    gate = lambda k: pl.BlockSpec((tm, D_MODEL), lambda i: (i, COL_GATES // D_MODEL + k))
    return pl.pallas_call(
        _merge_kernel,
        grid=(rows // tm,),
        in_specs=[pl.BlockSpec((tm, SSM_INNER), lambda i: (i, 0)), pl.BlockSpec((tm, MLA_INNER), lambda i: (i, 0)),
                  gate(0), gate(1), gate(2),
                  _const_spec((SSM_INNER, D_MODEL)), _const_spec((MLA_INNER, D_MODEL))],
        out_specs=pl.BlockSpec((tm, D_MODEL), lambda i: (i, 0)),
        out_shape=jax.ShapeDtypeStruct((rows, D_MODEL), BF16),
        compiler_params=_cparams(("arbitrary",)),
        name="merge",
    )(y_norm, attn, proj, proj, proj, p["w_ssm_out"], p["w_mla_out"])


def _outproj_kernel(x_ref, mixed_ref, w_ref, nw_ref, o_ref):
    x_out = x_ref[...] + _dot(mixed_ref[...], w_ref[...])
    o_ref[...] = _rms(x_out, nw_ref[...])


def _outproj(x, mixed, p, tm):
    rows = x.shape[0]
    row = pl.BlockSpec((tm, D_MODEL), lambda i: (i, 0))
    return pl.pallas_call(
        _outproj_kernel,
        grid=(rows // tm,),
        in_specs=[row, row, _const_spec((D_MODEL, D_MODEL)), _const_spec((1, D_MODEL))],
        out_specs=row,
        out_shape=jax.ShapeDtypeStruct((rows, D_MODEL), F32),
        compiler_params=_cparams(("arbitrary",)),
        name="outproj",
    )(x, mixed, p["w_out"], p["final_norm_w"])


def _prepare(norm_in_w, w_in, conv_w, conv_b, dt_bias, a_log, d_skip, ssm_norm_w, w_ssm_out, q_norm_w, w_q_up,
             kv_norm_w, w_kv_up, w_mla_out, w_out, final_norm_w, chunk_lens):
    offs = np.cumsum((0, SSM_INNER, CONV_DIM, SSM_HEADS, Q_LORA, KV_LORA, QK_ROPE, MLA_INNER, D_MODEL, D_MODEL))
    w_in_bf = w_in.astype(BF16)
    col = lambda k: w_in_bf[:, int(offs[k]):int(offs[k + 1])]
    half = QK_ROPE // 2
    kr_w = col(5)
    zeros = lambda n: jnp.zeros((D_MODEL, n), BF16)
    w_tail = jnp.concatenate([
        col(6), col(7), col(8),
        col(3), col(4),
        col(2), zeros(64), kr_w, zeros(64), kr_w[:, half:], kr_w[:, :half], zeros(64 + 128),
        zeros(SEG_MLA - Q_LORA - KV_LORA - 512)], axis=1)

    wq = w_q_up.reshape(Q_LORA, MLA_HEADS, QK_NOPE + QK_ROPE)
    wq_rope = wq[..., QK_NOPE:]
    zq = jnp.zeros((Q_LORA, MLA_HEADS, 64), w_q_up.dtype)
    wq_a = jnp.concatenate([wq, zq], axis=-1).reshape(Q_LORA, MLA_HEADS * QK_PAD).astype(BF16)
    wq_b = jnp.concatenate([wq_rope[..., half:], wq_rope[..., :half], zq], axis=-1)
    wq_b = wq_b.reshape(Q_LORA, MLA_HEADS * LANES).astype(BF16)

    wkv = w_kv_up.reshape(KV_LORA, MLA_HEADS, QK_NOPE + V_DIM)
    w_uk = wkv[..., :QK_NOPE].reshape(KV_LORA, MLA_HEADS * QK_NOPE).astype(BF16)
    w_uv_h = wkv[..., QK_NOPE:]
    w_uv = w_uv_h.reshape(KV_LORA, MLA_INNER).astype(BF16)
    w_uv_t = w_uv.T

    head_of_lane = np.arange(SSM_INNER) // SSM_HEAD_DIM
    p = dict(
        norm_in_w=norm_in_w.reshape(1, D_MODEL), w_main=w_in_bf, w_tail=w_tail,
        conv_w=conv_w, conv_b=conv_b.reshape(1, CONV_DIM), dt_bias=dt_bias.reshape(1, SSM_HEADS),
        a_log=a_log.reshape(1, SSM_HEADS), d_skip_p=jnp.repeat(d_skip, SSM_HEAD_DIM).reshape(1, SSM_INNER),
        ssm_norm_w=ssm_norm_w.reshape(1, SSM_INNER), w_ssm_out=w_ssm_out.astype(BF16),
        q_norm_w=q_norm_w.reshape(1, Q_LORA), kv_norm_w=kv_norm_w.reshape(1, KV_LORA),
        wq_a=wq_a, wq_b=wq_b, w_uk=w_uk, w_uv=w_uv, w_uv_t=w_uv_t,
        w_mla_out=w_mla_out.astype(BF16), w_out=w_out.astype(BF16),
        final_norm_w=final_norm_w.reshape(1, D_MODEL),
        e_p=jnp.asarray(np.tile(head_of_lane[None, :] == np.arange(SSM_HEADS)[:, None], (3, 1)), BF16),
    )
    for q in chunk_lens:
        head_of_key_lane = np.arange(SSM_HEADS * q) // q
        p["e_q%d" % q] = jnp.asarray(np.tile(head_of_key_lane[None, :] == np.arange(SSM_HEADS)[:, None], (3, 1)),
                                     BF16)
    return p


def _rope_tables(pos):
    half = QK_ROPE // 2
    inv = 1.0 / (ROPE_THETA ** (jnp.arange(half, dtype=F32) * (2.0 / QK_ROPE)))
    ang = pos.astype(F32)[:, None] * inv[None, :]
    cos, sin = jnp.cos(ang), jnp.sin(ang)
    pad = jnp.zeros((pos.shape[0], LANES - QK_ROPE), F32)
    return jnp.concatenate([cos, cos, pad], axis=1), jnp.concatenate([-sin, sin, pad], axis=1)


def kernel(x_prompt, x_sample, cache_kv_latent, cache_k_rope, state_ssm, state_conv, norm_in_w, w_in, conv_w,
           conv_b, dt_bias, a_log, d_skip, ssm_norm_w, w_ssm_out, q_norm_w, w_q_up, kv_norm_w, w_kv_up, w_mla_out,
           w_out, final_norm_w):
    depth = w_in.shape[0]
    assert depth == 1 and x_prompt.shape[0] == 1
    seq = x_prompt.shape[1]
    nb, t_new = x_sample.shape[0], x_sample.shape[1]
    past = cache_kv_latent.shape[2]
    q_prompt, q_sample = min(CHUNK, seq), min(CHUNK, t_new)
    p = _prepare(norm_in_w[0], w_in[0], conv_w[0], conv_b[0], dt_bias[0], a_log[0], d_skip[0], ssm_norm_w[0],
                 w_ssm_out[0], q_norm_w[0], w_q_up[0], kv_norm_w[0], w_kv_up[0], w_mla_out[0], w_out[0],
                 final_norm_w, sorted({q_prompt, q_sample}))

    xp = x_prompt[0]
    proj = _inproj(xp, p["norm_in_w"], p["w_main"], p["w_tail"], tm=min(1024, seq))
    y_norm, ssm_p, conv_p = _ssd(proj[None],
                                 jnp.zeros((1, CONV_WIDTH - 1, CONV_DIM), F32),
                                 jnp.zeros((1, SSM_INNER, SSM_STATE), F32), p, q_prompt,
                                 nsub=4 if seq % (4 * q_prompt) == 0 else 1)
    ct, sn = _rope_tables(jnp.arange(seq))
    ckv_p, kr_p, qt, kn, krb, vt = _mla_prompt(proj, ct, sn, p, tm=min(512, seq // 2))
    attn = _attention_prompt(qt, kn, krb, vt, tq=min(512, seq))
    mixed = _merge(y_norm[0], attn, proj, p, tm=min(256, seq))
    y_prompt = _outproj(xp, mixed, p, tm=min(256, seq))

    rows_s = nb * t_new
    xs = x_sample.reshape(rows_s, D_MODEL)
    proj_s = _inproj(xs, p["norm_in_w"], p["w_main"], p["w_tail"], tm=rows_s)
    y_norm_s, ssm_s, conv_s = _ssd(proj_s.reshape(nb, t_new, PROJ_COLS), state_conv[0],
                                   state_ssm[0].reshape(nb, SSM_INNER, SSM_STATE), p, q_sample, nsub=1)
    ct_s, sn_s = _rope_tables(past + jnp.arange(t_new))
    ckv_s, kr_s, ql, qr = _mla_sample(proj_s, jnp.tile(ct_s, (nb, 1)), jnp.tile(sn_s, (nb, 1)), p)
    ckv_s = ckv_s.reshape(nb, t_new, KV_LORA)
    kr_s = kr_s.reshape(nb, t_new, QK_ROPE)
    attn_s = _attention_cached(ql, qr, cache_kv_latent[0], cache_k_rope[0], ckv_s, kr_s, p["w_uv"])
    mixed_s = _merge(y_norm_s.reshape(rows_s, SSM_INNER), attn_s.reshape(rows_s, MLA_INNER), proj_s, p, tm=rows_s)
    y_sample = _outproj(xs, mixed_s, p, tm=rows_s).reshape(nb, t_new, D_MODEL)

    hshape = (SSM_HEADS, SSM_HEAD_DIM, SSM_STATE)
    return (y_prompt[None], y_sample,
            ckv_p[None, None], kr_p[None, None], ssm_p.reshape((1, 1) + hshape), conv_p[None],
            ckv_s[None], kr_s[None], ssm_s.reshape((1, nb) + hshape), conv_s[None])
```

```python
import functools
import math

import jax
import jax.numpy as jnp
import numpy as np
from jax import lax
from jax.experimental import pallas as pl
from jax.experimental.pallas import tpu as pltpu

F32 = jnp.float32
BF16 = jnp.bfloat16

D_MODEL = 2048
CHUNK = 64
SSM_INNER = 4096
SSM_HEAD_DIM = 64
SSM_HEADS = 64
SSM_GROUPS = 8
SSM_STATE = 128
CONV_WIDTH = 4
CONV_DIM = SSM_INNER + 2 * SSM_GROUPS * SSM_STATE
GROUP_LANES = SSM_INNER // SSM_GROUPS
MLA_HEADS = 16
Q_LORA = 512
KV_LORA = 512
QK_NOPE = 128
QK_ROPE = 64
V_DIM = 128
MLA_INNER = MLA_HEADS * V_DIM
ROPE_THETA = 10000.0
EPS = 1e-6
QK_PAD = 256
SCALE = (QK_NOPE + QK_ROPE) ** -0.5
LOG2E = math.log2(math.e)
V_ROWS = V_DIM + 16
LANES = 128
VMEM_LIMIT = 56 * 1024 * 1024

IN_TN = 1024
SEG_MLA = 2048
COL_XBC = 0
COL_GATES = COL_XBC + CONV_DIM
COL_Z = COL_GATES + MLA_INNER + 2 * D_MODEL
COL_MLA = COL_Z + SSM_INNER
PROJ_COLS = COL_MLA + SEG_MLA
COL_SMALL = COL_MLA + Q_LORA + KV_LORA


def _cparams(sem, vmem=VMEM_LIMIT):
    return pltpu.CompilerParams(dimension_semantics=sem, vmem_limit_bytes=vmem)


def _const_spec(shape):
    nd = len(shape)
    return pl.BlockSpec(shape, lambda *_: (0,) * nd, pipeline_mode=pl.Buffered(1))


def _sigmoid(x):
    return 0.5 + 0.5 * jnp.tanh(0.5 * x)


def _silu(x):
    h = 0.5 * x
    return h + h * jnp.tanh(h)


def _rms(x, w):
    return x * lax.rsqrt(jnp.mean(x * x, axis=-1, keepdims=True) + EPS) * w


def _dot(a, b):
    return jnp.dot(a, b, preferred_element_type=F32)


def _dot_nt(a, b):
    return lax.dot_general(a, b, (((1,), (1,)), ((), ())), preferred_element_type=F32)


def _dot_tn(a, b):
    return lax.dot_general(a, b, (((0,), (0,)), ((), ())), preferred_element_type=F32)


def _split3(v):
    hi = v.astype(BF16)
    r1 = v - hi.astype(F32)
    mid = r1.astype(BF16)
    lo = (r1 - mid.astype(F32)).astype(BF16)
    return hi, mid, lo


def _inproj_kernel(x_ref, nw_ref, wm_ref, wt_ref, o_ref, h_ref, *, from_main):
    j = pl.program_id(1)

    @pl.when(j == 0)
    def _():
        h_ref[...] = _rms(x_ref[...], nw_ref[...]).astype(BF16)

    is_main = functools.reduce(jnp.logical_or, [(j >= lo) & (j < hi) for lo, hi in from_main])

    @pl.when(is_main)
    def _():
        o_ref[...] = _dot(h_ref[...], wm_ref[...])

    @pl.when(jnp.logical_not(is_main))
    def _():
        o_ref[...] = _dot(h_ref[...], wt_ref[...])


def _inproj(x, norm_w, w_main, w_tail, tm):
    rows = x.shape[0]
    t = lambda cols: cols // IN_TN
    n_xbc, n_g, n_z = t(CONV_DIM), t(MLA_INNER + 2 * D_MODEL), t(SSM_INNER)
    j_g, j_z, j_mla = t(COL_GATES), t(COL_Z), t(COL_MLA)
    src_xbc, src_z = t(SSM_INNER), 0

    def main_map(i, j):
        return 0, jnp.where(j < j_z, src_xbc + jnp.minimum(j, n_xbc - 1), src_z + jnp.minimum(j - j_z, n_z - 1))

    def tail_map(i, j):
        return 0, jnp.where(j < j_mla, jnp.clip(j - j_g, 0, n_g - 1), n_g + j - j_mla)

    return pl.pallas_call(
        functools.partial(_inproj_kernel, from_main=((0, n_xbc), (j_z, j_z + n_z))),
        grid=(rows // tm, t(PROJ_COLS)),
        in_specs=[
            pl.BlockSpec((tm, D_MODEL), lambda i, j: (i, 0)),
            pl.BlockSpec((1, D_MODEL), lambda i, j: (0, 0)),
            pl.BlockSpec((D_MODEL, IN_TN), main_map),
            pl.BlockSpec((D_MODEL, IN_TN), tail_map),
        ],
        out_specs=pl.BlockSpec((tm, IN_TN), lambda i, j: (i, j)),
        out_shape=jax.ShapeDtypeStruct((rows, PROJ_COLS), F32),
        scratch_shapes=[pltpu.VMEM((tm, D_MODEL), BF16)],
        compiler_params=_cparams(("arbitrary", "arbitrary")),
        name="inproj",
    )(x, norm_w, w_main, w_tail)


CONV_PAD = 8
CONV_SLAB = 256


def _ssd_kernel(xbc_ref, z_ref, sm_ref, cprev_ref, s0_ref, cw_ref, cb_ref, dtb_ref, alog_ref, dsk_ref,
                nw_ref, ep_ref, eq_ref, y_ref, sout_ref, cout_ref, st_ref, xpad_ref, xc_ref, exp_p_ref,
                exp_q_ref, yd_ref, *, q, nsub):
    c = pl.program_id(1)
    nc = pl.num_programs(1)
    rows = nsub * q
    hp = LANES // q
    tiles_per_group = (SSM_HEADS // SSM_GROUPS) // hp
    tile_w = hp * SSM_HEAD_DIM
    lo = CONV_PAD - (CONV_WIDTH - 1)

    @pl.when(c == 0)
    def _():
        st_ref[...] = s0_ref[0].T
        xpad_ref[0:lo, :] = jnp.zeros((lo, CONV_DIM), F32)
        xpad_ref[lo:CONV_PAD, :] = cprev_ref[0]

    assert CONV_WIDTH == 4
    xpad_ref[CONV_PAD:CONV_PAD + rows, :] = xbc_ref[0]
    for s in range(0, CONV_DIM, CONV_SLAB):
        xw = xpad_ref[:, s:s + CONV_SLAB]
        x1 = pltpu.roll(xw, 1, 0)
        w = [cw_ref[k:k + 1, s:s + CONV_SLAB] for k in range(CONV_WIDTH)]
        u = (xw * w[3] + x1 * w[2]) + pltpu.roll(xw * w[1] + x1 * w[0], 2, 0)
        xc_ref[:, s:s + CONV_SLAB] = _silu(u[CONV_PAD:, :] + cb_ref[:, s:s + CONV_SLAB])
    tail_rows = xpad_ref[CONV_PAD + rows - (CONV_WIDTH - 1):CONV_PAD + rows, :]
    xpad_ref[lo:CONV_PAD, :] = tail_rows

    ri = lax.broadcasted_iota(jnp.int32, (q, q), 0)
    ci = lax.broadcasted_iota(jnp.int32, (q, q), 1)
    tril = jnp.where(ci <= ri, 1.0, 0.0).astype(BF16)
    r128 = lax.broadcasted_iota(jnp.int32, (q, LANES), 0)
    c128 = lax.broadcasted_iota(jnp.int32, (q, LANES), 1)
    key128 = c128 & (q - 1)
    causal = key128 <= r128
    diag = key128 == r128
    br = lax.broadcasted_iota(jnp.int32, (LANES, tile_w), 0)
    bc = lax.broadcasted_iota(jnp.int32, (LANES, tile_w), 1)
    blockdiag = (br // q) == (bc // SSM_HEAD_DIM)
    neg_a = -jnp.exp(alog_ref[...])

    for u in range(nsub):
        r0 = u * q
        xdt = sm_ref[0, r0:r0 + q, 0:SSM_HEADS] + dtb_ref[...]
        dt = jnp.maximum(xdt, 0.0) + jnp.log1p(jnp.exp(-jnp.abs(xdt)))
        h3 = _split3(dt * neg_a)
        acs = _dot(tril, h3[0]) + _dot(tril, h3[1]) + _dot(tril, h3[2])

        def expand(v, e_ref):
            pieces = jnp.concatenate([piece.astype(F32) for piece in _split3(v)], axis=1).astype(BF16)
            return _dot(pieces, e_ref[...])

        exp_q_ref[u] = expand(jnp.concatenate([acs, dt], axis=0), eq_ref)
        exp_p_ref[u] = expand(jnp.concatenate([jnp.exp(acs), jnp.exp(acs[q - 1:q, :] - acs) * dt], axis=0), ep_ref)

        for g in range(SSM_GROUPS):
            gs = g * GROUP_LANES
            bs = SSM_INNER + g * SSM_STATE
            cs = SSM_INNER + (SSM_GROUPS + g) * SSM_STATE
            b_bf = xc_ref[r0:r0 + q, bs:bs + SSM_STATE].astype(BF16)
            c_bf = xc_ref[r0:r0 + q, cs:cs + SSM_STATE].astype(BF16)
            cb_t = _dot_nt(c_bf, jnp.concatenate([b_bf] * hp, axis=0))

            for tt in range(tiles_per_group):
                t = g * tiles_per_group + tt
                a_t = exp_q_ref[u, 0:q, t * LANES:(t + 1) * LANES]
                d_t = exp_q_ref[u, q:2 * q, t * LANES:(t + 1) * LANES]
                a_key = jnp.sum(jnp.where(diag, a_t, 0.0), axis=0, keepdims=True)
                d_key = jnp.sum(jnp.where(diag, d_t, 0.0), axis=0, keepdims=True)
                w_t = jnp.exp(jnp.where(causal, a_t - a_key, -jnp.inf)) * d_key * cb_t
                x_t = xc_ref[r0:r0 + q, t * tile_w:(t + 1) * tile_w]
                rhs = jnp.where(blockdiag, jnp.concatenate([x_t] * hp, axis=0), 0.0).astype(BF16)
                yd_ref[r0:r0 + q, t * tile_w:(t + 1) * tile_w] = _dot(w_t.astype(BF16), rhs)

            decay = exp_p_ref[u, 0:q, gs:gs + GROUP_LANES]
            tail = exp_p_ref[u, q:2 * q, gs:gs + GROUP_LANES]
            xs_g = xc_ref[r0:r0 + q, gs:gs + GROUP_LANES]
            st_g = st_ref[:, gs:gs + GROUP_LANES]
            y_off = _dot(c_bf, st_g.astype(BF16)) * decay
            xt = (xs_g * tail).astype(BF16)
            st_ref[:, gs:gs + GROUP_LANES] = st_g * decay[q - 1:q, :] + _dot_tn(b_bf, xt)

            y = yd_ref[r0:r0 + q, gs:gs + GROUP_LANES] + y_off + dsk_ref[:, gs:gs + GROUP_LANES] * xs_g
            y = y * _silu(z_ref[0, r0:r0 + q, gs:gs + GROUP_LANES])
            y = y * lax.rsqrt(jnp.mean(y * y, axis=-1, keepdims=True) + EPS)
            y_ref[0, r0:r0 + q, gs:gs + GROUP_LANES] = (y * nw_ref[:, gs:gs + GROUP_LANES]).astype(y_ref.dtype)

    @pl.when(c == nc - 1)
    def _():
        sout_ref[0] = st_ref[...].T
        cout_ref[0] = tail_rows


def _ssd(proj, conv_prev, s0, p, q, nsub):
    nb, lb = proj.shape[0], proj.shape[1]
    rows = q * nsub
    nc = lb // rows
    kq = SSM_HEADS * q
    in_specs = [
        pl.BlockSpec((1, rows, CONV_DIM), lambda b, c: (b, c, COL_XBC // CONV_DIM)),
        pl.BlockSpec((1, rows, SSM_INNER), lambda b, c: (b, c, COL_Z // SSM_INNER)),
        pl.BlockSpec((1, rows, 512), lambda b, c: (b, c, COL_SMALL // 512)),
        pl.BlockSpec((1, CONV_WIDTH - 1, CONV_DIM), lambda b, c: (b, 0, 0)),
        pl.BlockSpec((1, SSM_INNER, SSM_STATE), lambda b, c: (b, 0, 0)),
        _const_spec((CONV_WIDTH, CONV_DIM)),
        _const_spec((1, CONV_DIM)),
        _const_spec((1, SSM_HEADS)),
        _const_spec((1, SSM_HEADS)),
        _const_spec((1, SSM_INNER)),
        _const_spec((1, SSM_INNER)),
        _const_spec((3 * SSM_HEADS, SSM_INNER)),
        _const_spec((3 * SSM_HEADS, kq)),
    ]
    out_specs = [
        pl.BlockSpec((1, rows, SSM_INNER), lambda b, c: (b, c, 0)),
        pl.BlockSpec((1, SSM_INNER, SSM_STATE), lambda b, c: (b, 0, 0)),
        pl.BlockSpec((1, CONV_WIDTH - 1, CONV_DIM), lambda b, c: (b, 0, 0)),
    ]
    out_shape = [
        jax.ShapeDtypeStruct((nb, lb, SSM_INNER), BF16),
        jax.ShapeDtypeStruct((nb, SSM_INNER, SSM_STATE), F32),
        jax.ShapeDtypeStruct((nb, CONV_WIDTH - 1, CONV_DIM), F32),
    ]
    scratch = [
        pltpu.VMEM((SSM_STATE, SSM_INNER), F32),
        pltpu.VMEM((CONV_PAD + rows, CONV_DIM), F32),
        pltpu.VMEM((rows, CONV_DIM), F32),
        pltpu.VMEM((nsub, 2 * q, SSM_INNER), F32),
        pltpu.VMEM((nsub, 2 * q, kq), F32),
        pltpu.VMEM((rows, SSM_INNER), F32),
    ]
    return pl.pallas_call(
        functools.partial(_ssd_kernel, q=q, nsub=nsub),
        grid=(nb, nc),
        in_specs=in_specs,
        out_specs=out_specs,
        out_shape=out_shape,
        scratch_shapes=scratch,
        compiler_params=_cparams(("arbitrary", "arbitrary")),
        name="ssd",
    )(proj, proj, proj, conv_prev, s0, p["conv_w"], p["conv_b"], p["dt_bias"], p["a_log"], p["d_skip_p"],
      p["ssm_norm_w"], p["e_p"], p["e_q%d" % q])


def _mla_latents(mla_ref, ct_ref, sn_ref, qnw_ref, kvnw_ref, ckv_ref, kr_ref):
    m = mla_ref[...]
    cqn = _rms(m[:, 0:Q_LORA], qnw_ref[...])
    ckv = _rms(m[:, Q_LORA:Q_LORA + KV_LORA], kvnw_ref[...])
    ckv_ref[...] = ckv
    small = m[:, Q_LORA + KV_LORA:]
    ct, sn = ct_ref[...], sn_ref[...]
    kr128 = small[:, 128:256] * ct + small[:, 256:384] * sn
    kr_ref[...] = kr128[:, 0:QK_ROPE]
    return cqn, ckv, kr128, ct, sn


def _mla_prompt_kernel(mla_ref, ct_ref, sn_ref, ctt_ref, snt_ref, qnw_ref, kvnw_ref, wqat_ref, wqbt_ref, wk_ref,
                       wvt_ref, ckv_ref, kr_ref, qt_ref, kn_ref, krb_ref, vt_ref):
    cqn, ckv, kr128, _, _ = _mla_latents(mla_ref, ct_ref, sn_ref, qnw_ref, kvnw_ref, ckv_ref, kr_ref)
    cqn_t = cqn.T.astype(BF16)
    ckv_t = ckv.T.astype(BF16)
    ckv_bf = ckv.astype(BF16)
    krb_ref[...] = kr128.astype(BF16)
    ctt, snt = ctt_ref[...], snt_ref[...]
    tm = ckv_bf.shape[0]
    ones_row = jnp.where(lax.broadcasted_iota(jnp.int32, (V_ROWS - V_DIM, tm), 0) == 0, 1.0, 0.0).astype(BF16)
    qs = SCALE * LOG2E
    for h in range(MLA_HEADS):
        qa = _dot(wqat_ref[h * QK_PAD:(h + 1) * QK_PAD, :], cqn_t)
        qb = _dot(wqbt_ref[h * LANES:(h + 1) * LANES, :], cqn_t)
        qt_ref[h, 0:QK_NOPE, :] = (qa[0:QK_NOPE] * qs).astype(BF16)
        qt_ref[h, QK_NOPE:QK_PAD, :] = ((qa[QK_NOPE:QK_PAD] * ctt + qb * snt) * qs).astype(BF16)
        kn_ref[h] = _dot(ckv_bf, wk_ref[:, h * QK_NOPE:(h + 1) * QK_NOPE]).astype(BF16)
        vt_ref[h, 0, 0:V_DIM, :] = _dot(wvt_ref[h * V_DIM:(h + 1) * V_DIM, :], ckv_t).astype(BF16)
        vt_ref[h, 0, V_DIM:V_ROWS, :] = ones_row


def _mla_prompt(proj, ct, sn, p, tm):
    rows = proj.shape[0]
    row = lambda w: pl.BlockSpec((tm, w), lambda i: (i, 0))
    mla = pl.BlockSpec((tm, SEG_MLA), lambda i: (i, COL_MLA // SEG_MLA))
    col = pl.BlockSpec((LANES, tm), lambda i: (0, i))
    return pl.pallas_call(
        _mla_prompt_kernel,
        grid=(rows // tm,),
        in_specs=[mla, row(LANES), row(LANES), col, col, _const_spec((1, Q_LORA)),
                  _const_spec((1, KV_LORA)),
                  _const_spec((MLA_HEADS * QK_PAD, Q_LORA)), _const_spec((MLA_HEADS * LANES, Q_LORA)),
                  _const_spec((KV_LORA, MLA_HEADS * QK_NOPE)), _const_spec((MLA_INNER, KV_LORA))],
        out_specs=[row(KV_LORA), row(QK_ROPE),
                   pl.BlockSpec((MLA_HEADS, QK_PAD, tm), lambda i: (0, 0, i)),
                   pl.BlockSpec((MLA_HEADS, tm, QK_NOPE), lambda i: (0, i, 0)), row(LANES),
                   pl.BlockSpec((MLA_HEADS, 1, V_ROWS, tm), lambda i: (0, i, 0, 0))],
        out_shape=[jax.ShapeDtypeStruct((rows, KV_LORA), F32), jax.ShapeDtypeStruct((rows, QK_ROPE), F32),
                   jax.ShapeDtypeStruct((MLA_HEADS, QK_PAD, rows), BF16),
                   jax.ShapeDtypeStruct((MLA_HEADS, rows, QK_NOPE), BF16),
                   jax.ShapeDtypeStruct((rows, LANES), BF16),
                   jax.ShapeDtypeStruct((MLA_HEADS, rows // tm, V_ROWS, tm), BF16)],
        compiler_params=_cparams(("arbitrary",)),
        name="mla_prep_prompt",
    )(proj, ct, sn, ct.T, sn.T, p["q_norm_w"], p["kv_norm_w"], p["wq_a"].T, p["wq_b"].T, p["w_uk"], p["w_uv_t"])


def _mla_sample_kernel(mla_ref, ct_ref, sn_ref, qnw_ref, kvnw_ref, wqa_ref, wqb_ref, wk_ref,
                       ckv_ref, kr_ref, ql_ref, qr_ref):
    cqn, _, _, ct, sn = _mla_latents(mla_ref, ct_ref, sn_ref, qnw_ref, kvnw_ref, ckv_ref, kr_ref)
    cqn = cqn.astype(BF16)
    qa = _dot(cqn, wqa_ref[...])
    qb = _dot(cqn, wqb_ref[...])
    for h in range(MLA_HEADS):
        o = h * QK_PAD
        q_rope = qa[:, o + QK_NOPE:o + QK_PAD] * ct + qb[:, h * LANES:(h + 1) * LANES] * sn
        qr_ref[h] = (q_rope * SCALE).astype(BF16)
        q_nope = qa[:, o:o + QK_NOPE].astype(BF16)
        q_lat = _dot_nt(q_nope, wk_ref[:, h * QK_NOPE:(h + 1) * QK_NOPE])
        ql_ref[h] = (q_lat * SCALE).astype(BF16)


def _mla_sample(proj, ct, sn, p):
    rows = proj.shape[0]
    full = lambda w: pl.BlockSpec((rows, w), lambda i: (0, 0))
    mla = pl.BlockSpec((rows, SEG_MLA), lambda i: (0, COL_MLA // SEG_MLA))
    return pl.pallas_call(
        _mla_sample_kernel,
        grid=(1,),
        in_specs=[mla, full(LANES), full(LANES), _const_spec((1, Q_LORA)), _const_spec((1, KV_LORA)),
                  _const_spec((Q_LORA, MLA_HEADS * QK_PAD)), _const_spec((Q_LORA, MLA_HEADS * LANES)),
                  _const_spec((KV_LORA, MLA_HEADS * QK_NOPE))],
        out_specs=[full(KV_LORA), full(QK_ROPE),
                   pl.BlockSpec((MLA_HEADS, rows, KV_LORA), lambda i: (0, 0, 0)),
                   pl.BlockSpec((MLA_HEADS, rows, LANES), lambda i: (0, 0, 0))],
        out_shape=[jax.ShapeDtypeStruct((rows, KV_LORA), F32), jax.ShapeDtypeStruct((rows, QK_ROPE), F32),
                   jax.ShapeDtypeStruct((MLA_HEADS, rows, KV_LORA), BF16),
                   jax.ShapeDtypeStruct((MLA_HEADS, rows, LANES), BF16)],
        compiler_params=_cparams(("arbitrary",)),
        name="mla_prep_sample",
    )(proj, ct, sn, p["q_norm_w"], p["kv_norm_w"], p["wq_a"], p["wq_b"], p["w_uk"])


def _attn_kernel(qi_ref, kj_ref, qt_ref, kn_ref, kr_ref, vt_ref, o_ref, m_ref, acc_ref, s0_ref, s1_ref, p0_ref,
                 p1_ref, a0_ref, a1_ref, *, tq, tks):
    s_refs, p_refs, a_refs = (s0_ref, s1_ref), (p0_ref, p1_ref), (a0_ref, a1_ref)
    n_sub = 2
    t = pl.program_id(0)
    i = qi_ref[t]
    j = kj_ref[t]
    tk = n_sub * tks
    last_j = ((i + 1) * tq - 1) // tk

    @pl.when(j == 0)
    def _():
        m_ref[...] = jnp.full(m_ref.shape, -jnp.inf, F32)
        acc_ref[...] = jnp.zeros(acc_ref.shape, F32)

    def step(masked, sub_tiles):
        if masked:
            q_chunk = (i * tq + lax.broadcasted_iota(jnp.int32, (tks, tq), 1)) // CHUNK
            k_chunk = (j * tk + lax.broadcasted_iota(jnp.int32, (tks, tq), 0)) // CHUNK
            visible = {c: k_chunk + (c * tks) // CHUNK <= q_chunk for c in sub_tiles}

        def scores(h, c, b):
            k_h = jnp.concatenate([kn_ref[h, c * tks:(c + 1) * tks, :], kr_ref[c * tks:(c + 1) * tks, :]], axis=1)
            s_refs[b][...] = _dot(k_h, qt_ref[h])

        def softmax(h, c, b):
            s = s_refs[b][...]
            if masked:
                s = jnp.where(visible[c], s, -jnp.inf)
            m_prev = m_ref[h]
            m_new = jnp.maximum(m_prev, jnp.max(s, axis=0, keepdims=True))
            m_ref[h] = m_new
            p_refs[b][...] = jnp.exp2(s - m_new).astype(BF16)
            a_refs[b][...] = jnp.exp2(m_prev - m_new)

        def values(h, c, b):
            acc_ref[h] = acc_ref[h] * a_refs[b][...] + _dot(vt_ref[h, c], p_refs[b][...])

        items = [(h, c) for h in range(MLA_HEADS) for c in sub_tiles]
        stages = (scores, softmax, values)
        for n in range(len(items) + len(stages) - 1):
            for lag, stage in enumerate(stages):
                if 0 <= n - lag < len(items):
                    stage(*items[n - lag], (n - lag) % 2)

    skip_upper = (j * tk + tks) >= (i + 1) * tq

    @pl.when(j < last_j)
    def _():
        step(False, (0, 1))

    @pl.when((j == last_j) & jnp.logical_not(skip_upper))
    def _():
        step(True, (0, 1))

    @pl.when((j == last_j) & skip_upper)
    def _():
        step(True, (0,))

    @pl.when(j == last_j)
    def _():
        for h in range(MLA_HEADS):
            den = acc_ref[h, V_DIM:V_DIM + 1, :]
            o_ref[:, h * V_DIM:(h + 1) * V_DIM] = (acc_ref[h, 0:V_DIM, :] * (1.0 / den)).T


def _attention_prompt(qt, kn, kr, vt, tq):
    rows = kn.shape[1]
    n_sub, tks = 2, vt.shape[3]
    tk = tks * n_sub
    qi, kj = [], []
    for i in range(rows // tq):
        for j in range(((i + 1) * tq - 1) // tk + 1):
            qi.append(i)
            kj.append(j)
    qi = jnp.asarray(np.array(qi, np.int32))
    kj = jnp.asarray(np.array(kj, np.int32))
    grid_spec = pltpu.PrefetchScalarGridSpec(
        num_scalar_prefetch=2,
        grid=(int(qi.shape[0]),),
        in_specs=[
            pl.BlockSpec((MLA_HEADS, QK_PAD, tq), lambda t, qi, kj: (0, 0, qi[t])),
            pl.BlockSpec((MLA_HEADS, tk, QK_NOPE), lambda t, qi, kj: (0, kj[t], 0)),
            pl.BlockSpec((tk, LANES), lambda t, qi, kj: (kj[t], 0)),
            pl.BlockSpec((MLA_HEADS, n_sub, V_ROWS, tks), lambda t, qi, kj: (0, kj[t], 0, 0)),
        ],
        out_specs=pl.BlockSpec((tq, MLA_INNER), lambda t, qi, kj: (qi[t], 0)),
        scratch_shapes=[
            pltpu.VMEM((MLA_HEADS, 1, tq), F32),
            pltpu.VMEM((MLA_HEADS, V_ROWS, tq), F32),
            pltpu.VMEM((tks, tq), F32), pltpu.VMEM((tks, tq), F32),
            pltpu.VMEM((tks, tq), BF16), pltpu.VMEM((tks, tq), BF16),
            pltpu.VMEM((1, tq), F32), pltpu.VMEM((1, tq), F32),
        ],
    )
    return pl.pallas_call(
        functools.partial(_attn_kernel, tq=tq, tks=tks),
        grid_spec=grid_spec,
        out_shape=jax.ShapeDtypeStruct((rows, MLA_INNER), F32),
        compiler_params=_cparams(("arbitrary",)),
        name="attn_prompt",
    )(qi, kj, qt, kn, kr, vt)


def _attn_cached_kernel(ql_ref, qr_ref, ckv_c_ref, kr_c_ref, ckv_n_ref, kr_n_ref, wv_ref, o_ref, *, t_new):
    rows = MLA_HEADS * t_new
    ql = ql_ref[...].reshape(rows, KV_LORA)
    qr = qr_ref[...].reshape(rows, LANES)[:, 0:QK_ROPE]
    kc = ckv_c_ref[0].astype(BF16)
    kn = ckv_n_ref[0].astype(BF16)
    s_c = _dot_nt(ql, kc) + _dot_nt(qr, kr_c_ref[0].astype(BF16))
    s_n = _dot_nt(ql, kn) + _dot_nt(qr, kr_n_ref[0].astype(BF16))
    m = jnp.maximum(jnp.max(s_c, axis=1, keepdims=True), jnp.max(s_n, axis=1, keepdims=True))
    p_c = jnp.exp(s_c - m)
    p_n = jnp.exp(s_n - m)
    den = jnp.sum(p_c, axis=1, keepdims=True) + jnp.sum(p_n, axis=1, keepdims=True)
    o_lat = (_dot(p_c.astype(BF16), kc) + _dot(p_n.astype(BF16), kn)) / den
    o_bf = o_lat.astype(BF16)
    for h in range(MLA_HEADS):
        o_ref[0, :, h * V_DIM:(h + 1) * V_DIM] = _dot(o_bf[h * t_new:(h + 1) * t_new, :],
                                                        wv_ref[:, h * V_DIM:(h + 1) * V_DIM])


def _attention_cached(ql, qr, cache_kv, cache_kr, ckv_new, kr_new, w_uv):
    nb, past = cache_kv.shape[0], cache_kv.shape[1]
    t_new = ckv_new.shape[1]
    return pl.pallas_call(
        functools.partial(_attn_cached_kernel, t_new=t_new),
        grid=(nb,),
        in_specs=[
            pl.BlockSpec((MLA_HEADS, t_new, KV_LORA), lambda b: (0, b, 0)),
            pl.BlockSpec((MLA_HEADS, t_new, LANES), lambda b: (0, b, 0)),
            pl.BlockSpec((1, past, KV_LORA), lambda b: (b, 0, 0)),
            pl.BlockSpec((1, past, QK_ROPE), lambda b: (b, 0, 0)),
            pl.BlockSpec((1, t_new, KV_LORA), lambda b: (b, 0, 0)),
            pl.BlockSpec((1, t_new, QK_ROPE), lambda b: (b, 0, 0)),
            _const_spec((KV_LORA, MLA_INNER)),
        ],
        out_specs=pl.BlockSpec((1, t_new, MLA_INNER), lambda b: (b, 0, 0)),
        out_shape=jax.ShapeDtypeStruct((nb, t_new, MLA_INNER), F32),
        compiler_params=_cparams(("arbitrary",)),
        name="attn_cached",
    )(ql, qr, cache_kv, cache_kr, ckv_new, kr_new, w_uv)


def _merge_kernel(y_ref, attn_ref, gm_ref, gs_ref, ga_ref, wssm_ref, wmla_ref, o_ref):
    y_ssm = _dot(y_ref[...], wssm_ref[...])
    y_mla = _dot((attn_ref[...] * _silu(gm_ref[...])).astype(BF16), wmla_ref[...])
    o_ref[...] = (_sigmoid(gs_ref[...]) * y_ssm + _sigmoid(ga_ref[...]) * y_mla).astype(o_ref.dtype)


def _merge(y_norm, attn, proj, p, tm):
    rows = y_norm.shape[0]
    gate = lambda k: pl.BlockSpec((tm, D_MODEL), lambda i: (i, COL_GATES // D_MODEL + k))
    return pl.pallas_call(
        _merge_kernel,
        grid=(rows // tm,),
        in_specs=[pl.BlockSpec((tm, SSM_INNER), lambda i: (i, 0)), pl.BlockSpec((tm, MLA_INNER), lambda i: (i, 0)),
                  gate(0), gate(1), gate(2),
                  _const_spec((SSM_INNER, D_MODEL)), _const_spec((MLA_INNER, D_MODEL))],
        out_specs=pl.BlockSpec((tm, D_MODEL), lambda i: (i, 0)),
        out_shape=jax.ShapeDtypeStruct((rows, D_MODEL), BF16),
        compiler_params=_cparams(("arbitrary",)),
        name="merge",
    )(y_norm, attn, proj, proj, proj, p["w_ssm_out"], p["w_mla_out"])


def _outproj_kernel(x_ref, mixed_ref, w_ref, nw_ref, o_ref):
    x_out = x_ref[...] + _dot(mixed_ref[...], w_ref[...])
    o_ref[...] = _rms(x_out, nw_ref[...])


def _outproj(x, mixed, p, tm):
    rows = x.shape[0]
    row = pl.BlockSpec((tm, D_MODEL), lambda i: (i, 0))
    return pl.pallas_call(
        _outproj_kernel,
        grid=(rows // tm,),
        in_specs=[row, row, _const_spec((D_MODEL, D_MODEL)), _const_spec((1, D_MODEL))],
        out_specs=row,
        out_shape=jax.ShapeDtypeStruct((rows, D_MODEL), F32),
        compiler_params=_cparams(("arbitrary",)),
        name="outproj",
    )(x, mixed, p["w_out"], p["final_norm_w"])


def _prepare(norm_in_w, w_in, conv_w, conv_b, dt_bias, a_log, d_skip, ssm_norm_w, w_ssm_out, q_norm_w, w_q_up,
             kv_norm_w, w_kv_up, w_mla_out, w_out, final_norm_w, chunk_lens):
    offs = np.cumsum((0, SSM_INNER, CONV_DIM, SSM_HEADS, Q_LORA, KV_LORA, QK_ROPE, MLA_INNER, D_MODEL, D_MODEL))
    w_in_bf = w_in.astype(BF16)
    col = lambda k: w_in_bf[:, int(offs[k]):int(offs[k + 1])]
    half = QK_ROPE // 2
    kr_w = col(5)
    zeros = lambda n: jnp.zeros((D_MODEL, n), BF16)
    w_tail = jnp.concatenate([
        col(6), col(7), col(8),
        col(3), col(4),
        col(2), zeros(64), kr_w, zeros(64), kr_w[:, half:], kr_w[:, :half], zeros(64 + 128),
        zeros(SEG_MLA - Q_LORA - KV_LORA - 512)], axis=1)

    wq = w_q_up.reshape(Q_LORA, MLA_HEADS, QK_NOPE + QK_ROPE)
    wq_rope = wq[..., QK_NOPE:]
    zq = jnp.zeros((Q_LORA, MLA_HEADS, 64), w_q_up.dtype)
    wq_a = jnp.concatenate([wq, zq], axis=-1).reshape(Q_LORA, MLA_HEADS * QK_PAD).astype(BF16)
    wq_b = jnp.concatenate([wq_rope[..., half:], wq_rope[..., :half], zq], axis=-1)
    wq_b = wq_b.reshape(Q_LORA, MLA_HEADS * LANES).astype(BF16)

    wkv = w_kv_up.reshape(KV_LORA, MLA_HEADS, QK_NOPE + V_DIM)
    w_uk = wkv[..., :QK_NOPE].reshape(KV_LORA, MLA_HEADS * QK_NOPE).astype(BF16)
    w_uv_h = wkv[..., QK_NOPE:]
    w_uv = w_uv_h.reshape(KV_LORA, MLA_INNER).astype(BF16)
    w_uv_t = w_uv.T

    head_of_lane = np.arange(SSM_INNER) // SSM_HEAD_DIM
    p = dict(
        norm_in_w=norm_in_w.reshape(1, D_MODEL), w_main=w_in_bf, w_tail=w_tail,
        conv_w=conv_w, conv_b=conv_b.reshape(1, CONV_DIM), dt_bias=dt_bias.reshape(1, SSM_HEADS),
        a_log=a_log.reshape(1, SSM_HEADS), d_skip_p=jnp.repeat(d_skip, SSM_HEAD_DIM).reshape(1, SSM_INNER),
        ssm_norm_w=ssm_norm_w.reshape(1, SSM_INNER), w_ssm_out=w_ssm_out.astype(BF16),
        q_norm_w=q_norm_w.reshape(1, Q_LORA), kv_norm_w=kv_norm_w.reshape(1, KV_LORA),
        wq_a=wq_a, wq_b=wq_b, w_uk=w_uk, w_uv=w_uv, w_uv_t=w_uv_t,
        w_mla_out=w_mla_out.astype(BF16), w_out=w_out.astype(BF16),
        final_norm_w=final_norm_w.reshape(1, D_MODEL),
        e_p=jnp.asarray(np.tile(head_of_lane[None, :] == np.arange(SSM_HEADS)[:, None], (3, 1)), BF16),
    )
    for q in chunk_lens:
        head_of_key_lane = np.arange(SSM_HEADS * q) // q
        p["e_q%d" % q] = jnp.asarray(np.tile(head_of_key_lane[None, :] == np.arange(SSM_HEADS)[:, None], (3, 1)),
                                     BF16)
    return p


def _rope_tables(pos):
    half = QK_ROPE // 2
    inv = 1.0 / (ROPE_THETA ** (jnp.arange(half, dtype=F32) * (2.0 / QK_ROPE)))
    ang = pos.astype(F32)[:, None] * inv[None, :]
    cos, sin = jnp.cos(ang), jnp.sin(ang)
    pad = jnp.zeros((pos.shape[0], LANES - QK_ROPE), F32)
    return jnp.concatenate([cos, cos, pad], axis=1), jnp.concatenate([-sin, sin, pad], axis=1)


def kernel(x_prompt, x_sample, cache_kv_latent, cache_k_rope, state_ssm, state_conv, norm_in_w, w_in, conv_w,
           conv_b, dt_bias, a_log, d_skip, ssm_norm_w, w_ssm_out, q_norm_w, w_q_up, kv_norm_w, w_kv_up, w_mla_out,
           w_out, final_norm_w):
    depth = w_in.shape[0]
    assert depth == 1 and x_prompt.shape[0] == 1
    seq = x_prompt.shape[1]
    nb, t_new = x_sample.shape[0], x_sample.shape[1]
    past = cache_kv_latent.shape[2]
    q_prompt, q_sample = min(CHUNK, seq), min(CHUNK, t_new)
    p = _prepare(norm_in_w[0], w_in[0], conv_w[0], conv_b[0], dt_bias[0], a_log[0], d_skip[0], ssm_norm_w[0],
                 w_ssm_out[0], q_norm_w[0], w_q_up[0], kv_norm_w[0], w_kv_up[0], w_mla_out[0], w_out[0],
                 final_norm_w, sorted({q_prompt, q_sample}))

    xp = x_prompt[0]
    proj = _inproj(xp, p["norm_in_w"], p["w_main"], p["w_tail"], tm=min(1024, seq))
    y_norm, ssm_p, conv_p = _ssd(proj[None],
                                 jnp.zeros((1, CONV_WIDTH - 1, CONV_DIM), F32),
                                 jnp.zeros((1, SSM_INNER, SSM_STATE), F32), p, q_prompt,
                                 nsub=4 if seq % (4 * q_prompt) == 0 else 1)
    ct, sn = _rope_tables(jnp.arange(seq))
    ckv_p, kr_p, qt, kn, krb, vt = _mla_prompt(proj, ct, sn, p, tm=min(512, seq // 2))
    attn = _attention_prompt(qt, kn, krb, vt, tq=min(512, seq))
    mixed = _merge(y_norm[0], attn, proj, p, tm=min(256, seq))
    y_prompt = _outproj(xp, mixed, p, tm=min(256, seq))

    rows_s = nb * t_new
    xs = x_sample.reshape(rows_s, D_MODEL)
    proj_s = _inproj(xs, p["norm_in_w"], p["w_main"], p["w_tail"], tm=rows_s)
    y_norm_s, ssm_s, conv_s = _ssd(proj_s.reshape(nb, t_new, PROJ_COLS), state_conv[0],
                                   state_ssm[0].reshape(nb, SSM_INNER, SSM_STATE), p, q_sample, nsub=1)
    ct_s, sn_s = _rope_tables(past + jnp.arange(t_new))
    ckv_s, kr_s, ql, qr = _mla_sample(proj_s, jnp.tile(ct_s, (nb, 1)), jnp.tile(sn_s, (nb, 1)), p)
    ckv_s = ckv_s.reshape(nb, t_new, KV_LORA)
    kr_s = kr_s.reshape(nb, t_new, QK_ROPE)
    attn_s = _attention_cached(ql, qr, cache_kv_latent[0], cache_k_rope[0], ckv_s, kr_s, p["w_uv"])
    mixed_s = _merge(y_norm_s.reshape(rows_s, SSM_INNER), attn_s.reshape(rows_s, MLA_INNER), proj_s, p, tm=rows_s)
    y_sample = _outproj(xs, mixed_s, p, tm=rows_s).reshape(nb, t_new, D_MODEL)

    hshape = (SSM_HEADS, SSM_HEAD_DIM, SSM_STATE)
    return (y_prompt[None], y_sample,
            ckv_p[None, None], kr_p[None, None], ssm_p.reshape((1, 1) + hshape), conv_p[None],
            ckv_s[None], kr_s[None], ssm_s.reshape((1, nb) + hshape), conv_s[None])
```

```python
import functools
import math
from typing import NamedTuple

import jax
import jax.numpy as jnp
import numpy as np
from jax import lax
from jax.experimental import pallas as pl
from jax.experimental.pallas import tpu as pltpu

F32 = jnp.float32
BF16 = jnp.bfloat16

D_MODEL = 2048
CHUNK = 64
SSM_INNER = 4096
SSM_HEAD_DIM = 64
SSM_HEADS = 64
SSM_GROUPS = 8
SSM_STATE = 128
CONV_WIDTH = 4
CONV_DIM = SSM_INNER + 2 * SSM_GROUPS * SSM_STATE
GROUP_LANES = SSM_INNER // SSM_GROUPS
MLA_HEADS = 16
Q_LORA = 512
KV_LORA = 512
QK_NOPE = 128
QK_ROPE = 64
V_DIM = 128
MLA_INNER = MLA_HEADS * V_DIM
ROPE_THETA = 10000.0
EPS = 1e-6
QK_PAD = 256
SCALE = (QK_NOPE + QK_ROPE) ** -0.5
LOG2E = math.log2(math.e)
V_ROWS = V_DIM + 16
LANES = 128
V7X_VMEM_BYTES = 64 * 1024 * 1024
VMEM_LIMIT = V7X_VMEM_BYTES * 7 // 8
ATTN_SUB_TILES = 2

IN_TN = 1024
SEG_MLA = 2048
SMALL_COLS = 512
SMALL_DT, SMALL_KR, SMALL_KR_SWAPPED = 0, 128, 256
COL_XBC = 0
COL_GATES = COL_XBC + CONV_DIM
COL_Z = COL_GATES + MLA_INNER + 2 * D_MODEL
COL_MLA = COL_Z + SSM_INNER
PROJ_COLS = COL_MLA + SEG_MLA
COL_SMALL = COL_MLA + Q_LORA + KV_LORA


def _cparams(sem, vmem=VMEM_LIMIT):
    return pltpu.CompilerParams(dimension_semantics=sem, vmem_limit_bytes=vmem)


def _const_spec(shape):
    nd = len(shape)
    return pl.BlockSpec(shape, lambda *_: (0,) * nd, pipeline_mode=pl.Buffered(1))


def _sigmoid(x):
    return 0.5 + 0.5 * jnp.tanh(0.5 * x)


def _silu(x):
    h = 0.5 * x
    return h + h * jnp.tanh(h)


def _rms(x, w):
    return x * lax.rsqrt(jnp.mean(x * x, axis=-1, keepdims=True) + EPS) * w


def _dot(a, b):
    return jnp.dot(a, b, preferred_element_type=F32)


def _dot_nt(a, b):
    return lax.dot_general(a, b, (((1,), (1,)), ((), ())), preferred_element_type=F32)


def _dot_tn(a, b):
    return lax.dot_general(a, b, (((0,), (0,)), ((), ())), preferred_element_type=F32)


def _split3(v):
    hi = v.astype(BF16)
    r1 = v - hi.astype(F32)
    mid = r1.astype(BF16)
    lo = (r1 - mid.astype(F32)).astype(BF16)
    return hi, mid, lo


def _inproj_kernel(x_ref, nw_ref, wm_ref, wt_ref, o_ref, h_ref, *, from_main):
    j = pl.program_id(1)

    @pl.when(j == 0)
    def _():
        h_ref[...] = _rms(x_ref[...], nw_ref[...]).astype(BF16)

    is_main = functools.reduce(jnp.logical_or, [(j >= lo) & (j < hi) for lo, hi in from_main])

    @pl.when(is_main)
    def _():
        o_ref[...] = _dot(h_ref[...], wm_ref[...])

    @pl.when(jnp.logical_not(is_main))
    def _():
        o_ref[...] = _dot(h_ref[...], wt_ref[...])


def _inproj(x, norm_w, w_main, w_tail, tm):
    rows = x.shape[0]
    t = lambda cols: cols // IN_TN
    n_xbc, n_g, n_z = t(CONV_DIM), t(MLA_INNER + 2 * D_MODEL), t(SSM_INNER)
    j_g, j_z, j_mla = t(COL_GATES), t(COL_Z), t(COL_MLA)
    src_xbc, src_z = t(SSM_INNER), 0

    def main_map(i, j):
        return 0, jnp.where(j < j_z, src_xbc + jnp.minimum(j, n_xbc - 1), src_z + jnp.minimum(j - j_z, n_z - 1))

    def tail_map(i, j):
        return 0, jnp.where(j < j_mla, jnp.clip(j - j_g, 0, n_g - 1), n_g + j - j_mla)

    return pl.pallas_call(
        functools.partial(_inproj_kernel, from_main=((0, n_xbc), (j_z, j_z + n_z))),
        grid=(rows // tm, t(PROJ_COLS)),
        in_specs=[
            pl.BlockSpec((tm, D_MODEL), lambda i, j: (i, 0)),
            pl.BlockSpec((1, D_MODEL), lambda i, j: (0, 0)),
            pl.BlockSpec((D_MODEL, IN_TN), main_map),
            pl.BlockSpec((D_MODEL, IN_TN), tail_map),
        ],
        out_specs=pl.BlockSpec((tm, IN_TN), lambda i, j: (i, j)),
        out_shape=jax.ShapeDtypeStruct((rows, PROJ_COLS), F32),
        scratch_shapes=[pltpu.VMEM((tm, D_MODEL), BF16)],
        compiler_params=_cparams(("arbitrary", "arbitrary")),
        name="inproj",
    )(x, norm_w, w_main, w_tail)


CONV_PAD = 8
CONV_SLAB = 256


def _ssd_kernel(xbc_ref, z_ref, sm_ref, cprev_ref, s0_ref, cw_ref, cb_ref, dtb_ref, alog_ref, dsk_ref,
                nw_ref, ep_ref, eq_ref, y_ref, sout_ref, cout_ref, st_ref, xpad_ref, xc_ref, exp_p_ref,
                exp_q_ref, yd_ref, *, q, nsub):
    c = pl.program_id(1)
    nc = pl.num_programs(1)
    rows = nsub * q
    hp = LANES // q
    tiles_per_group = (SSM_HEADS // SSM_GROUPS) // hp
    tile_w = hp * SSM_HEAD_DIM
    lo = CONV_PAD - (CONV_WIDTH - 1)

    @pl.when(c == 0)
    def _():
        st_ref[...] = s0_ref[0].T
        xpad_ref[0:lo, :] = jnp.zeros((lo, CONV_DIM), F32)
        xpad_ref[lo:CONV_PAD, :] = cprev_ref[0]

    assert CONV_WIDTH == 4
    xpad_ref[CONV_PAD:CONV_PAD + rows, :] = xbc_ref[0]
    for s in range(0, CONV_DIM, CONV_SLAB):
        xw = xpad_ref[:, s:s + CONV_SLAB]
        x1 = pltpu.roll(xw, 1, 0)
        w = [cw_ref[k:k + 1, s:s + CONV_SLAB] for k in range(CONV_WIDTH)]
        u = (xw * w[3] + x1 * w[2]) + pltpu.roll(xw * w[1] + x1 * w[0], 2, 0)
        xc_ref[:, s:s + CONV_SLAB] = _silu(u[CONV_PAD:, :] + cb_ref[:, s:s + CONV_SLAB])
    tail_rows = xpad_ref[CONV_PAD + rows - (CONV_WIDTH - 1):CONV_PAD + rows, :]
    xpad_ref[lo:CONV_PAD, :] = tail_rows

    ri = lax.broadcasted_iota(jnp.int32, (q, q), 0)
    ci = lax.broadcasted_iota(jnp.int32, (q, q), 1)
    tril = jnp.where(ci <= ri, 1.0, 0.0).astype(BF16)
    r128 = lax.broadcasted_iota(jnp.int32, (q, LANES), 0)
    c128 = lax.broadcasted_iota(jnp.int32, (q, LANES), 1)
    key128 = c128 & (q - 1)
    causal = key128 <= r128
    diag = key128 == r128
    br = lax.broadcasted_iota(jnp.int32, (LANES, tile_w), 0)
    bc = lax.broadcasted_iota(jnp.int32, (LANES, tile_w), 1)
    blockdiag = (br // q) == (bc // SSM_HEAD_DIM)
    neg_a = -jnp.exp(alog_ref[...])

    for u in range(nsub):
        r0 = u * q
        xdt = sm_ref[0, r0:r0 + q, SMALL_DT:SMALL_DT + SSM_HEADS] + dtb_ref[...]
        dt = jnp.maximum(xdt, 0.0) + jnp.log1p(jnp.exp(-jnp.abs(xdt)))
        h3 = _split3(dt * neg_a)
        acs = _dot(tril, h3[0]) + _dot(tril, h3[1]) + _dot(tril, h3[2])

        def expand(v, e_ref):
            pieces = jnp.concatenate([piece.astype(F32) for piece in _split3(v)], axis=1).astype(BF16)
            return _dot(pieces, e_ref[...])

        exp_q_ref[u] = expand(jnp.concatenate([acs, dt], axis=0), eq_ref)
        exp_p_ref[u] = expand(jnp.concatenate([jnp.exp(acs), jnp.exp(acs[q - 1:q, :] - acs) * dt], axis=0), ep_ref)

        for g in range(SSM_GROUPS):
            gs = g * GROUP_LANES
            bs = SSM_INNER + g * SSM_STATE
            cs = SSM_INNER + (SSM_GROUPS + g) * SSM_STATE
            b_bf = xc_ref[r0:r0 + q, bs:bs + SSM_STATE].astype(BF16)
            c_bf = xc_ref[r0:r0 + q, cs:cs + SSM_STATE].astype(BF16)
            cb_t = _dot_nt(c_bf, jnp.concatenate([b_bf] * hp, axis=0))

            for tt in range(tiles_per_group):
                t = g * tiles_per_group + tt
                a_t = exp_q_ref[u, 0:q, t * LANES:(t + 1) * LANES]
                d_t = exp_q_ref[u, q:2 * q, t * LANES:(t + 1) * LANES]
                a_key = jnp.sum(jnp.where(diag, a_t, 0.0), axis=0, keepdims=True)
                d_key = jnp.sum(jnp.where(diag, d_t, 0.0), axis=0, keepdims=True)
                w_t = jnp.exp(jnp.where(causal, a_t - a_key, -jnp.inf)) * d_key * cb_t
                x_t = xc_ref[r0:r0 + q, t * tile_w:(t + 1) * tile_w]
                rhs = jnp.where(blockdiag, jnp.concatenate([x_t] * hp, axis=0), 0.0).astype(BF16)
                yd_ref[r0:r0 + q, t * tile_w:(t + 1) * tile_w] = _dot(w_t.astype(BF16), rhs)

            decay = exp_p_ref[u, 0:q, gs:gs + GROUP_LANES]
            tail = exp_p_ref[u, q:2 * q, gs:gs + GROUP_LANES]
            xs_g = xc_ref[r0:r0 + q, gs:gs + GROUP_LANES]
            st_g = st_ref[:, gs:gs + GROUP_LANES]
            y_off = _dot(c_bf, st_g.astype(BF16)) * decay
            xt = (xs_g * tail).astype(BF16)
            st_ref[:, gs:gs + GROUP_LANES] = st_g * decay[q - 1:q, :] + _dot_tn(b_bf, xt)

            y = yd_ref[r0:r0 + q, gs:gs + GROUP_LANES] + y_off + dsk_ref[:, gs:gs + GROUP_LANES] * xs_g
            y = y * _silu(z_ref[0, r0:r0 + q, gs:gs + GROUP_LANES])
            y = y * lax.rsqrt(jnp.mean(y * y, axis=-1, keepdims=True) + EPS)
            y_ref[0, r0:r0 + q, gs:gs + GROUP_LANES] = (y * nw_ref[:, gs:gs + GROUP_LANES]).astype(y_ref.dtype)

    @pl.when(c == nc - 1)
    def _():
        sout_ref[0] = st_ref[...].T
        cout_ref[0] = tail_rows


def _ssd(proj, conv_prev, s0, p, q, nsub):
    nb, lb = proj.shape[0], proj.shape[1]
    rows = q * nsub
    nc = lb // rows
    kq = SSM_HEADS * q
    in_specs = [
        pl.BlockSpec((1, rows, CONV_DIM), lambda b, c: (b, c, COL_XBC // CONV_DIM)),
        pl.BlockSpec((1, rows, SSM_INNER), lambda b, c: (b, c, COL_Z // SSM_INNER)),
        pl.BlockSpec((1, rows, SMALL_COLS), lambda b, c: (b, c, COL_SMALL // SMALL_COLS)),
        pl.BlockSpec((1, CONV_WIDTH - 1, CONV_DIM), lambda b, c: (b, 0, 0)),
        pl.BlockSpec((1, SSM_INNER, SSM_STATE), lambda b, c: (b, 0, 0)),
        _const_spec((CONV_WIDTH, CONV_DIM)),
        _const_spec((1, CONV_DIM)),
        _const_spec((1, SSM_HEADS)),
        _const_spec((1, SSM_HEADS)),
        _const_spec((1, SSM_INNER)),
        _const_spec((1, SSM_INNER)),
        _const_spec((3 * SSM_HEADS, SSM_INNER)),
        _const_spec((3 * SSM_HEADS, kq)),
    ]
    out_specs = [
        pl.BlockSpec((1, rows, SSM_INNER), lambda b, c: (b, c, 0)),
        pl.BlockSpec((1, SSM_INNER, SSM_STATE), lambda b, c: (b, 0, 0)),
        pl.BlockSpec((1, CONV_WIDTH - 1, CONV_DIM), lambda b, c: (b, 0, 0)),
    ]
    out_shape = [
        jax.ShapeDtypeStruct((nb, lb, SSM_INNER), BF16),
        jax.ShapeDtypeStruct((nb, SSM_INNER, SSM_STATE), F32),
        jax.ShapeDtypeStruct((nb, CONV_WIDTH - 1, CONV_DIM), F32),
    ]
    scratch = [
        pltpu.VMEM((SSM_STATE, SSM_INNER), F32),
        pltpu.VMEM((CONV_PAD + rows, CONV_DIM), F32),
        pltpu.VMEM((rows, CONV_DIM), F32),
        pltpu.VMEM((nsub, 2 * q, SSM_INNER), F32),
        pltpu.VMEM((nsub, 2 * q, kq), F32),
        pltpu.VMEM((rows, SSM_INNER), F32),
    ]
    return pl.pallas_call(
        functools.partial(_ssd_kernel, q=q, nsub=nsub),
        grid=(nb, nc),
        in_specs=in_specs,
        out_specs=out_specs,
        out_shape=out_shape,
        scratch_shapes=scratch,
        compiler_params=_cparams(("arbitrary", "arbitrary")),
        name="ssd",
    )(proj, proj, proj, conv_prev, s0, p["conv_w"], p["conv_b"], p["dt_bias"], p["a_log"], p["d_skip_p"],
      p["ssm_norm_w"], p["e_p"], p["e_q%d" % q])


def _mla_latents(mla_ref, ct_ref, sn_ref, qnw_ref, kvnw_ref, ckv_ref, kr_ref):
    m = mla_ref[...]
    cqn = _rms(m[:, 0:Q_LORA], qnw_ref[...])
    ckv = _rms(m[:, Q_LORA:Q_LORA + KV_LORA], kvnw_ref[...])
    ckv_ref[...] = ckv
    small = m[:, Q_LORA + KV_LORA:]
    ct, sn = ct_ref[...], sn_ref[...]
    kr128 = (small[:, SMALL_KR:SMALL_KR + LANES] * ct
             + small[:, SMALL_KR_SWAPPED:SMALL_KR_SWAPPED + LANES] * sn)
    kr_ref[...] = kr128[:, 0:QK_ROPE]
    return cqn, ckv, kr128, ct, sn


def _mla_prompt_kernel(mla_ref, ct_ref, sn_ref, qnw_ref, kvnw_ref, wqat_ref, wqbt_ref, wk_ref,
                       wvt_ref, ckv_ref, kr_ref, qt_ref, kn_ref, krb_ref, vt_ref):
    cqn, ckv, kr128, ct, sn = _mla_latents(mla_ref, ct_ref, sn_ref, qnw_ref, kvnw_ref, ckv_ref, kr_ref)
    cqn_t = cqn.T.astype(BF16)
    ckv_t = ckv.T.astype(BF16)
    ckv_bf = ckv.astype(BF16)
    krb_ref[...] = kr128.astype(BF16)
    ctt, snt = ct.T, sn.T
    tm = ckv_bf.shape[0]
    ones_row = jnp.where(lax.broadcasted_iota(jnp.int32, (V_ROWS - V_DIM, tm), 0) == 0, 1.0, 0.0).astype(BF16)
    qs = SCALE * LOG2E
    for h in range(MLA_HEADS):
        qa = _dot(wqat_ref[h * QK_PAD:(h + 1) * QK_PAD, :], cqn_t)
        qb = _dot(wqbt_ref[h * LANES:(h + 1) * LANES, :], cqn_t)
        qt_ref[h, 0:QK_NOPE, :] = (qa[0:QK_NOPE] * qs).astype(BF16)
        qt_ref[h, QK_NOPE:QK_PAD, :] = ((qa[QK_NOPE:QK_PAD] * ctt + qb * snt) * qs).astype(BF16)
        kn_ref[h] = _dot(ckv_bf, wk_ref[:, h * QK_NOPE:(h + 1) * QK_NOPE]).astype(BF16)
        vt_ref[h, 0, 0:V_DIM, :] = _dot(wvt_ref[h * V_DIM:(h + 1) * V_DIM, :], ckv_t).astype(BF16)
        vt_ref[h, 0, V_DIM:V_ROWS, :] = ones_row


def _mla_prompt(proj, ct, sn, p, tm):
    rows = proj.shape[0]
    row = lambda w: pl.BlockSpec((tm, w), lambda i: (i, 0))
    mla = pl.BlockSpec((tm, SEG_MLA), lambda i: (i, COL_MLA // SEG_MLA))
    return pl.pallas_call(
        _mla_prompt_kernel,
        grid=(rows // tm,),
        in_specs=[mla, row(LANES), row(LANES), _const_spec((1, Q_LORA)),
                  _const_spec((1, KV_LORA)),
                  _const_spec((MLA_HEADS * QK_PAD, Q_LORA)), _const_spec((MLA_HEADS * LANES, Q_LORA)),
                  _const_spec((KV_LORA, MLA_HEADS * QK_NOPE)), _const_spec((MLA_INNER, KV_LORA))],
        out_specs=[row(KV_LORA), row(QK_ROPE),
                   pl.BlockSpec((MLA_HEADS, QK_PAD, tm), lambda i: (0, 0, i)),
                   pl.BlockSpec((MLA_HEADS, tm, QK_NOPE), lambda i: (0, i, 0)), row(LANES),
                   pl.BlockSpec((MLA_HEADS, 1, V_ROWS, tm), lambda i: (0, i, 0, 0))],
        out_shape=[jax.ShapeDtypeStruct((rows, KV_LORA), F32), jax.ShapeDtypeStruct((rows, QK_ROPE), F32),
                   jax.ShapeDtypeStruct((MLA_HEADS, QK_PAD, rows), BF16),
                   jax.ShapeDtypeStruct((MLA_HEADS, rows, QK_NOPE), BF16),
                   jax.ShapeDtypeStruct((rows, LANES), BF16),
                   jax.ShapeDtypeStruct((MLA_HEADS, rows // tm, V_ROWS, tm), BF16)],
        compiler_params=_cparams(("arbitrary",)),
        name="mla_prep_prompt",
    )(proj, ct, sn, p["q_norm_w"], p["kv_norm_w"], p["wq_a"].T, p["wq_b"].T, p["w_uk"], p["w_uv_t"])


def _mla_sample_kernel(mla_ref, ct_ref, sn_ref, qnw_ref, kvnw_ref, wqa_ref, wqb_ref, wk_ref,
                       ckv_ref, kr_ref, ql_ref, qr_ref):
    cqn, _, _, ct, sn = _mla_latents(mla_ref, ct_ref, sn_ref, qnw_ref, kvnw_ref, ckv_ref, kr_ref)
    cqn = cqn.astype(BF16)
    qa = _dot(cqn, wqa_ref[...])
    qb = _dot(cqn, wqb_ref[...])
    for h in range(MLA_HEADS):
        o = h * QK_PAD
        q_rope = qa[:, o + QK_NOPE:o + QK_PAD] * ct + qb[:, h * LANES:(h + 1) * LANES] * sn
        qr_ref[h] = (q_rope * SCALE).astype(BF16)
        q_nope = qa[:, o:o + QK_NOPE].astype(BF16)
        q_lat = _dot_nt(q_nope, wk_ref[:, h * QK_NOPE:(h + 1) * QK_NOPE])
        ql_ref[h] = (q_lat * SCALE).astype(BF16)


def _mla_sample(proj, ct, sn, p):
    rows = proj.shape[0]
    full = lambda w: pl.BlockSpec((rows, w), lambda i: (0, 0))
    mla = pl.BlockSpec((rows, SEG_MLA), lambda i: (0, COL_MLA // SEG_MLA))
    return pl.pallas_call(
        _mla_sample_kernel,
        grid=(1,),
        in_specs=[mla, full(LANES), full(LANES), _const_spec((1, Q_LORA)), _const_spec((1, KV_LORA)),
                  _const_spec((Q_LORA, MLA_HEADS * QK_PAD)), _const_spec((Q_LORA, MLA_HEADS * LANES)),
                  _const_spec((KV_LORA, MLA_HEADS * QK_NOPE))],
        out_specs=[full(KV_LORA), full(QK_ROPE),
                   pl.BlockSpec((MLA_HEADS, rows, KV_LORA), lambda i: (0, 0, 0)),
                   pl.BlockSpec((MLA_HEADS, rows, LANES), lambda i: (0, 0, 0))],
        out_shape=[jax.ShapeDtypeStruct((rows, KV_LORA), F32), jax.ShapeDtypeStruct((rows, QK_ROPE), F32),
                   jax.ShapeDtypeStruct((MLA_HEADS, rows, KV_LORA), BF16),
                   jax.ShapeDtypeStruct((MLA_HEADS, rows, LANES), BF16)],
        compiler_params=_cparams(("arbitrary",)),
        name="mla_prep_sample",
    )(proj, ct, sn, p["q_norm_w"], p["kv_norm_w"], p["wq_a"], p["wq_b"], p["w_uk"])


def _attn_kernel(qi_ref, kj_ref, qt_ref, kn_ref, kr_ref, vt_ref, o_ref, m_ref, acc_ref, s0_ref, s1_ref, p0_ref,
                 p1_ref, a0_ref, a1_ref, *, tq, tks):
    s_refs, p_refs, a_refs = (s0_ref, s1_ref), (p0_ref, p1_ref), (a0_ref, a1_ref)
    n_sub = ATTN_SUB_TILES
    t = pl.program_id(0)
    i = qi_ref[t]
    j = kj_ref[t]
    tk = n_sub * tks
    last_j = ((i + 1) * tq - 1) // tk

    @pl.when(j == 0)
    def _():
        m_ref[...] = jnp.full(m_ref.shape, -jnp.inf, F32)
        acc_ref[...] = jnp.zeros(acc_ref.shape, F32)

    def step(masked, sub_tiles):
        if masked:
            q_chunk = (i * tq + lax.broadcasted_iota(jnp.int32, (tks, tq), 1)) // CHUNK
            k_chunk = (j * tk + lax.broadcasted_iota(jnp.int32, (tks, tq), 0)) // CHUNK
            visible = {c: k_chunk + (c * tks) // CHUNK <= q_chunk for c in sub_tiles}

        def scores(h, c, b):
            k_h = jnp.concatenate([kn_ref[h, c * tks:(c + 1) * tks, :], kr_ref[c * tks:(c + 1) * tks, :]], axis=1)
            s_refs[b][...] = _dot(k_h, qt_ref[h])

        def softmax(h, c, b):
            s = s_refs[b][...]
            if masked:
                s = jnp.where(visible[c], s, -jnp.inf)
            m_prev = m_ref[h]
            m_new = jnp.maximum(m_prev, jnp.max(s, axis=0, keepdims=True))
            m_ref[h] = m_new
            p_refs[b][...] = jnp.exp2(s - m_new).astype(BF16)
            a_refs[b][...] = jnp.exp2(m_prev - m_new)

        def values(h, c, b):
            acc_ref[h] = acc_ref[h] * a_refs[b][...] + _dot(vt_ref[h, c], p_refs[b][...])

        items = [(h, c) for h in range(MLA_HEADS) for c in sub_tiles]
        stages = (scores, softmax, values)
        for n in range(len(items) + len(stages) - 1):
            for lag, stage in enumerate(stages):
                if 0 <= n - lag < len(items):
                    stage(*items[n - lag], (n - lag) % 2)

    skip_upper = (j * tk + tks) >= (i + 1) * tq

    @pl.when(j < last_j)
    def _():
        step(False, tuple(range(n_sub)))

    @pl.when((j == last_j) & jnp.logical_not(skip_upper))
    def _():
        step(True, tuple(range(n_sub)))

    @pl.when((j == last_j) & skip_upper)
    def _():
        step(True, (0,))

    @pl.when(j == last_j)
    def _():
        for h in range(MLA_HEADS):
            den = acc_ref[h, V_DIM:V_DIM + 1, :]
            o_ref[:, h * V_DIM:(h + 1) * V_DIM] = (acc_ref[h, 0:V_DIM, :] * (1.0 / den)).T


def _attention_prompt(qt, kn, kr, vt, tq):
    rows = kn.shape[1]
    n_sub, tks = ATTN_SUB_TILES, vt.shape[3]
    tk = tks * n_sub
    qi, kj = [], []
    for i in range(rows // tq):
        for j in range(((i + 1) * tq - 1) // tk + 1):
            qi.append(i)
            kj.append(j)
    qi = jnp.asarray(np.array(qi, np.int32))
    kj = jnp.asarray(np.array(kj, np.int32))
    grid_spec = pltpu.PrefetchScalarGridSpec(
        num_scalar_prefetch=2,
        grid=(int(qi.shape[0]),),
        in_specs=[
            pl.BlockSpec((MLA_HEADS, QK_PAD, tq), lambda t, qi, kj: (0, 0, qi[t])),
            pl.BlockSpec((MLA_HEADS, tk, QK_NOPE), lambda t, qi, kj: (0, kj[t], 0)),
            pl.BlockSpec((tk, LANES), lambda t, qi, kj: (kj[t], 0)),
            pl.BlockSpec((MLA_HEADS, n_sub, V_ROWS, tks), lambda t, qi, kj: (0, kj[t], 0, 0)),
        ],
        out_specs=pl.BlockSpec((tq, MLA_INNER), lambda t, qi, kj: (qi[t], 0)),
        scratch_shapes=[
            pltpu.VMEM((MLA_HEADS, 1, tq), F32),
            pltpu.VMEM((MLA_HEADS, V_ROWS, tq), F32),
            pltpu.VMEM((tks, tq), F32), pltpu.VMEM((tks, tq), F32),
            pltpu.VMEM((tks, tq), BF16), pltpu.VMEM((tks, tq), BF16),
            pltpu.VMEM((1, tq), F32), pltpu.VMEM((1, tq), F32),
        ],
    )
    return pl.pallas_call(
        functools.partial(_attn_kernel, tq=tq, tks=tks),
        grid_spec=grid_spec,
        out_shape=jax.ShapeDtypeStruct((rows, MLA_INNER), F32),
        compiler_params=_cparams(("arbitrary",)),
        name="attn_prompt",
    )(qi, kj, qt, kn, kr, vt)


def _attn_cached_kernel(ql_ref, qr_ref, ckv_c_ref, kr_c_ref, ckv_n_ref, kr_n_ref, wv_ref, o_ref, *, t_new):
    rows = MLA_HEADS * t_new
    ql = ql_ref[...].reshape(rows, KV_LORA)
    qr = qr_ref[...].reshape(rows, LANES)[:, 0:QK_ROPE]
    kc = ckv_c_ref[0].astype(BF16)
    kn = ckv_n_ref[0].astype(BF16)
    s_c = _dot_nt(ql, kc) + _dot_nt(qr, kr_c_ref[0].astype(BF16))
    s_n = _dot_nt(ql, kn) + _dot_nt(qr, kr_n_ref[0].astype(BF16))
    m = jnp.maximum(jnp.max(s_c, axis=1, keepdims=True), jnp.max(s_n, axis=1, keepdims=True))
    p_c = jnp.exp(s_c - m)
    p_n = jnp.exp(s_n - m)
    den = jnp.sum(p_c, axis=1, keepdims=True) + jnp.sum(p_n, axis=1, keepdims=True)
    o_lat = (_dot(p_c.astype(BF16), kc) + _dot(p_n.astype(BF16), kn)) / den
    o_bf = o_lat.astype(BF16)
    for h in range(MLA_HEADS):
        o_ref[0, :, h * V_DIM:(h + 1) * V_DIM] = _dot(o_bf[h * t_new:(h + 1) * t_new, :],
                                                        wv_ref[:, h * V_DIM:(h + 1) * V_DIM])


def _attention_cached(ql, qr, cache_kv, cache_kr, ckv_new, kr_new, w_uv):
    nb, past = cache_kv.shape[0], cache_kv.shape[1]
    t_new = ckv_new.shape[1]
    return pl.pallas_call(
        functools.partial(_attn_cached_kernel, t_new=t_new),
        grid=(nb,),
        in_specs=[
            pl.BlockSpec((MLA_HEADS, t_new, KV_LORA), lambda b: (0, b, 0)),
            pl.BlockSpec((MLA_HEADS, t_new, LANES), lambda b: (0, b, 0)),
            pl.BlockSpec((1, past, KV_LORA), lambda b: (b, 0, 0)),
            pl.BlockSpec((1, past, QK_ROPE), lambda b: (b, 0, 0)),
            pl.BlockSpec((1, t_new, KV_LORA), lambda b: (b, 0, 0)),
            pl.BlockSpec((1, t_new, QK_ROPE), lambda b: (b, 0, 0)),
            _const_spec((KV_LORA, MLA_INNER)),
        ],
        out_specs=pl.BlockSpec((1, t_new, MLA_INNER), lambda b: (b, 0, 0)),
        out_shape=jax.ShapeDtypeStruct((nb, t_new, MLA_INNER), F32),
        compiler_params=_cparams(("arbitrary",)),
        name="attn_cached",
    )(ql, qr, cache_kv, cache_kr, ckv_new, kr_new, w_uv)


def _merge_kernel(y_ref, attn_ref, gm_ref, gs_ref, ga_ref, wssm_ref, wmla_ref, o_ref):
    y_ssm = _dot(y_ref[...], wssm_ref[...])
    y_mla = _dot((attn_ref[...] * _silu(gm_ref[...])).astype(BF16), wmla_ref[...])
    o_ref[...] = (_sigmoid(gs_ref[...]) * y_ssm + _sigmoid(ga_ref[...]) * y_mla).astype(o_ref.dtype)


def _merge(y_norm, attn, proj, p, tm):
    rows = y_norm.shape[0]
    gate = lambda k: pl.BlockSpec((tm, D_MODEL), lambda i: (i, COL_GATES // D_MODEL + k))
    return pl.pallas_call(
        _merge_kernel,
        grid=(rows // tm,),
        in_specs=[pl.BlockSpec((tm, SSM_INNER), lambda i: (i, 0)), pl.BlockSpec((tm, MLA_INNER), lambda i: (i, 0)),
                  gate(0), gate(1), gate(2),
                  _const_spec((SSM_INNER, D_MODEL)), _const_spec((MLA_INNER, D_MODEL))],
        out_specs=pl.BlockSpec((tm, D_MODEL), lambda i: (i, 0)),
        out_shape=jax.ShapeDtypeStruct((rows, D_MODEL), BF16),
        compiler_params=_cparams(("arbitrary",)),
        name="merge",
    )(y_norm, attn, proj, proj, proj, p["w_ssm_out"], p["w_mla_out"])


def _outproj_kernel(x_ref, mixed_ref, w_ref, nw_ref, o_ref):
    x_out = x_ref[...] + _dot(mixed_ref[...], w_ref[...])
    o_ref[...] = _rms(x_out, nw_ref[...])


def _outproj(x, mixed, p, tm):
    rows = x.shape[0]
    row = pl.BlockSpec((tm, D_MODEL), lambda i: (i, 0))
    return pl.pallas_call(
        _outproj_kernel,
        grid=(rows // tm,),
        in_specs=[row, row, _const_spec((D_MODEL, D_MODEL)), _const_spec((1, D_MODEL))],
        out_specs=row,
        out_shape=jax.ShapeDtypeStruct((rows, D_MODEL), F32),
        compiler_params=_cparams(("arbitrary",)),
        name="outproj",
    )(x, mixed, p["w_out"], p["final_norm_w"])


def _prepare(norm_in_w, w_in, conv_w, conv_b, dt_bias, a_log, d_skip, ssm_norm_w, w_ssm_out, q_norm_w, w_q_up,
             kv_norm_w, w_kv_up, w_mla_out, w_out, final_norm_w, chunk_lens):
    offs = np.cumsum((0, SSM_INNER, CONV_DIM, SSM_HEADS, Q_LORA, KV_LORA, QK_ROPE, MLA_INNER, D_MODEL, D_MODEL))
    w_in_bf = w_in.astype(BF16)
    col = lambda k: w_in_bf[:, int(offs[k]):int(offs[k + 1])]
    half = QK_ROPE // 2
    kr_w = col(5)
    zeros = lambda n: jnp.zeros((D_MODEL, n), BF16)
    w_tail = jnp.concatenate([
        col(6), col(7), col(8),
        col(3), col(4),
        col(2), zeros(SMALL_KR - SMALL_DT - SSM_HEADS),
        kr_w, zeros(SMALL_KR_SWAPPED - SMALL_KR - QK_ROPE),
        kr_w[:, half:], kr_w[:, :half], zeros(SMALL_COLS - SMALL_KR_SWAPPED - QK_ROPE),
        zeros(SEG_MLA - Q_LORA - KV_LORA - SMALL_COLS)], axis=1)

    wq = w_q_up.reshape(Q_LORA, MLA_HEADS, QK_NOPE + QK_ROPE)
    wq_rope = wq[..., QK_NOPE:]
    zq = jnp.zeros((Q_LORA, MLA_HEADS, 64), w_q_up.dtype)
    wq_a = jnp.concatenate([wq, zq], axis=-1).reshape(Q_LORA, MLA_HEADS * QK_PAD).astype(BF16)
    wq_b = jnp.concatenate([wq_rope[..., half:], wq_rope[..., :half], zq], axis=-1)
    wq_b = wq_b.reshape(Q_LORA, MLA_HEADS * LANES).astype(BF16)

    wkv = w_kv_up.reshape(KV_LORA, MLA_HEADS, QK_NOPE + V_DIM)
    w_uk = wkv[..., :QK_NOPE].reshape(KV_LORA, MLA_HEADS * QK_NOPE).astype(BF16)
    w_uv_h = wkv[..., QK_NOPE:]
    w_uv = w_uv_h.reshape(KV_LORA, MLA_INNER).astype(BF16)
    w_uv_t = w_uv.T

    head_of_lane = np.arange(SSM_INNER) // SSM_HEAD_DIM
    p = dict(
        norm_in_w=norm_in_w.reshape(1, D_MODEL), w_main=w_in_bf, w_tail=w_tail,
        conv_w=conv_w, conv_b=conv_b.reshape(1, CONV_DIM), dt_bias=dt_bias.reshape(1, SSM_HEADS),
        a_log=a_log.reshape(1, SSM_HEADS), d_skip_p=jnp.repeat(d_skip, SSM_HEAD_DIM).reshape(1, SSM_INNER),
        ssm_norm_w=ssm_norm_w.reshape(1, SSM_INNER), w_ssm_out=w_ssm_out.astype(BF16),
        q_norm_w=q_norm_w.reshape(1, Q_LORA), kv_norm_w=kv_norm_w.reshape(1, KV_LORA),
        wq_a=wq_a, wq_b=wq_b, w_uk=w_uk, w_uv=w_uv, w_uv_t=w_uv_t,
        w_mla_out=w_mla_out.astype(BF16), w_out=w_out.astype(BF16),
        final_norm_w=final_norm_w.reshape(1, D_MODEL),
        e_p=jnp.asarray(np.tile(head_of_lane[None, :] == np.arange(SSM_HEADS)[:, None], (3, 1)), BF16),
    )
    for q in chunk_lens:
        head_of_key_lane = np.arange(SSM_HEADS * q) // q
        p["e_q%d" % q] = jnp.asarray(np.tile(head_of_key_lane[None, :] == np.arange(SSM_HEADS)[:, None], (3, 1)),
                                     BF16)
    return p


def _rope_tables(pos):
    half = QK_ROPE // 2
    inv = 1.0 / (ROPE_THETA ** (jnp.arange(half, dtype=F32) * (2.0 / QK_ROPE)))
    ang = pos.astype(F32)[:, None] * inv[None, :]
    cos, sin = jnp.cos(ang), jnp.sin(ang)
    pad = jnp.zeros((pos.shape[0], LANES - QK_ROPE), F32)
    return jnp.concatenate([cos, cos, pad], axis=1), jnp.concatenate([-sin, sin, pad], axis=1)


class _Tiles(NamedTuple):
    inproj: int
    ssd_chunks: int
    prep: int
    attn_q: int
    merge: int
    outproj: int


def _tiles(seq, chunk):
    return _Tiles(inproj=min(1024, seq), ssd_chunks=4 if seq % (4 * chunk) == 0 else 1,
                  prep=min(512, seq // ATTN_SUB_TILES), attn_q=min(512, seq), merge=min(256, seq),
                  outproj=min(512, seq))


def kernel(x_prompt, x_sample, cache_kv_latent, cache_k_rope, state_ssm, state_conv, norm_in_w, w_in, conv_w,
           conv_b, dt_bias, a_log, d_skip, ssm_norm_w, w_ssm_out, q_norm_w, w_q_up, kv_norm_w, w_kv_up, w_mla_out,
           w_out, final_norm_w):
    depth = w_in.shape[0]
    assert depth == 1 and x_prompt.shape[0] == 1
    seq = x_prompt.shape[1]
    nb, t_new = x_sample.shape[0], x_sample.shape[1]
    past = cache_kv_latent.shape[2]
    q_prompt, q_sample = min(CHUNK, seq), min(CHUNK, t_new)
    p = _prepare(norm_in_w[0], w_in[0], conv_w[0], conv_b[0], dt_bias[0], a_log[0], d_skip[0], ssm_norm_w[0],
                 w_ssm_out[0], q_norm_w[0], w_q_up[0], kv_norm_w[0], w_kv_up[0], w_mla_out[0], w_out[0],
                 final_norm_w, sorted({q_prompt, q_sample}))

    t = _tiles(seq, q_prompt)
    xp = x_prompt[0]
    proj = _inproj(xp, p["norm_in_w"], p["w_main"], p["w_tail"], tm=t.inproj)
    y_norm, ssm_p, conv_p = _ssd(proj[None],
                                 jnp.zeros((1, CONV_WIDTH - 1, CONV_DIM), F32),
                                 jnp.zeros((1, SSM_INNER, SSM_STATE), F32), p, q_prompt, nsub=t.ssd_chunks)
    ct, sn = _rope_tables(jnp.arange(seq))
    ckv_p, kr_p, qt, kn, krb, vt = _mla_prompt(proj, ct, sn, p, tm=t.prep)
    attn = _attention_prompt(qt, kn, krb, vt, tq=t.attn_q)
    mixed = _merge(y_norm[0], attn, proj, p, tm=t.merge)
    y_prompt = _outproj(xp, mixed, p, tm=t.outproj)

    rows_s = nb * t_new
    xs = x_sample.reshape(rows_s, D_MODEL)
    proj_s = _inproj(xs, p["norm_in_w"], p["w_main"], p["w_tail"], tm=rows_s)
    y_norm_s, ssm_s, conv_s = _ssd(proj_s.reshape(nb, t_new, PROJ_COLS), state_conv[0],
                                   state_ssm[0].reshape(nb, SSM_INNER, SSM_STATE), p, q_sample, nsub=1)
    ct_s, sn_s = _rope_tables(past + jnp.arange(t_new))
    ckv_s, kr_s, ql, qr = _mla_sample(proj_s, jnp.tile(ct_s, (nb, 1)), jnp.tile(sn_s, (nb, 1)), p)
    ckv_s = ckv_s.reshape(nb, t_new, KV_LORA)
    kr_s = kr_s.reshape(nb, t_new, QK_ROPE)
    attn_s = _attention_cached(ql, qr, cache_kv_latent[0], cache_k_rope[0], ckv_s, kr_s, p["w_uv"])
    mixed_s = _merge(y_norm_s.reshape(rows_s, SSM_INNER), attn_s.reshape(rows_s, MLA_INNER), proj_s, p, tm=rows_s)
    y_sample = _outproj(xs, mixed_s, p, tm=rows_s).reshape(nb, t_new, D_MODEL)

    hshape = (SSM_HEADS, SSM_HEAD_DIM, SSM_STATE)
    return (y_prompt[None], y_sample,
            ckv_p[None, None], kr_p[None, None], ssm_p.reshape((1, 1) + hshape), conv_p[None],
            ckv_s[None], kr_s[None], ssm_s.reshape((1, nb) + hshape), conv_s[None])
```

```python
import functools
import math
from typing import NamedTuple

import jax
import jax.numpy as jnp
import numpy as np
from jax import lax
from jax.experimental import pallas as pl
from jax.experimental.pallas import tpu as pltpu

F32 = jnp.float32
BF16 = jnp.bfloat16

D_MODEL = 2048
CHUNK = 64
SSM_INNER = 4096
SSM_HEAD_DIM = 64
SSM_HEADS = 64
SSM_GROUPS = 8
SSM_STATE = 128
CONV_WIDTH = 4
CONV_DIM = SSM_INNER + 2 * SSM_GROUPS * SSM_STATE
GROUP_LANES = SSM_INNER // SSM_GROUPS
MLA_HEADS = 16
Q_LORA = 512
KV_LORA = 512
QK_NOPE = 128
QK_ROPE = 64
V_DIM = 128
MLA_INNER = MLA_HEADS * V_DIM
ROPE_THETA = 10000.0
EPS = 1e-6
QK_PAD = 256
SCALE = (QK_NOPE + QK_ROPE) ** -0.5
LOG2E = math.log2(math.e)
V_ROWS = V_DIM + 16
LANES = 128
V7X_VMEM_BYTES = 64 * 1024 * 1024
VMEM_LIMIT = V7X_VMEM_BYTES * 7 // 8
ATTN_SUB_TILES = 2

IN_TN = 1024
SEG_MLA = 2048
SMALL_COLS = 512
SMALL_DT, SMALL_KR, SMALL_KR_SWAPPED = 0, 128, 256
COL_XBC = 0
COL_GATES = COL_XBC + CONV_DIM
COL_Z = COL_GATES + MLA_INNER + 2 * D_MODEL
COL_MLA = COL_Z + SSM_INNER
PROJ_COLS = COL_MLA + SEG_MLA
COL_SMALL = COL_MLA + Q_LORA + KV_LORA


def _cparams(sem, vmem=VMEM_LIMIT):
    return pltpu.CompilerParams(dimension_semantics=sem, vmem_limit_bytes=vmem)


def _const_spec(shape):
    nd = len(shape)
    return pl.BlockSpec(shape, lambda *_: (0,) * nd, pipeline_mode=pl.Buffered(1))


def _sigmoid(x):
    return 0.5 + 0.5 * jnp.tanh(0.5 * x)


def _silu(x):
    h = 0.5 * x
    return h + h * jnp.tanh(h)


def _rms(x, w):
    return x * lax.rsqrt(jnp.mean(x * x, axis=-1, keepdims=True) + EPS) * w


def _dot(a, b):
    return jnp.dot(a, b, preferred_element_type=F32)


def _dot_nt(a, b):
    return lax.dot_general(a, b, (((1,), (1,)), ((), ())), preferred_element_type=F32)


def _dot_tn(a, b):
    return lax.dot_general(a, b, (((0,), (0,)), ((), ())), preferred_element_type=F32)


def _split3(v):
    hi = v.astype(BF16)
    r1 = v - hi.astype(F32)
    mid = r1.astype(BF16)
    lo = (r1 - mid.astype(F32)).astype(BF16)
    return hi, mid, lo


def _inproj_kernel(x_ref, nw_ref, wm_ref, wt_ref, o_ref, h_ref, *, from_main):
    j = pl.program_id(1)

    @pl.when(j == 0)
    def _():
        h_ref[...] = _rms(x_ref[...], nw_ref[...]).astype(BF16)

    is_main = functools.reduce(jnp.logical_or, [(j >= lo) & (j < hi) for lo, hi in from_main])

    @pl.when(is_main)
    def _():
        o_ref[...] = _dot(h_ref[...], wm_ref[...])

    @pl.when(jnp.logical_not(is_main))
    def _():
        o_ref[...] = _dot(h_ref[...], wt_ref[...])


def _inproj(x, norm_w, w_main, w_tail, tm):
    rows = x.shape[0]
    t = lambda cols: cols // IN_TN
    n_xbc, n_g, n_z = t(CONV_DIM), t(MLA_INNER + 2 * D_MODEL), t(SSM_INNER)
    j_g, j_z, j_mla = t(COL_GATES), t(COL_Z), t(COL_MLA)
    src_xbc, src_z = t(SSM_INNER), 0

    def main_map(i, j):
        return 0, jnp.where(j < j_z, src_xbc + jnp.minimum(j, n_xbc - 1), src_z + jnp.minimum(j - j_z, n_z - 1))

    def tail_map(i, j):
        return 0, jnp.where(j < j_mla, jnp.clip(j - j_g, 0, n_g - 1), n_g + j - j_mla)

    return pl.pallas_call(
        functools.partial(_inproj_kernel, from_main=((0, n_xbc), (j_z, j_z + n_z))),
        grid=(rows // tm, t(PROJ_COLS)),
        in_specs=[
            pl.BlockSpec((tm, D_MODEL), lambda i, j: (i, 0)),
            pl.BlockSpec((1, D_MODEL), lambda i, j: (0, 0)),
            pl.BlockSpec((D_MODEL, IN_TN), main_map),
            pl.BlockSpec((D_MODEL, IN_TN), tail_map),
        ],
        out_specs=pl.BlockSpec((tm, IN_TN), lambda i, j: (i, j)),
        out_shape=jax.ShapeDtypeStruct((rows, PROJ_COLS), F32),
        scratch_shapes=[pltpu.VMEM((tm, D_MODEL), BF16)],
        compiler_params=_cparams(("arbitrary", "arbitrary")),
        name="inproj",
    )(x, norm_w, w_main, w_tail)


CONV_PAD = 8
CONV_SLAB = 256


def _ssd_kernel(xbc_ref, z_ref, sm_ref, cprev_ref, s0_ref, cw_ref, cb_ref, dtb_ref, alog_ref, dsk_ref,
                nw_ref, ep_ref, eq_ref, y_ref, sout_ref, cout_ref, st_ref, xpad_ref, xc_ref, exp_p_ref,
                exp_q_ref, yd_ref, *, q, nsub):
    c = pl.program_id(1)
    nc = pl.num_programs(1)
    rows = nsub * q
    hp = LANES // q
    tiles_per_group = (SSM_HEADS // SSM_GROUPS) // hp
    tile_w = hp * SSM_HEAD_DIM
    lo = CONV_PAD - (CONV_WIDTH - 1)

    @pl.when(c == 0)
    def _():
        st_ref[...] = s0_ref[0].T
        xpad_ref[0:lo, :] = jnp.zeros((lo, CONV_DIM), F32)
        xpad_ref[lo:CONV_PAD, :] = cprev_ref[0]

    assert CONV_WIDTH == 4
    xpad_ref[CONV_PAD:CONV_PAD + rows, :] = xbc_ref[0]
    for s in range(0, CONV_DIM, CONV_SLAB):
        xw = xpad_ref[:, s:s + CONV_SLAB]
        x1 = pltpu.roll(xw, 1, 0)
        w = [cw_ref[k:k + 1, s:s + CONV_SLAB] for k in range(CONV_WIDTH)]
        u = (xw * w[3] + x1 * w[2]) + pltpu.roll(xw * w[1] + x1 * w[0], 2, 0)
        xc_ref[:, s:s + CONV_SLAB] = _silu(u[CONV_PAD:, :] + cb_ref[:, s:s + CONV_SLAB])
    tail_rows = xpad_ref[CONV_PAD + rows - (CONV_WIDTH - 1):CONV_PAD + rows, :]
    xpad_ref[lo:CONV_PAD, :] = tail_rows

    ri = lax.broadcasted_iota(jnp.int32, (q, q), 0)
    ci = lax.broadcasted_iota(jnp.int32, (q, q), 1)
    tril = jnp.where(ci <= ri, 1.0, 0.0).astype(BF16)
    r128 = lax.broadcasted_iota(jnp.int32, (q, LANES), 0)
    c128 = lax.broadcasted_iota(jnp.int32, (q, LANES), 1)
    key128 = c128 & (q - 1)
    causal = key128 <= r128
    diag = key128 == r128
    br = lax.broadcasted_iota(jnp.int32, (LANES, tile_w), 0)
    bc = lax.broadcasted_iota(jnp.int32, (LANES, tile_w), 1)
    blockdiag = (br // q) == (bc // SSM_HEAD_DIM)
    neg_a = -jnp.exp(alog_ref[...])

    for u in range(nsub):
        r0 = u * q
        xdt = sm_ref[0, r0:r0 + q, SMALL_DT:SMALL_DT + SSM_HEADS] + dtb_ref[...]
        dt = jnp.maximum(xdt, 0.0) + jnp.log1p(jnp.exp(-jnp.abs(xdt)))
        h3 = _split3(dt * neg_a)
        acs = _dot(tril, h3[0]) + _dot(tril, h3[1]) + _dot(tril, h3[2])

        def expand(v, e_ref):
            pieces = jnp.concatenate([piece.astype(F32) for piece in _split3(v)], axis=1).astype(BF16)
            return _dot(pieces, e_ref[...])

        exp_q_ref[u] = expand(jnp.concatenate([acs, dt], axis=0), eq_ref)
        exp_p_ref[u] = expand(jnp.concatenate([jnp.exp(acs), jnp.exp(acs[q - 1:q, :] - acs) * dt], axis=0), ep_ref)

        for g in range(SSM_GROUPS):
            gs = g * GROUP_LANES
            bs = SSM_INNER + g * SSM_STATE
            cs = SSM_INNER + (SSM_GROUPS + g) * SSM_STATE
            b_bf = xc_ref[r0:r0 + q, bs:bs + SSM_STATE].astype(BF16)
            c_bf = xc_ref[r0:r0 + q, cs:cs + SSM_STATE].astype(BF16)
            cb_t = _dot_nt(c_bf, jnp.concatenate([b_bf] * hp, axis=0))

            for tt in range(tiles_per_group):
                t = g * tiles_per_group + tt
                a_t = exp_q_ref[u, 0:q, t * LANES:(t + 1) * LANES]
                d_t = exp_q_ref[u, q:2 * q, t * LANES:(t + 1) * LANES]
                a_key = jnp.sum(jnp.where(diag, a_t, 0.0), axis=0, keepdims=True)
                d_key = jnp.sum(jnp.where(diag, d_t, 0.0), axis=0, keepdims=True)
                w_t = jnp.exp(jnp.where(causal, a_t - a_key, -jnp.inf)) * d_key * cb_t
                x_t = xc_ref[r0:r0 + q, t * tile_w:(t + 1) * tile_w]
                rhs = jnp.where(blockdiag, jnp.concatenate([x_t] * hp, axis=0), 0.0).astype(BF16)
                yd_ref[r0:r0 + q, t * tile_w:(t + 1) * tile_w] = _dot(w_t.astype(BF16), rhs)

            decay = exp_p_ref[u, 0:q, gs:gs + GROUP_LANES]
            tail = exp_p_ref[u, q:2 * q, gs:gs + GROUP_LANES]
            xs_g = xc_ref[r0:r0 + q, gs:gs + GROUP_LANES]
            st_g = st_ref[:, gs:gs + GROUP_LANES]
            y_off = _dot(c_bf, st_g.astype(BF16)) * decay
            xt = (xs_g * tail).astype(BF16)
            st_ref[:, gs:gs + GROUP_LANES] = st_g * decay[q - 1:q, :] + _dot_tn(b_bf, xt)

            y = yd_ref[r0:r0 + q, gs:gs + GROUP_LANES] + y_off + dsk_ref[:, gs:gs + GROUP_LANES] * xs_g
            y = y * _silu(z_ref[0, r0:r0 + q, gs:gs + GROUP_LANES])
            y = y * lax.rsqrt(jnp.mean(y * y, axis=-1, keepdims=True) + EPS)
            y_ref[0, r0:r0 + q, gs:gs + GROUP_LANES] = (y * nw_ref[:, gs:gs + GROUP_LANES]).astype(y_ref.dtype)

    @pl.when(c == nc - 1)
    def _():
        sout_ref[0] = st_ref[...].T
        cout_ref[0] = tail_rows


def _ssd(proj, conv_prev, s0, p, q, nsub):
    nb, lb = proj.shape[0], proj.shape[1]
    rows = q * nsub
    nc = lb // rows
    kq = SSM_HEADS * q
    in_specs = [
        pl.BlockSpec((1, rows, CONV_DIM), lambda b, c: (b, c, COL_XBC // CONV_DIM)),
        pl.BlockSpec((1, rows, SSM_INNER), lambda b, c: (b, c, COL_Z // SSM_INNER)),
        pl.BlockSpec((1, rows, SMALL_COLS), lambda b, c: (b, c, COL_SMALL // SMALL_COLS)),
        pl.BlockSpec((1, CONV_WIDTH - 1, CONV_DIM), lambda b, c: (b, 0, 0)),
        pl.BlockSpec((1, SSM_INNER, SSM_STATE), lambda b, c: (b, 0, 0)),
        _const_spec((CONV_WIDTH, CONV_DIM)),
        _const_spec((1, CONV_DIM)),
        _const_spec((1, SSM_HEADS)),
        _const_spec((1, SSM_HEADS)),
        _const_spec((1, SSM_INNER)),
        _const_spec((1, SSM_INNER)),
        _const_spec((3 * SSM_HEADS, SSM_INNER)),
        _const_spec((3 * SSM_HEADS, kq)),
    ]
    out_specs = [
        pl.BlockSpec((1, rows, SSM_INNER), lambda b, c: (b, c, 0)),
        pl.BlockSpec((1, SSM_INNER, SSM_STATE), lambda b, c: (b, 0, 0)),
        pl.BlockSpec((1, CONV_WIDTH - 1, CONV_DIM), lambda b, c: (b, 0, 0)),
    ]
    out_shape = [
        jax.ShapeDtypeStruct((nb, lb, SSM_INNER), BF16),
        jax.ShapeDtypeStruct((nb, SSM_INNER, SSM_STATE), F32),
        jax.ShapeDtypeStruct((nb, CONV_WIDTH - 1, CONV_DIM), F32),
    ]
    scratch = [
        pltpu.VMEM((SSM_STATE, SSM_INNER), F32),
        pltpu.VMEM((CONV_PAD + rows, CONV_DIM), F32),
        pltpu.VMEM((rows, CONV_DIM), F32),
        pltpu.VMEM((nsub, 2 * q, SSM_INNER), F32),
        pltpu.VMEM((nsub, 2 * q, kq), F32),
        pltpu.VMEM((rows, SSM_INNER), F32),
    ]
    return pl.pallas_call(
        functools.partial(_ssd_kernel, q=q, nsub=nsub),
        grid=(nb, nc),
        in_specs=in_specs,
        out_specs=out_specs,
        out_shape=out_shape,
        scratch_shapes=scratch,
        compiler_params=_cparams(("arbitrary", "arbitrary")),
        name="ssd",
    )(proj, proj, proj, conv_prev, s0, p["conv_w"], p["conv_b"], p["dt_bias"], p["a_log"], p["d_skip_p"],
      p["ssm_norm_w"], p["e_p"], p["e_q%d" % q])


def _mla_latents(mla_ref, ct_ref, sn_ref, qnw_ref, kvnw_ref, ckv_ref, kr_ref):
    m = mla_ref[...]
    cqn = _rms(m[:, 0:Q_LORA], qnw_ref[...])
    ckv = _rms(m[:, Q_LORA:Q_LORA + KV_LORA], kvnw_ref[...])
    ckv_ref[...] = ckv
    small = m[:, Q_LORA + KV_LORA:]
    ct, sn = ct_ref[...], sn_ref[...]
    kr128 = (small[:, SMALL_KR:SMALL_KR + LANES] * ct
             + small[:, SMALL_KR_SWAPPED:SMALL_KR_SWAPPED + LANES] * sn)
    kr_ref[...] = kr128[:, 0:QK_ROPE]
    return cqn, ckv, kr128, ct, sn


def _mla_prompt_kernel(mla_ref, ct_ref, sn_ref, qnw_ref, kvnw_ref, wqat_ref, wqbt_ref, wk_ref,
                       wvt_ref, ckv_ref, kr_ref, qt_ref, kn_ref, krb_ref, vt_ref):
    cqn, ckv, kr128, ct, sn = _mla_latents(mla_ref, ct_ref, sn_ref, qnw_ref, kvnw_ref, ckv_ref, kr_ref)
    cqn_t = cqn.T.astype(BF16)
    ckv_t = ckv.T.astype(BF16)
    ckv_bf = ckv.astype(BF16)
    krb_ref[...] = kr128.astype(BF16)
    ctt, snt = ct.T, sn.T
    tm = ckv_bf.shape[0]
    ones_row = jnp.where(lax.broadcasted_iota(jnp.int32, (V_ROWS - V_DIM, tm), 0) == 0, 1.0, 0.0).astype(BF16)
    qs = SCALE * LOG2E
    for h in range(MLA_HEADS):
        qa = _dot(wqat_ref[h * QK_PAD:(h + 1) * QK_PAD, :], cqn_t)
        qb = _dot(wqbt_ref[h * LANES:(h + 1) * LANES, :], cqn_t)
        qt_ref[h, 0:QK_NOPE, :] = (qa[0:QK_NOPE] * qs).astype(BF16)
        qt_ref[h, QK_NOPE:QK_PAD, :] = ((qa[QK_NOPE:QK_PAD] * ctt + qb * snt) * qs).astype(BF16)
        vt_ref[h, 0, 0:V_DIM, :] = _dot(wvt_ref[h * V_DIM:(h + 1) * V_DIM, :], ckv_t).astype(BF16)
        vt_ref[h, 0, V_DIM:V_ROWS, :] = ones_row
    for g in range(MLA_HEADS // 2):
        kn2 = _dot(ckv_bf, wk_ref[:, 2 * g * QK_NOPE:(2 * g + 2) * QK_NOPE]).astype(BF16)
        kn_ref[2 * g] = kn2[:, 0:QK_NOPE]
        kn_ref[2 * g + 1] = kn2[:, QK_NOPE:2 * QK_NOPE]


def _mla_prompt(proj, ct, sn, p, tm):
    rows = proj.shape[0]
    row = lambda w: pl.BlockSpec((tm, w), lambda i: (i, 0))
    mla = pl.BlockSpec((tm, SEG_MLA), lambda i: (i, COL_MLA // SEG_MLA))
    return pl.pallas_call(
        _mla_prompt_kernel,
        grid=(rows // tm,),
        in_specs=[mla, row(LANES), row(LANES), _const_spec((1, Q_LORA)),
                  _const_spec((1, KV_LORA)),
                  _const_spec((MLA_HEADS * QK_PAD, Q_LORA)), _const_spec((MLA_HEADS * LANES, Q_LORA)),
                  _const_spec((KV_LORA, MLA_HEADS * QK_NOPE)), _const_spec((MLA_INNER, KV_LORA))],
        out_specs=[row(KV_LORA), row(QK_ROPE),
                   pl.BlockSpec((MLA_HEADS, QK_PAD, tm), lambda i: (0, 0, i)),
                   pl.BlockSpec((MLA_HEADS, tm, QK_NOPE), lambda i: (0, i, 0)), row(LANES),
                   pl.BlockSpec((MLA_HEADS, 1, V_ROWS, tm), lambda i: (0, i, 0, 0))],
        out_shape=[jax.ShapeDtypeStruct((rows, KV_LORA), F32), jax.ShapeDtypeStruct((rows, QK_ROPE), F32),
                   jax.ShapeDtypeStruct((MLA_HEADS, QK_PAD, rows), BF16),
                   jax.ShapeDtypeStruct((MLA_HEADS, rows, QK_NOPE), BF16),
                   jax.ShapeDtypeStruct((rows, LANES), BF16),
                   jax.ShapeDtypeStruct((MLA_HEADS, rows // tm, V_ROWS, tm), BF16)],
        compiler_params=_cparams(("arbitrary",)),
        name="mla_prep_prompt",
    )(proj, ct, sn, p["q_norm_w"], p["kv_norm_w"], p["wq_a"].T, p["wq_b"].T, p["w_uk"], p["w_uv_t"])


def _mla_sample_kernel(mla_ref, ct_ref, sn_ref, qnw_ref, kvnw_ref, wqa_ref, wqb_ref, wk_ref,
                       ckv_ref, kr_ref, ql_ref, qr_ref):
    cqn, _, _, ct, sn = _mla_latents(mla_ref, ct_ref, sn_ref, qnw_ref, kvnw_ref, ckv_ref, kr_ref)
    cqn = cqn.astype(BF16)
    qa = _dot(cqn, wqa_ref[...])
    qb = _dot(cqn, wqb_ref[...])
    for h in range(MLA_HEADS):
        o = h * QK_PAD
        q_rope = qa[:, o + QK_NOPE:o + QK_PAD] * ct + qb[:, h * LANES:(h + 1) * LANES] * sn
        qr_ref[h] = (q_rope * SCALE).astype(BF16)
        q_nope = qa[:, o:o + QK_NOPE].astype(BF16)
        q_lat = _dot_nt(q_nope, wk_ref[:, h * QK_NOPE:(h + 1) * QK_NOPE])
        ql_ref[h] = (q_lat * SCALE).astype(BF16)


def _mla_sample(proj, ct, sn, p):
    rows = proj.shape[0]
    full = lambda w: pl.BlockSpec((rows, w), lambda i: (0, 0))
    mla = pl.BlockSpec((rows, SEG_MLA), lambda i: (0, COL_MLA // SEG_MLA))
    return pl.pallas_call(
        _mla_sample_kernel,
        grid=(1,),
        in_specs=[mla, full(LANES), full(LANES), _const_spec((1, Q_LORA)), _const_spec((1, KV_LORA)),
                  _const_spec((Q_LORA, MLA_HEADS * QK_PAD)), _const_spec((Q_LORA, MLA_HEADS * LANES)),
                  _const_spec((KV_LORA, MLA_HEADS * QK_NOPE))],
        out_specs=[full(KV_LORA), full(QK_ROPE),
                   pl.BlockSpec((MLA_HEADS, rows, KV_LORA), lambda i: (0, 0, 0)),
                   pl.BlockSpec((MLA_HEADS, rows, LANES), lambda i: (0, 0, 0))],
        out_shape=[jax.ShapeDtypeStruct((rows, KV_LORA), F32), jax.ShapeDtypeStruct((rows, QK_ROPE), F32),
                   jax.ShapeDtypeStruct((MLA_HEADS, rows, KV_LORA), BF16),
                   jax.ShapeDtypeStruct((MLA_HEADS, rows, LANES), BF16)],
        compiler_params=_cparams(("arbitrary",)),
        name="mla_prep_sample",
    )(proj, ct, sn, p["q_norm_w"], p["kv_norm_w"], p["wq_a"], p["wq_b"], p["w_uk"])


def _attn_kernel(qi_ref, kj_ref, qt_ref, kn_ref, kr_ref, vt_ref, o_ref, m_ref, acc_ref, s0_ref, s1_ref, p0_ref,
                 p1_ref, a0_ref, a1_ref, *, tq, tks):
    s_refs, p_refs, a_refs = (s0_ref, s1_ref), (p0_ref, p1_ref), (a0_ref, a1_ref)
    n_sub = ATTN_SUB_TILES
    t = pl.program_id(0)
    i = qi_ref[t]
    j = kj_ref[t]
    tk = n_sub * tks
    last_j = ((i + 1) * tq - 1) // tk

    @pl.when(j == 0)
    def _():
        m_ref[...] = jnp.full(m_ref.shape, -jnp.inf, F32)
        acc_ref[...] = jnp.zeros(acc_ref.shape, F32)

    def step(masked, sub_tiles):
        if masked:
            q_chunk = (i * tq + lax.broadcasted_iota(jnp.int32, (tks, tq), 1)) // CHUNK
            k_chunk = (j * tk + lax.broadcasted_iota(jnp.int32, (tks, tq), 0)) // CHUNK
            visible = {c: k_chunk + (c * tks) // CHUNK <= q_chunk for c in sub_tiles}

        def scores(h, c, b):
            k_h = jnp.concatenate([kn_ref[h, c * tks:(c + 1) * tks, :], kr_ref[c * tks:(c + 1) * tks, :]], axis=1)
            s_refs[b][...] = _dot(k_h, qt_ref[h])

        def softmax(h, c, b):
            s = s_refs[b][...]
            if masked:
                s = jnp.where(visible[c], s, -jnp.inf)
            m_prev = m_ref[h]
            m_new = jnp.maximum(m_prev, jnp.max(s, axis=0, keepdims=True))
            m_ref[h] = m_new
            p_refs[b][...] = jnp.exp2(s - m_new).astype(BF16)
            a_refs[b][...] = jnp.exp2(m_prev - m_new)

        def values(h, c, b):
            acc_ref[h] = acc_ref[h] * a_refs[b][...] + _dot(vt_ref[h, c], p_refs[b][...])

        items = [(h, c) for h in range(MLA_HEADS) for c in sub_tiles]
        stages = (scores, softmax, values)
        for n in range(len(items) + len(stages) - 1):
            for lag, stage in enumerate(stages):
                if 0 <= n - lag < len(items):
                    stage(*items[n - lag], (n - lag) % 2)

    skip_upper = (j * tk + tks) >= (i + 1) * tq

    @pl.when(j < last_j)
    def _():
        step(False, tuple(range(n_sub)))

    @pl.when((j == last_j) & jnp.logical_not(skip_upper))
    def _():
        step(True, tuple(range(n_sub)))

    @pl.when((j == last_j) & skip_upper)
    def _():
        step(True, (0,))

    @pl.when(j == last_j)
    def _():
        for h in range(MLA_HEADS):
            den = acc_ref[h, V_DIM:V_DIM + 1, :]
            o_ref[:, h * V_DIM:(h + 1) * V_DIM] = (acc_ref[h, 0:V_DIM, :] * (1.0 / den)).T


def _attention_prompt(qt, kn, kr, vt, tq):
    rows = kn.shape[1]
    n_sub, tks = ATTN_SUB_TILES, vt.shape[3]
    tk = tks * n_sub
    qi, kj = [], []
    for i in range(rows // tq):
        for j in range(((i + 1) * tq - 1) // tk + 1):
            qi.append(i)
            kj.append(j)
    qi = jnp.asarray(np.array(qi, np.int32))
    kj = jnp.asarray(np.array(kj, np.int32))
    grid_spec = pltpu.PrefetchScalarGridSpec(
        num_scalar_prefetch=2,
        grid=(int(qi.shape[0]),),
        in_specs=[
            pl.BlockSpec((MLA_HEADS, QK_PAD, tq), lambda t, qi, kj: (0, 0, qi[t])),
            pl.BlockSpec((MLA_HEADS, tk, QK_NOPE), lambda t, qi, kj: (0, kj[t], 0)),
            pl.BlockSpec((tk, LANES), lambda t, qi, kj: (kj[t], 0)),
            pl.BlockSpec((MLA_HEADS, n_sub, V_ROWS, tks), lambda t, qi, kj: (0, kj[t], 0, 0)),
        ],
        out_specs=pl.BlockSpec((tq, MLA_INNER), lambda t, qi, kj: (qi[t], 0)),
        scratch_shapes=[
            pltpu.VMEM((MLA_HEADS, 1, tq), F32),
            pltpu.VMEM((MLA_HEADS, V_ROWS, tq), F32),
            pltpu.VMEM((tks, tq), F32), pltpu.VMEM((tks, tq), F32),
            pltpu.VMEM((tks, tq), BF16), pltpu.VMEM((tks, tq), BF16),
            pltpu.VMEM((1, tq), F32), pltpu.VMEM((1, tq), F32),
        ],
    )
    return pl.pallas_call(
        functools.partial(_attn_kernel, tq=tq, tks=tks),
        grid_spec=grid_spec,
        out_shape=jax.ShapeDtypeStruct((rows, MLA_INNER), F32),
        compiler_params=_cparams(("arbitrary",)),
        name="attn_prompt",
    )(qi, kj, qt, kn, kr, vt)


def _attn_cached_kernel(ql_ref, qr_ref, ckv_c_ref, kr_c_ref, ckv_n_ref, kr_n_ref, wv_ref, o_ref, *, t_new):
    rows = MLA_HEADS * t_new
    ql = ql_ref[...].reshape(rows, KV_LORA)
    qr = qr_ref[...].reshape(rows, LANES)[:, 0:QK_ROPE]
    kc = ckv_c_ref[0].astype(BF16)
    kn = ckv_n_ref[0].astype(BF16)
    s_c = _dot_nt(ql, kc) + _dot_nt(qr, kr_c_ref[0].astype(BF16))
    s_n = _dot_nt(ql, kn) + _dot_nt(qr, kr_n_ref[0].astype(BF16))
    m = jnp.maximum(jnp.max(s_c, axis=1, keepdims=True), jnp.max(s_n, axis=1, keepdims=True))
    p_c = jnp.exp(s_c - m)
    p_n = jnp.exp(s_n - m)
    den = jnp.sum(p_c, axis=1, keepdims=True) + jnp.sum(p_n, axis=1, keepdims=True)
    o_lat = (_dot(p_c.astype(BF16), kc) + _dot(p_n.astype(BF16), kn)) / den
    o_bf = o_lat.astype(BF16)
    for h in range(MLA_HEADS):
        o_ref[0, :, h * V_DIM:(h + 1) * V_DIM] = _dot(o_bf[h * t_new:(h + 1) * t_new, :],
                                                        wv_ref[:, h * V_DIM:(h + 1) * V_DIM])


def _attention_cached(ql, qr, cache_kv, cache_kr, ckv_new, kr_new, w_uv):
    nb, past = cache_kv.shape[0], cache_kv.shape[1]
    t_new = ckv_new.shape[1]
    return pl.pallas_call(
        functools.partial(_attn_cached_kernel, t_new=t_new),
        grid=(nb,),
        in_specs=[
            pl.BlockSpec((MLA_HEADS, t_new, KV_LORA), lambda b: (0, b, 0)),
            pl.BlockSpec((MLA_HEADS, t_new, LANES), lambda b: (0, b, 0)),
            pl.BlockSpec((1, past, KV_LORA), lambda b: (b, 0, 0)),
            pl.BlockSpec((1, past, QK_ROPE), lambda b: (b, 0, 0)),
            pl.BlockSpec((1, t_new, KV_LORA), lambda b: (b, 0, 0)),
            pl.BlockSpec((1, t_new, QK_ROPE), lambda b: (b, 0, 0)),
            _const_spec((KV_LORA, MLA_INNER)),
        ],
        out_specs=pl.BlockSpec((1, t_new, MLA_INNER), lambda b: (b, 0, 0)),
        out_shape=jax.ShapeDtypeStruct((nb, t_new, MLA_INNER), F32),
        compiler_params=_cparams(("arbitrary",)),
        name="attn_cached",
    )(ql, qr, cache_kv, cache_kr, ckv_new, kr_new, w_uv)


def _merge_kernel(y_ref, attn_ref, gm_ref, gs_ref, ga_ref, wssm_ref, wmla_ref, o_ref):
    y_ssm = _dot(y_ref[...], wssm_ref[...])
    y_mla = _dot((attn_ref[...] * _silu(gm_ref[...])).astype(BF16), wmla_ref[...])
    o_ref[...] = (_sigmoid(gs_ref[...]) * y_ssm + _sigmoid(ga_ref[...]) * y_mla).astype(o_ref.dtype)


def _merge(y_norm, attn, proj, p, tm):
    rows = y_norm.shape[0]
    gate = lambda k: pl.BlockSpec((tm, D_MODEL), lambda i: (i, COL_GATES // D_MODEL + k))
    return pl.pallas_call(
        _merge_kernel,
        grid=(rows // tm,),
        in_specs=[pl.BlockSpec((tm, SSM_INNER), lambda i: (i, 0)), pl.BlockSpec((tm, MLA_INNER), lambda i: (i, 0)),
                  gate(0), gate(1), gate(2),
                  _const_spec((SSM_INNER, D_MODEL)), _const_spec((MLA_INNER, D_MODEL))],
        out_specs=pl.BlockSpec((tm, D_MODEL), lambda i: (i, 0)),
        out_shape=jax.ShapeDtypeStruct((rows, D_MODEL), BF16),
        compiler_params=_cparams(("arbitrary",)),
        name="merge",
    )(y_norm, attn, proj, proj, proj, p["w_ssm_out"], p["w_mla_out"])


def _outproj_kernel(x_ref, mixed_ref, w_ref, nw_ref, o_ref):
    x_out = x_ref[...] + _dot(mixed_ref[...], w_ref[...])
    o_ref[...] = _rms(x_out, nw_ref[...])


def _outproj(x, mixed, p, tm):
    rows = x.shape[0]
    row = pl.BlockSpec((tm, D_MODEL), lambda i: (i, 0))
    return pl.pallas_call(
        _outproj_kernel,
        grid=(rows // tm,),
        in_specs=[row, row, _const_spec((D_MODEL, D_MODEL)), _const_spec((1, D_MODEL))],
        out_specs=row,
        out_shape=jax.ShapeDtypeStruct((rows, D_MODEL), F32),
        compiler_params=_cparams(("arbitrary",)),
        name="outproj",
    )(x, mixed, p["w_out"], p["final_norm_w"])


def _prepare(norm_in_w, w_in, conv_w, conv_b, dt_bias, a_log, d_skip, ssm_norm_w, w_ssm_out, q_norm_w, w_q_up,
             kv_norm_w, w_kv_up, w_mla_out, w_out, final_norm_w, chunk_lens):
    offs = np.cumsum((0, SSM_INNER, CONV_DIM, SSM_HEADS, Q_LORA, KV_LORA, QK_ROPE, MLA_INNER, D_MODEL, D_MODEL))
    w_in_bf = w_in.astype(BF16)
    col = lambda k: w_in_bf[:, int(offs[k]):int(offs[k + 1])]
    half = QK_ROPE // 2
    kr_w = col(5)
    zeros = lambda n: jnp.zeros((D_MODEL, n), BF16)
    w_tail = jnp.concatenate([
        col(6), col(7), col(8),
        col(3), col(4),
        col(2), zeros(SMALL_KR - SMALL_DT - SSM_HEADS),
        kr_w, zeros(SMALL_KR_SWAPPED - SMALL_KR - QK_ROPE),
        kr_w[:, half:], kr_w[:, :half], zeros(SMALL_COLS - SMALL_KR_SWAPPED - QK_ROPE),
        zeros(SEG_MLA - Q_LORA - KV_LORA - SMALL_COLS)], axis=1)

    wq = w_q_up.reshape(Q_LORA, MLA_HEADS, QK_NOPE + QK_ROPE)
    wq_rope = wq[..., QK_NOPE:]
    zq = jnp.zeros((Q_LORA, MLA_HEADS, 64), w_q_up.dtype)
    wq_a = jnp.concatenate([wq, zq], axis=-1).reshape(Q_LORA, MLA_HEADS * QK_PAD).astype(BF16)
    wq_b = jnp.concatenate([wq_rope[..., half:], wq_rope[..., :half], zq], axis=-1)
    wq_b = wq_b.reshape(Q_LORA, MLA_HEADS * LANES).astype(BF16)

    wkv = w_kv_up.reshape(KV_LORA, MLA_HEADS, QK_NOPE + V_DIM)
    w_uk = wkv[..., :QK_NOPE].reshape(KV_LORA, MLA_HEADS * QK_NOPE).astype(BF16)
    w_uv_h = wkv[..., QK_NOPE:]
    w_uv = w_uv_h.reshape(KV_LORA, MLA_INNER).astype(BF16)
    w_uv_t = w_uv.T

    head_of_lane = np.arange(SSM_INNER) // SSM_HEAD_DIM
    p = dict(
        norm_in_w=norm_in_w.reshape(1, D_MODEL), w_main=w_in_bf, w_tail=w_tail,
        conv_w=conv_w, conv_b=conv_b.reshape(1, CONV_DIM), dt_bias=dt_bias.reshape(1, SSM_HEADS),
        a_log=a_log.reshape(1, SSM_HEADS), d_skip_p=jnp.repeat(d_skip, SSM_HEAD_DIM).reshape(1, SSM_INNER),
        ssm_norm_w=ssm_norm_w.reshape(1, SSM_INNER), w_ssm_out=w_ssm_out.astype(BF16),
        q_norm_w=q_norm_w.reshape(1, Q_LORA), kv_norm_w=kv_norm_w.reshape(1, KV_LORA),
        wq_a=wq_a, wq_b=wq_b, w_uk=w_uk, w_uv=w_uv, w_uv_t=w_uv_t,
        w_mla_out=w_mla_out.astype(BF16), w_out=w_out.astype(BF16),
        final_norm_w=final_norm_w.reshape(1, D_MODEL),
        e_p=jnp.asarray(np.tile(head_of_lane[None, :] == np.arange(SSM_HEADS)[:, None], (3, 1)), BF16),
    )
    for q in chunk_lens:
        head_of_key_lane = np.arange(SSM_HEADS * q) // q
        p["e_q%d" % q] = jnp.asarray(np.tile(head_of_key_lane[None, :] == np.arange(SSM_HEADS)[:, None], (3, 1)),
                                     BF16)
    return p


def _rope_tables(pos):
    half = QK_ROPE // 2
    inv = 1.0 / (ROPE_THETA ** (jnp.arange(half, dtype=F32) * (2.0 / QK_ROPE)))
    ang = pos.astype(F32)[:, None] * inv[None, :]
    cos, sin = jnp.cos(ang), jnp.sin(ang)
    pad = jnp.zeros((pos.shape[0], LANES - QK_ROPE), F32)
    return jnp.concatenate([cos, cos, pad], axis=1), jnp.concatenate([-sin, sin, pad], axis=1)


class _Tiles(NamedTuple):
    inproj: int
    ssd_chunks: int
    prep: int
    attn_q: int
    merge: int
    outproj: int


def _tiles(seq, chunk):
    return _Tiles(inproj=min(1024, seq), ssd_chunks=4 if seq % (4 * chunk) == 0 else 1,
                  prep=min(512, seq // ATTN_SUB_TILES), attn_q=min(512, seq), merge=min(256, seq),
                  outproj=min(512, seq))


def kernel(x_prompt, x_sample, cache_kv_latent, cache_k_rope, state_ssm, state_conv, norm_in_w, w_in, conv_w,
           conv_b, dt_bias, a_log, d_skip, ssm_norm_w, w_ssm_out, q_norm_w, w_q_up, kv_norm_w, w_kv_up, w_mla_out,
           w_out, final_norm_w):
    depth = w_in.shape[0]
    assert depth == 1 and x_prompt.shape[0] == 1
    seq = x_prompt.shape[1]
    nb, t_new = x_sample.shape[0], x_sample.shape[1]
    past = cache_kv_latent.shape[2]
    q_prompt, q_sample = min(CHUNK, seq), min(CHUNK, t_new)
    p = _prepare(norm_in_w[0], w_in[0], conv_w[0], conv_b[0], dt_bias[0], a_log[0], d_skip[0], ssm_norm_w[0],
                 w_ssm_out[0], q_norm_w[0], w_q_up[0], kv_norm_w[0], w_kv_up[0], w_mla_out[0], w_out[0],
                 final_norm_w, sorted({q_prompt, q_sample}))

    t = _tiles(seq, q_prompt)
    xp = x_prompt[0]
    proj = _inproj(xp, p["norm_in_w"], p["w_main"], p["w_tail"], tm=t.inproj)
    y_norm, ssm_p, conv_p = _ssd(proj[None],
                                 jnp.zeros((1, CONV_WIDTH - 1, CONV_DIM), F32),
                                 jnp.zeros((1, SSM_INNER, SSM_STATE), F32), p, q_prompt, nsub=t.ssd_chunks)
    ct, sn = _rope_tables(jnp.arange(seq))
    ckv_p, kr_p, qt, kn, krb, vt = _mla_prompt(proj, ct, sn, p, tm=t.prep)
    attn = _attention_prompt(qt, kn, krb, vt, tq=t.attn_q)
    mixed = _merge(y_norm[0], attn, proj, p, tm=t.merge)
    y_prompt = _outproj(xp, mixed, p, tm=t.outproj)

    rows_s = nb * t_new
    xs = x_sample.reshape(rows_s, D_MODEL)
    proj_s = _inproj(xs, p["norm_in_w"], p["w_main"], p["w_tail"], tm=rows_s)
    y_norm_s, ssm_s, conv_s = _ssd(proj_s.reshape(nb, t_new, PROJ_COLS), state_conv[0],
                                   state_ssm[0].reshape(nb, SSM_INNER, SSM_STATE), p, q_sample, nsub=1)
    ct_s, sn_s = _rope_tables(past + jnp.arange(t_new))
    ckv_s, kr_s, ql, qr = _mla_sample(proj_s, jnp.tile(ct_s, (nb, 1)), jnp.tile(sn_s, (nb, 1)), p)
    ckv_s = ckv_s.reshape(nb, t_new, KV_LORA)
    kr_s = kr_s.reshape(nb, t_new, QK_ROPE)
    attn_s = _attention_cached(ql, qr, cache_kv_latent[0], cache_k_rope[0], ckv_s, kr_s, p["w_uv"])
    mixed_s = _merge(y_norm_s.reshape(rows_s, SSM_INNER), attn_s.reshape(rows_s, MLA_INNER), proj_s, p, tm=rows_s)
    y_sample = _outproj(xs, mixed_s, p, tm=rows_s).reshape(nb, t_new, D_MODEL)

    hshape = (SSM_HEADS, SSM_HEAD_DIM, SSM_STATE)
    return (y_prompt[None], y_sample,
            ckv_p[None, None], kr_p[None, None], ssm_p.reshape((1, 1) + hshape), conv_p[None],
            ckv_s[None], kr_s[None], ssm_s.reshape((1, nb) + hshape), conv_s[None])
```

```python
import functools
import math
from typing import NamedTuple

import jax
import jax.numpy as jnp
import numpy as np
from jax import lax
from jax.experimental import pallas as pl
from jax.experimental.pallas import tpu as pltpu

F32 = jnp.float32
BF16 = jnp.bfloat16

D_MODEL = 2048
CHUNK = 64
SSM_INNER = 4096
SSM_HEAD_DIM = 64
SSM_HEADS = 64
SSM_GROUPS = 8
SSM_STATE = 128
CONV_WIDTH = 4
CONV_DIM = SSM_INNER + 2 * SSM_GROUPS * SSM_STATE
GROUP_LANES = SSM_INNER // SSM_GROUPS
MLA_HEADS = 16
Q_LORA = 512
KV_LORA = 512
QK_NOPE = 128
QK_ROPE = 64
V_DIM = 128
MLA_INNER = MLA_HEADS * V_DIM
ROPE_THETA = 10000.0
EPS = 1e-6
QK_PAD = 256
SCALE = (QK_NOPE + QK_ROPE) ** -0.5
LOG2E = math.log2(math.e)
V_ROWS = V_DIM + 16
LANES = 128
V7X_VMEM_BYTES = 64 * 1024 * 1024
VMEM_LIMIT = V7X_VMEM_BYTES * 7 // 8
ATTN_SUB_TILES = 2

IN_TN = 1024
SEG_MLA = 2048
SMALL_COLS = 512
SMALL_DT, SMALL_KR, SMALL_KR_SWAPPED = 0, 128, 256
COL_XBC = 0
COL_GATES = COL_XBC + CONV_DIM
COL_Z = COL_GATES + MLA_INNER + 2 * D_MODEL
COL_MLA = COL_Z + SSM_INNER
PROJ_COLS = COL_MLA + SEG_MLA
COL_SMALL = COL_MLA + Q_LORA + KV_LORA


def _cparams(sem, vmem=VMEM_LIMIT):
    return pltpu.CompilerParams(dimension_semantics=sem, vmem_limit_bytes=vmem)


def _const_spec(shape):
    nd = len(shape)
    return pl.BlockSpec(shape, lambda *_: (0,) * nd, pipeline_mode=pl.Buffered(1))


def _sigmoid(x):
    return 0.5 + 0.5 * jnp.tanh(0.5 * x)


def _silu(x):
    h = 0.5 * x
    return h + h * jnp.tanh(h)


def _rms(x, w):
    return x * lax.rsqrt(jnp.mean(x * x, axis=-1, keepdims=True) + EPS) * w


def _dot(a, b):
    return jnp.dot(a, b, preferred_element_type=F32)


def _dot_nt(a, b):
    return lax.dot_general(a, b, (((1,), (1,)), ((), ())), preferred_element_type=F32)


def _dot_tn(a, b):
    return lax.dot_general(a, b, (((0,), (0,)), ((), ())), preferred_element_type=F32)


def _split3(v):
    hi = v.astype(BF16)
    r1 = v - hi.astype(F32)
    mid = r1.astype(BF16)
    lo = (r1 - mid.astype(F32)).astype(BF16)
    return hi, mid, lo


def _inproj_kernel(x_ref, nw_ref, wm_ref, wt_ref, o_ref, h_ref, *, from_main):
    j = pl.program_id(1)

    @pl.when(j == 0)
    def _():
        h_ref[...] = _rms(x_ref[...], nw_ref[...]).astype(BF16)

    is_main = functools.reduce(jnp.logical_or, [(j >= lo) & (j < hi) for lo, hi in from_main])

    @pl.when(is_main)
    def _():
        o_ref[...] = _dot(h_ref[...], wm_ref[...])

    @pl.when(jnp.logical_not(is_main))
    def _():
        o_ref[...] = _dot(h_ref[...], wt_ref[...])


def _inproj(x, norm_w, w_main, w_tail, tm):
    rows = x.shape[0]
    t = lambda cols: cols // IN_TN
    n_xbc, n_g, n_z = t(CONV_DIM), t(MLA_INNER + 2 * D_MODEL), t(SSM_INNER)
    j_g, j_z, j_mla = t(COL_GATES), t(COL_Z), t(COL_MLA)
    src_xbc, src_z = t(SSM_INNER), 0

    def main_map(i, j):
        return 0, jnp.where(j < j_z, src_xbc + jnp.minimum(j, n_xbc - 1), src_z + jnp.minimum(j - j_z, n_z - 1))

    def tail_map(i, j):
        return 0, jnp.where(j < j_mla, jnp.clip(j - j_g, 0, n_g - 1), n_g + j - j_mla)

    return pl.pallas_call(
        functools.partial(_inproj_kernel, from_main=((0, n_xbc), (j_z, j_z + n_z))),
        grid=(rows // tm, t(PROJ_COLS)),
        in_specs=[
            pl.BlockSpec((tm, D_MODEL), lambda i, j: (i, 0)),
            pl.BlockSpec((1, D_MODEL), lambda i, j: (0, 0)),
            pl.BlockSpec((D_MODEL, IN_TN), main_map),
            pl.BlockSpec((D_MODEL, IN_TN), tail_map),
        ],
        out_specs=pl.BlockSpec((tm, IN_TN), lambda i, j: (i, j)),
        out_shape=jax.ShapeDtypeStruct((rows, PROJ_COLS), F32),
        scratch_shapes=[pltpu.VMEM((tm, D_MODEL), BF16)],
        compiler_params=_cparams(("parallel", "arbitrary")),
        name="inproj",
    )(x, norm_w, w_main, w_tail)


CONV_PAD = 8
CONV_SLAB = 256


def _ssd_kernel(xbc_ref, z_ref, sm_ref, cprev_ref, s0_ref, cw_ref, cb_ref, dtb_ref, alog_ref, dsk_ref,
                nw_ref, ep_ref, eq_ref, y_ref, sout_ref, cout_ref, st_ref, xpad_ref, xc_ref, exp_p_ref,
                exp_q_ref, yd_ref, *, q, nsub):
    c = pl.program_id(1)
    nc = pl.num_programs(1)
    rows = nsub * q
    hp = LANES // q
    tiles_per_group = (SSM_HEADS // SSM_GROUPS) // hp
    tile_w = hp * SSM_HEAD_DIM
    lo = CONV_PAD - (CONV_WIDTH - 1)

    @pl.when(c == 0)
    def _():
        st_ref[...] = s0_ref[0].T
        xpad_ref[0:lo, :] = jnp.zeros((lo, CONV_DIM), F32)
        xpad_ref[lo:CONV_PAD, :] = cprev_ref[0]

    assert CONV_WIDTH == 4
    xpad_ref[CONV_PAD:CONV_PAD + rows, :] = xbc_ref[0]
    for s in range(0, CONV_DIM, CONV_SLAB):
        xw = xpad_ref[:, s:s + CONV_SLAB]
        x1 = pltpu.roll(xw, 1, 0)
        w = [cw_ref[k:k + 1, s:s + CONV_SLAB] for k in range(CONV_WIDTH)]
        u = (xw * w[3] + x1 * w[2]) + pltpu.roll(xw * w[1] + x1 * w[0], 2, 0)
        xc_ref[:, s:s + CONV_SLAB] = _silu(u[CONV_PAD:, :] + cb_ref[:, s:s + CONV_SLAB])
    tail_rows = xpad_ref[CONV_PAD + rows - (CONV_WIDTH - 1):CONV_PAD + rows, :]
    xpad_ref[lo:CONV_PAD, :] = tail_rows

    ri = lax.broadcasted_iota(jnp.int32, (q, q), 0)
    ci = lax.broadcasted_iota(jnp.int32, (q, q), 1)
    tril = jnp.where(ci <= ri, 1.0, 0.0).astype(BF16)
    r128 = lax.broadcasted_iota(jnp.int32, (q, LANES), 0)
    c128 = lax.broadcasted_iota(jnp.int32, (q, LANES), 1)
    key128 = c128 & (q - 1)
    causal = key128 <= r128
    diag = key128 == r128
    br = lax.broadcasted_iota(jnp.int32, (LANES, tile_w), 0)
    bc = lax.broadcasted_iota(jnp.int32, (LANES, tile_w), 1)
    blockdiag = (br // q) == (bc // SSM_HEAD_DIM)
    neg_a = -jnp.exp(alog_ref[...])

    for u in range(nsub):
        r0 = u * q
        xdt = sm_ref[0, r0:r0 + q, SMALL_DT:SMALL_DT + SSM_HEADS] + dtb_ref[...]
        dt = jnp.maximum(xdt, 0.0) + jnp.log1p(jnp.exp(-jnp.abs(xdt)))
        h3 = _split3(dt * neg_a)
        acs = _dot(tril, h3[0]) + _dot(tril, h3[1]) + _dot(tril, h3[2])

        def expand(v, e_ref):
            pieces = jnp.concatenate([piece.astype(F32) for piece in _split3(v)], axis=1).astype(BF16)
            return _dot(pieces, e_ref[...])

        exp_q_ref[u] = expand(jnp.concatenate([acs, dt], axis=0), eq_ref)
        exp_p_ref[u] = expand(jnp.concatenate([jnp.exp(acs), jnp.exp(acs[q - 1:q, :] - acs) * dt], axis=0), ep_ref)

        for g in range(SSM_GROUPS):
            gs = g * GROUP_LANES
            bs = SSM_INNER + g * SSM_STATE
            cs = SSM_INNER + (SSM_GROUPS + g) * SSM_STATE
            b_bf = xc_ref[r0:r0 + q, bs:bs + SSM_STATE].astype(BF16)
            c_bf = xc_ref[r0:r0 + q, cs:cs + SSM_STATE].astype(BF16)
            cb_t = _dot_nt(c_bf, jnp.concatenate([b_bf] * hp, axis=0))

            for tt in range(tiles_per_group):
                t = g * tiles_per_group + tt
                a_t = exp_q_ref[u, 0:q, t * LANES:(t + 1) * LANES]
                d_t = exp_q_ref[u, q:2 * q, t * LANES:(t + 1) * LANES]
                a_key = jnp.sum(jnp.where(diag, a_t, 0.0), axis=0, keepdims=True)
                d_key = jnp.sum(jnp.where(diag, d_t, 0.0), axis=0, keepdims=True)
                w_t = jnp.exp(jnp.where(causal, a_t - a_key, -jnp.inf)) * d_key * cb_t
                x_t = xc_ref[r0:r0 + q, t * tile_w:(t + 1) * tile_w]
                rhs = jnp.where(blockdiag, jnp.concatenate([x_t] * hp, axis=0), 0.0).astype(BF16)
                yd_ref[r0:r0 + q, t * tile_w:(t + 1) * tile_w] = _dot(w_t.astype(BF16), rhs)

            decay = exp_p_ref[u, 0:q, gs:gs + GROUP_LANES]
            tail = exp_p_ref[u, q:2 * q, gs:gs + GROUP_LANES]
            xs_g = xc_ref[r0:r0 + q, gs:gs + GROUP_LANES]
            st_g = st_ref[:, gs:gs + GROUP_LANES]
            y_off = _dot(c_bf, st_g.astype(BF16)) * decay
            xt = (xs_g * tail).astype(BF16)
            st_ref[:, gs:gs + GROUP_LANES] = st_g * decay[q - 1:q, :] + _dot_tn(b_bf, xt)

            y = yd_ref[r0:r0 + q, gs:gs + GROUP_LANES] + y_off + dsk_ref[:, gs:gs + GROUP_LANES] * xs_g
            y = y * _silu(z_ref[0, r0:r0 + q, gs:gs + GROUP_LANES])
            y = y * lax.rsqrt(jnp.mean(y * y, axis=-1, keepdims=True) + EPS)
            y_ref[0, r0:r0 + q, gs:gs + GROUP_LANES] = (y * nw_ref[:, gs:gs + GROUP_LANES]).astype(y_ref.dtype)

    @pl.when(c == nc - 1)
    def _():
        sout_ref[0] = st_ref[...].T
        cout_ref[0] = tail_rows


def _ssd(proj, conv_prev, s0, p, q, nsub):
    nb, lb = proj.shape[0], proj.shape[1]
    rows = q * nsub
    nc = lb // rows
    kq = SSM_HEADS * q
    in_specs = [
        pl.BlockSpec((1, rows, CONV_DIM), lambda b, c: (b, c, COL_XBC // CONV_DIM)),
        pl.BlockSpec((1, rows, SSM_INNER), lambda b, c: (b, c, COL_Z // SSM_INNER)),
        pl.BlockSpec((1, rows, SMALL_COLS), lambda b, c: (b, c, COL_SMALL // SMALL_COLS)),
        pl.BlockSpec((1, CONV_WIDTH - 1, CONV_DIM), lambda b, c: (b, 0, 0)),
        pl.BlockSpec((1, SSM_INNER, SSM_STATE), lambda b, c: (b, 0, 0)),
        _const_spec((CONV_WIDTH, CONV_DIM)),
        _const_spec((1, CONV_DIM)),
        _const_spec((1, SSM_HEADS)),
        _const_spec((1, SSM_HEADS)),
        _const_spec((1, SSM_INNER)),
        _const_spec((1, SSM_INNER)),
        _const_spec((3 * SSM_HEADS, SSM_INNER)),
        _const_spec((3 * SSM_HEADS, kq)),
    ]
    out_specs = [
        pl.BlockSpec((1, rows, SSM_INNER), lambda b, c: (b, c, 0)),
        pl.BlockSpec((1, SSM_INNER, SSM_STATE), lambda b, c: (b, 0, 0)),
        pl.BlockSpec((1, CONV_WIDTH - 1, CONV_DIM), lambda b, c: (b, 0, 0)),
    ]
    out_shape = [
        jax.ShapeDtypeStruct((nb, lb, SSM_INNER), BF16),
        jax.ShapeDtypeStruct((nb, SSM_INNER, SSM_STATE), F32),
        jax.ShapeDtypeStruct((nb, CONV_WIDTH - 1, CONV_DIM), F32),
    ]
    scratch = [
        pltpu.VMEM((SSM_STATE, SSM_INNER), F32),
        pltpu.VMEM((CONV_PAD + rows, CONV_DIM), F32),
        pltpu.VMEM((rows, CONV_DIM), F32),
        pltpu.VMEM((nsub, 2 * q, SSM_INNER), F32),
        pltpu.VMEM((nsub, 2 * q, kq), F32),
        pltpu.VMEM((rows, SSM_INNER), F32),
    ]
    return pl.pallas_call(
        functools.partial(_ssd_kernel, q=q, nsub=nsub),
        grid=(nb, nc),
        in_specs=in_specs,
        out_specs=out_specs,
        out_shape=out_shape,
        scratch_shapes=scratch,
        compiler_params=_cparams(("parallel", "arbitrary")),
        name="ssd",
    )(proj, proj, proj, conv_prev, s0, p["conv_w"], p["conv_b"], p["dt_bias"], p["a_log"], p["d_skip_p"],
      p["ssm_norm_w"], p["e_p"], p["e_q%d" % q])


def _mla_latents(mla_ref, ct_ref, sn_ref, qnw_ref, kvnw_ref, ckv_ref, kr_ref):
    m = mla_ref[...]
    cqn = _rms(m[:, 0:Q_LORA], qnw_ref[...])
    ckv = _rms(m[:, Q_LORA:Q_LORA + KV_LORA], kvnw_ref[...])
    ckv_ref[...] = ckv
    small = m[:, Q_LORA + KV_LORA:]
    ct, sn = ct_ref[...], sn_ref[...]
    kr128 = (small[:, SMALL_KR:SMALL_KR + LANES] * ct
             + small[:, SMALL_KR_SWAPPED:SMALL_KR_SWAPPED + LANES] * sn)
    kr_ref[...] = kr128[:, 0:QK_ROPE]
    return cqn, ckv, kr128, ct, sn


def _mla_prompt_kernel(mla_ref, ct_ref, sn_ref, qnw_ref, kvnw_ref, wqat_ref, wqbt_ref, wk_ref,
                       wvt_ref, ckv_ref, kr_ref, qt_ref, kn_ref, krb_ref, vt_ref):
    cqn, ckv, kr128, ct, sn = _mla_latents(mla_ref, ct_ref, sn_ref, qnw_ref, kvnw_ref, ckv_ref, kr_ref)
    cqn_t = cqn.T.astype(BF16)
    ckv_t = ckv.T.astype(BF16)
    ckv_bf = ckv.astype(BF16)
    krb_ref[...] = kr128.astype(BF16)
    ctt, snt = ct.T, sn.T
    tm = ckv_bf.shape[0]
    ones_row = jnp.where(lax.broadcasted_iota(jnp.int32, (V_ROWS - V_DIM, tm), 0) == 0, 1.0, 0.0).astype(BF16)
    qs = SCALE * LOG2E
    for h in range(MLA_HEADS):
        qa = _dot(wqat_ref[h * QK_PAD:(h + 1) * QK_PAD, :], cqn_t)
        qb = _dot(wqbt_ref[h * LANES:(h + 1) * LANES, :], cqn_t)
        qt_ref[h, 0:QK_NOPE, :] = (qa[0:QK_NOPE] * qs).astype(BF16)
        qt_ref[h, QK_NOPE:QK_PAD, :] = ((qa[QK_NOPE:QK_PAD] * ctt + qb * snt) * qs).astype(BF16)
        vt_ref[h, 0, 0:V_DIM, :] = _dot(wvt_ref[h * V_DIM:(h + 1) * V_DIM, :], ckv_t).astype(BF16)
        vt_ref[h, 0, V_DIM:V_ROWS, :] = ones_row
    for g in range(MLA_HEADS // 2):
        kn2 = _dot(ckv_bf, wk_ref[:, 2 * g * QK_NOPE:(2 * g + 2) * QK_NOPE]).astype(BF16)
        kn_ref[2 * g] = kn2[:, 0:QK_NOPE]
        kn_ref[2 * g + 1] = kn2[:, QK_NOPE:2 * QK_NOPE]


def _mla_prompt(proj, ct, sn, p, tm):
    rows = proj.shape[0]
    row = lambda w: pl.BlockSpec((tm, w), lambda i: (i, 0))
    mla = pl.BlockSpec((tm, SEG_MLA), lambda i: (i, COL_MLA // SEG_MLA))
    return pl.pallas_call(
        _mla_prompt_kernel,
        grid=(rows // tm,),
        in_specs=[mla, row(LANES), row(LANES), _const_spec((1, Q_LORA)),
                  _const_spec((1, KV_LORA)),
                  _const_spec((MLA_HEADS * QK_PAD, Q_LORA)), _const_spec((MLA_HEADS * LANES, Q_LORA)),
                  _const_spec((KV_LORA, MLA_HEADS * QK_NOPE)), _const_spec((MLA_INNER, KV_LORA))],
        out_specs=[row(KV_LORA), row(QK_ROPE),
                   pl.BlockSpec((MLA_HEADS, QK_PAD, tm), lambda i: (0, 0, i)),
                   pl.BlockSpec((MLA_HEADS, tm, QK_NOPE), lambda i: (0, i, 0)), row(LANES),
                   pl.BlockSpec((MLA_HEADS, 1, V_ROWS, tm), lambda i: (0, i, 0, 0))],
        out_shape=[jax.ShapeDtypeStruct((rows, KV_LORA), F32), jax.ShapeDtypeStruct((rows, QK_ROPE), F32),
                   jax.ShapeDtypeStruct((MLA_HEADS, QK_PAD, rows), BF16),
                   jax.ShapeDtypeStruct((MLA_HEADS, rows, QK_NOPE), BF16),
                   jax.ShapeDtypeStruct((rows, LANES), BF16),
                   jax.ShapeDtypeStruct((MLA_HEADS, rows // tm, V_ROWS, tm), BF16)],
        compiler_params=_cparams(("parallel",)),
        name="mla_prep_prompt",
    )(proj, ct, sn, p["q_norm_w"], p["kv_norm_w"], p["wq_a"].T, p["wq_b"].T, p["w_uk"], p["w_uv_t"])


def _mla_sample_kernel(mla_ref, ct_ref, sn_ref, qnw_ref, kvnw_ref, wqa_ref, wqb_ref, wk_ref,
                       ckv_ref, kr_ref, ql_ref, qr_ref):
    cqn, _, _, ct, sn = _mla_latents(mla_ref, ct_ref, sn_ref, qnw_ref, kvnw_ref, ckv_ref, kr_ref)
    cqn = cqn.astype(BF16)
    qa = _dot(cqn, wqa_ref[...])
    qb = _dot(cqn, wqb_ref[...])
    for h in range(MLA_HEADS):
        o = h * QK_PAD
        q_rope = qa[:, o + QK_NOPE:o + QK_PAD] * ct + qb[:, h * LANES:(h + 1) * LANES] * sn
        qr_ref[h] = (q_rope * SCALE).astype(BF16)
        q_nope = qa[:, o:o + QK_NOPE].astype(BF16)
        q_lat = _dot_nt(q_nope, wk_ref[:, h * QK_NOPE:(h + 1) * QK_NOPE])
        ql_ref[h] = (q_lat * SCALE).astype(BF16)


def _mla_sample(proj, ct, sn, p):
    rows = proj.shape[0]
    full = lambda w: pl.BlockSpec((rows, w), lambda i: (0, 0))
    mla = pl.BlockSpec((rows, SEG_MLA), lambda i: (0, COL_MLA // SEG_MLA))
    return pl.pallas_call(
        _mla_sample_kernel,
        grid=(1,),
        in_specs=[mla, full(LANES), full(LANES), _const_spec((1, Q_LORA)), _const_spec((1, KV_LORA)),
                  _const_spec((Q_LORA, MLA_HEADS * QK_PAD)), _const_spec((Q_LORA, MLA_HEADS * LANES)),
                  _const_spec((KV_LORA, MLA_HEADS * QK_NOPE))],
        out_specs=[full(KV_LORA), full(QK_ROPE),
                   pl.BlockSpec((MLA_HEADS, rows, KV_LORA), lambda i: (0, 0, 0)),
                   pl.BlockSpec((MLA_HEADS, rows, LANES), lambda i: (0, 0, 0))],
        out_shape=[jax.ShapeDtypeStruct((rows, KV_LORA), F32), jax.ShapeDtypeStruct((rows, QK_ROPE), F32),
                   jax.ShapeDtypeStruct((MLA_HEADS, rows, KV_LORA), BF16),
                   jax.ShapeDtypeStruct((MLA_HEADS, rows, LANES), BF16)],
        compiler_params=_cparams(("parallel",)),
        name="mla_prep_sample",
    )(proj, ct, sn, p["q_norm_w"], p["kv_norm_w"], p["wq_a"], p["wq_b"], p["w_uk"])


def _attn_kernel(qi_ref, kj_ref, qt_ref, kn_ref, kr_ref, vt_ref, o_ref, m_ref, acc_ref, s0_ref, s1_ref, p0_ref,
                 p1_ref, a0_ref, a1_ref, *, tq, tks):
    s_refs, p_refs, a_refs = (s0_ref, s1_ref), (p0_ref, p1_ref), (a0_ref, a1_ref)
    n_sub = ATTN_SUB_TILES
    t = pl.program_id(0)
    i = qi_ref[t]
    j = kj_ref[t]
    tk = n_sub * tks
    last_j = ((i + 1) * tq - 1) // tk

    @pl.when(j == 0)
    def _():
        m_ref[...] = jnp.full(m_ref.shape, -jnp.inf, F32)
        acc_ref[...] = jnp.zeros(acc_ref.shape, F32)

    def step(masked, sub_tiles):
        if masked:
            q_chunk = (i * tq + lax.broadcasted_iota(jnp.int32, (tks, tq), 1)) // CHUNK
            k_chunk = (j * tk + lax.broadcasted_iota(jnp.int32, (tks, tq), 0)) // CHUNK
            visible = {c: k_chunk + (c * tks) // CHUNK <= q_chunk for c in sub_tiles}

        def scores(h, c, b):
            k_h = jnp.concatenate([kn_ref[h, c * tks:(c + 1) * tks, :], kr_ref[c * tks:(c + 1) * tks, :]], axis=1)
            s_refs[b][...] = _dot(k_h, qt_ref[h])

        def softmax(h, c, b):
            s = s_refs[b][...]
            if masked:
                s = jnp.where(visible[c], s, -jnp.inf)
            m_prev = m_ref[h]
            m_new = jnp.maximum(m_prev, jnp.max(s, axis=0, keepdims=True))
            m_ref[h] = m_new
            p_refs[b][...] = jnp.exp2(s - m_new).astype(BF16)
            a_refs[b][...] = jnp.exp2(m_prev - m_new)

        def values(h, c, b):
            acc_ref[h] = acc_ref[h] * a_refs[b][...] + _dot(vt_ref[h, c], p_refs[b][...])

        items = [(h, c) for h in range(MLA_HEADS) for c in sub_tiles]
        stages = (scores, softmax, values)
        for n in range(len(items) + len(stages) - 1):
            for lag, stage in enumerate(stages):
                if 0 <= n - lag < len(items):
                    stage(*items[n - lag], (n - lag) % 2)

    skip_upper = (j * tk + tks) >= (i + 1) * tq

    @pl.when(j < last_j)
    def _():
        step(False, tuple(range(n_sub)))

    @pl.when((j == last_j) & jnp.logical_not(skip_upper))
    def _():
        step(True, tuple(range(n_sub)))

    @pl.when((j == last_j) & skip_upper)
    def _():
        step(True, (0,))

    @pl.when(j == last_j)
    def _():
        for h in range(MLA_HEADS):
            den = acc_ref[h, V_DIM:V_DIM + 1, :]
            o_ref[:, h * V_DIM:(h + 1) * V_DIM] = (acc_ref[h, 0:V_DIM, :] * (1.0 / den)).T


def _attention_prompt(qt, kn, kr, vt, tq):
    rows = kn.shape[1]
    n_sub, tks = ATTN_SUB_TILES, vt.shape[3]
    tk = tks * n_sub
    qi, kj = [], []
    for i in range(rows // tq):
        for j in range(((i + 1) * tq - 1) // tk + 1):
            qi.append(i)
            kj.append(j)
    qi = jnp.asarray(np.array(qi, np.int32))
    kj = jnp.asarray(np.array(kj, np.int32))
    grid_spec = pltpu.PrefetchScalarGridSpec(
        num_scalar_prefetch=2,
        grid=(int(qi.shape[0]),),
        in_specs=[
            pl.BlockSpec((MLA_HEADS, QK_PAD, tq), lambda t, qi, kj: (0, 0, qi[t])),
            pl.BlockSpec((MLA_HEADS, tk, QK_NOPE), lambda t, qi, kj: (0, kj[t], 0)),
            pl.BlockSpec((tk, LANES), lambda t, qi, kj: (kj[t], 0)),
            pl.BlockSpec((MLA_HEADS, n_sub, V_ROWS, tks), lambda t, qi, kj: (0, kj[t], 0, 0)),
        ],
        out_specs=pl.BlockSpec((tq, MLA_INNER), lambda t, qi, kj: (qi[t], 0)),
        scratch_shapes=[
            pltpu.VMEM((MLA_HEADS, 1, tq), F32),
            pltpu.VMEM((MLA_HEADS, V_ROWS, tq), F32),
            pltpu.VMEM((tks, tq), F32), pltpu.VMEM((tks, tq), F32),
            pltpu.VMEM((tks, tq), BF16), pltpu.VMEM((tks, tq), BF16),
            pltpu.VMEM((1, tq), F32), pltpu.VMEM((1, tq), F32),
        ],
    )
    return pl.pallas_call(
        functools.partial(_attn_kernel, tq=tq, tks=tks),
        grid_spec=grid_spec,
        out_shape=jax.ShapeDtypeStruct((rows, MLA_INNER), F32),
        compiler_params=_cparams(("arbitrary",)),
        name="attn_prompt",
    )(qi, kj, qt, kn, kr, vt)


def _attn_cached_kernel(ql_ref, qr_ref, ckv_c_ref, kr_c_ref, ckv_n_ref, kr_n_ref, wv_ref, o_ref, *, t_new):
    rows = MLA_HEADS * t_new
    ql = ql_ref[...].reshape(rows, KV_LORA)
    qr = qr_ref[...].reshape(rows, LANES)[:, 0:QK_ROPE]
    kc = ckv_c_ref[0].astype(BF16)
    kn = ckv_n_ref[0].astype(BF16)
    s_c = _dot_nt(ql, kc) + _dot_nt(qr, kr_c_ref[0].astype(BF16))
    s_n = _dot_nt(ql, kn) + _dot_nt(qr, kr_n_ref[0].astype(BF16))
    m = jnp.maximum(jnp.max(s_c, axis=1, keepdims=True), jnp.max(s_n, axis=1, keepdims=True))
    p_c = jnp.exp(s_c - m)
    p_n = jnp.exp(s_n - m)
    den = jnp.sum(p_c, axis=1, keepdims=True) + jnp.sum(p_n, axis=1, keepdims=True)
    o_lat = (_dot(p_c.astype(BF16), kc) + _dot(p_n.astype(BF16), kn)) / den
    o_bf = o_lat.astype(BF16)
    for h in range(MLA_HEADS):
        o_ref[0, :, h * V_DIM:(h + 1) * V_DIM] = _dot(o_bf[h * t_new:(h + 1) * t_new, :],
                                                        wv_ref[:, h * V_DIM:(h + 1) * V_DIM])


def _attention_cached(ql, qr, cache_kv, cache_kr, ckv_new, kr_new, w_uv):
    nb, past = cache_kv.shape[0], cache_kv.shape[1]
    t_new = ckv_new.shape[1]
    return pl.pallas_call(
        functools.partial(_attn_cached_kernel, t_new=t_new),
        grid=(nb,),
        in_specs=[
            pl.BlockSpec((MLA_HEADS, t_new, KV_LORA), lambda b: (0, b, 0)),
            pl.BlockSpec((MLA_HEADS, t_new, LANES), lambda b: (0, b, 0)),
            pl.BlockSpec((1, past, KV_LORA), lambda b: (b, 0, 0)),
            pl.BlockSpec((1, past, QK_ROPE), lambda b: (b, 0, 0)),
            pl.BlockSpec((1, t_new, KV_LORA), lambda b: (b, 0, 0)),
            pl.BlockSpec((1, t_new, QK_ROPE), lambda b: (b, 0, 0)),
            _const_spec((KV_LORA, MLA_INNER)),
        ],
        out_specs=pl.BlockSpec((1, t_new, MLA_INNER), lambda b: (b, 0, 0)),
        out_shape=jax.ShapeDtypeStruct((nb, t_new, MLA_INNER), F32),
        compiler_params=_cparams(("parallel",)),
        name="attn_cached",
    )(ql, qr, cache_kv, cache_kr, ckv_new, kr_new, w_uv)


def _merge_kernel(y_ref, attn_ref, gm_ref, gs_ref, ga_ref, wssm_ref, wmla_ref, o_ref):
    y_ssm = _dot(y_ref[...], wssm_ref[...])
    y_mla = _dot((attn_ref[...] * _silu(gm_ref[...])).astype(BF16), wmla_ref[...])
    o_ref[...] = (_sigmoid(gs_ref[...]) * y_ssm + _sigmoid(ga_ref[...]) * y_mla).astype(o_ref.dtype)


def _merge(y_norm, attn, proj, p, tm):
    rows = y_norm.shape[0]
    gate = lambda k: pl.BlockSpec((tm, D_MODEL), lambda i: (i, COL_GATES // D_MODEL + k))
    return pl.pallas_call(
        _merge_kernel,
        grid=(rows // tm,),
        in_specs=[pl.BlockSpec((tm, SSM_INNER), lambda i: (i, 0)), pl.BlockSpec((tm, MLA_INNER), lambda i: (i, 0)),
                  gate(0), gate(1), gate(2),
                  _const_spec((SSM_INNER, D_MODEL)), _const_spec((MLA_INNER, D_MODEL))],
        out_specs=pl.BlockSpec((tm, D_MODEL), lambda i: (i, 0)),
        out_shape=jax.ShapeDtypeStruct((rows, D_MODEL), BF16),
        compiler_params=_cparams(("parallel",)),
        name="merge",
    )(y_norm, attn, proj, proj, proj, p["w_ssm_out"], p["w_mla_out"])


def _outproj_kernel(x_ref, mixed_ref, w_ref, nw_ref, o_ref):
    x_out = x_ref[...] + _dot(mixed_ref[...], w_ref[...])
    o_ref[...] = _rms(x_out, nw_ref[...])


def _outproj(x, mixed, p, tm):
    rows = x.shape[0]
    row = pl.BlockSpec((tm, D_MODEL), lambda i: (i, 0))
    return pl.pallas_call(
        _outproj_kernel,
        grid=(rows // tm,),
        in_specs=[row, row, _const_spec((D_MODEL, D_MODEL)), _const_spec((1, D_MODEL))],
        out_specs=row,
        out_shape=jax.ShapeDtypeStruct((rows, D_MODEL), F32),
        compiler_params=_cparams(("parallel",)),
        name="outproj",
    )(x, mixed, p["w_out"], p["final_norm_w"])


def _prepare(norm_in_w, w_in, conv_w, conv_b, dt_bias, a_log, d_skip, ssm_norm_w, w_ssm_out, q_norm_w, w_q_up,
             kv_norm_w, w_kv_up, w_mla_out, w_out, final_norm_w, chunk_lens):
    offs = np.cumsum((0, SSM_INNER, CONV_DIM, SSM_HEADS, Q_LORA, KV_LORA, QK_ROPE, MLA_INNER, D_MODEL, D_MODEL))
    w_in_bf = w_in.astype(BF16)
    col = lambda k: w_in_bf[:, int(offs[k]):int(offs[k + 1])]
    half = QK_ROPE // 2
    kr_w = col(5)
    zeros = lambda n: jnp.zeros((D_MODEL, n), BF16)
    w_tail = jnp.concatenate([
        col(6), col(7), col(8),
        col(3), col(4),
        col(2), zeros(SMALL_KR - SMALL_DT - SSM_HEADS),
        kr_w, zeros(SMALL_KR_SWAPPED - SMALL_KR - QK_ROPE),
        kr_w[:, half:], kr_w[:, :half], zeros(SMALL_COLS - SMALL_KR_SWAPPED - QK_ROPE),
        zeros(SEG_MLA - Q_LORA - KV_LORA - SMALL_COLS)], axis=1)

    wq = w_q_up.reshape(Q_LORA, MLA_HEADS, QK_NOPE + QK_ROPE)
    wq_rope = wq[..., QK_NOPE:]
    zq = jnp.zeros((Q_LORA, MLA_HEADS, 64), w_q_up.dtype)
    wq_a = jnp.concatenate([wq, zq], axis=-1).reshape(Q_LORA, MLA_HEADS * QK_PAD).astype(BF16)
    wq_b = jnp.concatenate([wq_rope[..., half:], wq_rope[..., :half], zq], axis=-1)
    wq_b = wq_b.reshape(Q_LORA, MLA_HEADS * LANES).astype(BF16)

    wkv = w_kv_up.reshape(KV_LORA, MLA_HEADS, QK_NOPE + V_DIM)
    w_uk = wkv[..., :QK_NOPE].reshape(KV_LORA, MLA_HEADS * QK_NOPE).astype(BF16)
    w_uv_h = wkv[..., QK_NOPE:]
    w_uv = w_uv_h.reshape(KV_LORA, MLA_INNER).astype(BF16)
    w_uv_t = w_uv.T

    head_of_lane = np.arange(SSM_INNER) // SSM_HEAD_DIM
    p = dict(
        norm_in_w=norm_in_w.reshape(1, D_MODEL), w_main=w_in_bf, w_tail=w_tail,
        conv_w=conv_w, conv_b=conv_b.reshape(1, CONV_DIM), dt_bias=dt_bias.reshape(1, SSM_HEADS),
        a_log=a_log.reshape(1, SSM_HEADS), d_skip_p=jnp.repeat(d_skip, SSM_HEAD_DIM).reshape(1, SSM_INNER),
        ssm_norm_w=ssm_norm_w.reshape(1, SSM_INNER), w_ssm_out=w_ssm_out.astype(BF16),
        q_norm_w=q_norm_w.reshape(1, Q_LORA), kv_norm_w=kv_norm_w.reshape(1, KV_LORA),
        wq_a=wq_a, wq_b=wq_b, w_uk=w_uk, w_uv=w_uv, w_uv_t=w_uv_t,
        w_mla_out=w_mla_out.astype(BF16), w_out=w_out.astype(BF16),
        final_norm_w=final_norm_w.reshape(1, D_MODEL),
        e_p=jnp.asarray(np.tile(head_of_lane[None, :] == np.arange(SSM_HEADS)[:, None], (3, 1)), BF16),
    )
    for q in chunk_lens:
        head_of_key_lane = np.arange(SSM_HEADS * q) // q
        p["e_q%d" % q] = jnp.asarray(np.tile(head_of_key_lane[None, :] == np.arange(SSM_HEADS)[:, None], (3, 1)),
                                     BF16)
    return p


def _rope_tables(pos):
    half = QK_ROPE // 2
    inv = 1.0 / (ROPE_THETA ** (jnp.arange(half, dtype=F32) * (2.0 / QK_ROPE)))
    ang = pos.astype(F32)[:, None] * inv[None, :]
    cos, sin = jnp.cos(ang), jnp.sin(ang)
    pad = jnp.zeros((pos.shape[0], LANES - QK_ROPE), F32)
    return jnp.concatenate([cos, cos, pad], axis=1), jnp.concatenate([-sin, sin, pad], axis=1)


class _Tiles(NamedTuple):
    inproj: int
    ssd_chunks: int
    prep: int
    attn_q: int
    merge: int
    outproj: int


def _tiles(seq, chunk):
    return _Tiles(inproj=min(1024, seq), ssd_chunks=4 if seq % (4 * chunk) == 0 else 1,
                  prep=min(512, seq // ATTN_SUB_TILES), attn_q=min(512, seq), merge=min(256, seq),
                  outproj=min(512, seq))


def kernel(x_prompt, x_sample, cache_kv_latent, cache_k_rope, state_ssm, state_conv, norm_in_w, w_in, conv_w,
           conv_b, dt_bias, a_log, d_skip, ssm_norm_w, w_ssm_out, q_norm_w, w_q_up, kv_norm_w, w_kv_up, w_mla_out,
           w_out, final_norm_w):
    depth = w_in.shape[0]
    assert depth == 1 and x_prompt.shape[0] == 1
    seq = x_prompt.shape[1]
    nb, t_new = x_sample.shape[0], x_sample.shape[1]
    past = cache_kv_latent.shape[2]
    q_prompt, q_sample = min(CHUNK, seq), min(CHUNK, t_new)
    p = _prepare(norm_in_w[0], w_in[0], conv_w[0], conv_b[0], dt_bias[0], a_log[0], d_skip[0], ssm_norm_w[0],
                 w_ssm_out[0], q_norm_w[0], w_q_up[0], kv_norm_w[0], w_kv_up[0], w_mla_out[0], w_out[0],
                 final_norm_w, sorted({q_prompt, q_sample}))

    t = _tiles(seq, q_prompt)
    xp = x_prompt[0]
    proj = _inproj(xp, p["norm_in_w"], p["w_main"], p["w_tail"], tm=t.inproj)
    y_norm, ssm_p, conv_p = _ssd(proj[None],
                                 jnp.zeros((1, CONV_WIDTH - 1, CONV_DIM), F32),
                                 jnp.zeros((1, SSM_INNER, SSM_STATE), F32), p, q_prompt, nsub=t.ssd_chunks)
    ct, sn = _rope_tables(jnp.arange(seq))
    ckv_p, kr_p, qt, kn, krb, vt = _mla_prompt(proj, ct, sn, p, tm=t.prep)
    attn = _attention_prompt(qt, kn, krb, vt, tq=t.attn_q)
    mixed = _merge(y_norm[0], attn, proj, p, tm=t.merge)
    y_prompt = _outproj(xp, mixed, p, tm=t.outproj)

    rows_s = nb * t_new
    xs = x_sample.reshape(rows_s, D_MODEL)
    proj_s = _inproj(xs, p["norm_in_w"], p["w_main"], p["w_tail"], tm=rows_s)
    y_norm_s, ssm_s, conv_s = _ssd(proj_s.reshape(nb, t_new, PROJ_COLS), state_conv[0],
                                   state_ssm[0].reshape(nb, SSM_INNER, SSM_STATE), p, q_sample, nsub=1)
    ct_s, sn_s = _rope_tables(past + jnp.arange(t_new))
    ckv_s, kr_s, ql, qr = _mla_sample(proj_s, jnp.tile(ct_s, (nb, 1)), jnp.tile(sn_s, (nb, 1)), p)
    ckv_s = ckv_s.reshape(nb, t_new, KV_LORA)
    kr_s = kr_s.reshape(nb, t_new, QK_ROPE)
    attn_s = _attention_cached(ql, qr, cache_kv_latent[0], cache_k_rope[0], ckv_s, kr_s, p["w_uv"])
    mixed_s = _merge(y_norm_s.reshape(rows_s, SSM_INNER), attn_s.reshape(rows_s, MLA_INNER), proj_s, p, tm=rows_s)
    y_sample = _outproj(xs, mixed_s, p, tm=rows_s).reshape(nb, t_new, D_MODEL)

    hshape = (SSM_HEADS, SSM_HEAD_DIM, SSM_STATE)
    return (y_prompt[None], y_sample,
            ckv_p[None, None], kr_p[None, None], ssm_p.reshape((1, 1) + hshape), conv_p[None],
            ckv_s[None], kr_s[None], ssm_s.reshape((1, nb) + hshape), conv_s[None])
```
